```python
import jax, jax.numpy as jnp
from jax import lax
import numpy as np

D_MODEL = 1024
BATCH = 8
SEQ = 4096
DEPTH = 2

N_EVEN = (DEPTH + 1) // 2
N_ODD = DEPTH // 2
EPS = 1e-6
A_WIDTH = D_MODEL // 2
CONV_WIDTH = 3
B_WIDTH = D_MODEL // 2
HG_HEADS = 4
HG_DK = B_WIDTH // HG_HEADS
HG_DV = B_WIDTH // HG_HEADS
HG_CHUNK = 64
IN0_SIZES = (A_WIDTH, A_WIDTH, A_WIDTH, B_WIDTH, B_WIDTH, B_WIDTH, B_WIDTH)
IN0_COLS = sum(IN0_SIZES)
IN0_SPLITS = tuple(int(s) for s in np.cumsum(IN0_SIZES)[:-1])
GM_WIDTH = D_MODEL
GM_GROUPS = 4
GM_CHUNK = 128
D_FF = 4 * D_MODEL

kernel_name = "hybrid_conv_hgrn2_gmlp_adaln"


def rmsnorm(x, g):
    xf = x.astype(jnp.float32)
    inv = lax.rsqrt(jnp.mean(xf * xf, axis=-1, keepdims=True) + EPS)
    return (xf * inv).astype(x.dtype) * g


def layernorm(x, g, b):
    xf = x.astype(jnp.float32)
    mu = jnp.mean(xf, axis=-1, keepdims=True)
    xc = xf - mu
    inv = lax.rsqrt(jnp.mean(xc * xc, axis=-1, keepdims=True) + EPS)
    return (xc * inv).astype(x.dtype) * g + b


def ada_modulation(c, w, b):
    m = jnp.einsum('bd,de->be', jax.nn.silu(c), w) + b
    return jnp.split(m[:, None, :], 6, axis=-1)


def short_conv_mixer(gate_b, gate_c, h, w_conv, b_conv):
    z = gate_c * h
    s = z.shape[1]
    zp = jnp.pad(z, ((0, 0), (CONV_WIDTH - 1, 0), (0, 0)))
    conv = b_conv
    for tap in range(CONV_WIDTH):
        conv = conv + zp[:, tap:tap + s, :] * w_conv[tap]
    return gate_b * conv


def hgrn2_mixer(q, f_logit, i, g, lower_bound, gain):
    f32 = jnp.float32
    bsz, s, _ = q.shape
    n_chunks = s // HG_CHUNK
    lb = lower_bound.astype(f32)
    f = lb + (1.0 - lb) * jax.nn.sigmoid(f_logit.astype(f32))
    log_f = jnp.log(f)
    k = 1.0 - f

    def to_chunks(t, d):
        return t.reshape(bsz, n_chunks, HG_CHUNK, HG_HEADS, d).transpose(1, 0, 3, 2, 4)

    qc = to_chunks(q.astype(f32), HG_DK)
    kc = to_chunks(k, HG_DK)
    lfc = to_chunks(log_f, HG_DK)
    vc = to_chunks(i.astype(f32), HG_DV)
    causal = jnp.tril(jnp.ones((HG_CHUNK, HG_CHUNK), dtype=bool))[:, :, None]

    def step(state, inp):
        qb, kb, lfb, vb = inp
        cum = jnp.cumsum(lfb, axis=2)
        o_inter = jnp.einsum('bhck,bhkv->bhcv', qb * jnp.exp(cum), state)
        diff = cum[:, :, :, None, :] - cum[:, :, None, :, :]
        decay = jnp.exp(jnp.where(causal, diff, -jnp.inf))
        scores = jnp.einsum('bhtk,bhtsk,bhsk->bhts', qb, decay, kb)
        o_intra = jnp.einsum('bhts,bhsv->bhtv', scores, vb)
        last = cum[:, :, -1:, :]
        k_dec = kb * jnp.exp(last - cum)
        new_state = (jnp.exp(last[:, :, 0, :])[..., None] * state
                     + jnp.einsum('bhck,bhcv->bhkv', k_dec, vb))
        return new_state, o_inter + o_intra

    s0 = jnp.zeros((bsz, HG_HEADS, HG_DK, HG_DV), f32)
    _, o = lax.scan(step, s0, (qc, kc, lfc, vc))
    o = o.transpose(1, 0, 3, 2, 4).reshape(bsz, s, HG_HEADS, HG_DV)
    o = o * lax.rsqrt(jnp.mean(o * o, axis=-1, keepdims=True) + EPS)
    o = o.reshape(bsz, s, HG_HEADS * HG_DV) * gain.astype(f32) * jax.nn.silu(g.astype(f32))
    return o.astype(q.dtype)


def spatial_gating_mixer(z, ln_g, ln_b, w_s, b_s):
    u, v = jnp.split(z, 2, axis=-1)
    v = layernorm(v, ln_g, ln_b)
    bsz, s, e = v.shape
    vc = v.reshape(bsz, s // GM_CHUNK, GM_CHUNK, GM_GROUPS, e // GM_GROUPS)
    mask = jnp.tril(jnp.ones((GM_CHUNK, GM_CHUNK), dtype=w_s.dtype))
    w = w_s * mask
    mixed = jnp.einsum('gts,bnsgd->bntgd', w, vc) + b_s.T[None, None, :, :, None]
    return u * mixed.reshape(bsz, s, e)


def setup_inputs(seed: int = 0) -> dict:
    key = jax.random.key(seed)
    ks = jax.random.split(key, 24)
    nrm = jax.random.normal
    D = D_MODEL
    inp = {}
    inp["x"] = nrm(ks[0], (BATCH, SEQ, D), jnp.float32)
    inp["c"] = nrm(ks[1], (BATCH, D), jnp.float32)
    inp["ada_w"] = nrm(ks[2], (DEPTH, D, 6 * D), jnp.float32) * D ** -0.5
    inp["ada_b"] = nrm(ks[3], (DEPTH, 6 * D), jnp.float32) * 0.02
    inp["norm_mix_g"] = 1.0 + 0.02 * nrm(ks[4], (DEPTH, D), jnp.float32)
    inp["norm_ffn_g"] = 1.0 + 0.02 * nrm(ks[5], (DEPTH, D), jnp.float32)
    inp["w_in0"] = nrm(ks[6], (N_EVEN, D, IN0_COLS), jnp.float32) * D ** -0.5
    inp["conv_w"] = nrm(ks[7], (N_EVEN, CONV_WIDTH, A_WIDTH), jnp.float32) * CONV_WIDTH ** -0.5
    inp["conv_b"] = 0.02 * nrm(ks[8], (N_EVEN, A_WIDTH), jnp.float32)
    inp["hg_lb"] = 0.5 * nrm(ks[9], (DEPTH + 1, B_WIDTH), jnp.float32)
    inp["hg_gain"] = 1.0 + 0.02 * nrm(ks[10], (N_EVEN, B_WIDTH), jnp.float32)
    inp["w_out0"] = nrm(ks[11], (N_EVEN, A_WIDTH + B_WIDTH, D), jnp.float32) * (A_WIDTH + B_WIDTH) ** -0.5
    inp["w_in1"] = nrm(ks[12], (N_ODD, D, 2 * GM_WIDTH), jnp.float32) * D ** -0.5
    inp["b_in1"] = 0.02 * nrm(ks[13], (N_ODD, 2 * GM_WIDTH), jnp.float32)
    inp["gm_ln_g"] = 1.0 + 0.02 * nrm(ks[14], (N_ODD, GM_WIDTH), jnp.float32)
    inp["gm_ln_b"] = 0.02 * nrm(ks[15], (N_ODD, GM_WIDTH), jnp.float32)
    inp["gm_ws"] = nrm(ks[16], (N_ODD, GM_GROUPS, GM_CHUNK, GM_CHUNK), jnp.float32) * GM_CHUNK ** -0.5
    inp["gm_bs"] = 1.0 + 0.02 * nrm(ks[17], (N_ODD, GM_GROUPS, GM_CHUNK), jnp.float32)
    inp["w_out1"] = nrm(ks[18], (N_ODD, GM_WIDTH, D), jnp.float32) * GM_WIDTH ** -0.5
    inp["w_ff1"] = nrm(ks[19], (DEPTH, D, D_FF), jnp.float32) * D ** -0.5
    inp["w_ff2"] = nrm(ks[20], (DEPTH, D_FF, D), jnp.float32) * D_FF ** -0.5
    inp["final_g"] = 1.0 + 0.02 * nrm(ks[21], (D,), jnp.float32)
    return inp


def reference(x, c, ada_w, ada_b, norm_mix_g, norm_ffn_g, w_in0, conv_w, conv_b,
              hg_lb, hg_gain, w_out0, w_in1, b_in1, gm_ln_g, gm_ln_b, gm_ws, gm_bs,
              w_out1, w_ff1, w_ff2, final_g):
    lower_bounds = jnp.cumsum(jax.nn.softmax(hg_lb.astype(jnp.float32), axis=0), axis=0)
    for layer in range(DEPTH):
        sh1, sc1, g1, sh2, sc2, g2 = ada_modulation(c, ada_w[layer], ada_b[layer])
        h = rmsnorm(x, norm_mix_g[layer]) * (1.0 + sc1) + sh1
        j = layer // 2
        if layer % 2 == 0:
            p = jnp.einsum('bsd,de->bse', h, w_in0[j])
            a_b, a_c, a_h, b_q, b_f, b_i, b_g = jnp.split(p, IN0_SPLITS, axis=-1)
            y_a = short_conv_mixer(a_b, a_c, a_h, conv_w[j], conv_b[j])
            y_b = hgrn2_mixer(b_q, b_f, b_i, b_g, lower_bounds[layer], hg_gain[j])
            y = jnp.einsum('bse,ed->bsd', jnp.concatenate([y_a, y_b], axis=-1), w_out0[j])
        else:
            z = jax.nn.gelu(jnp.einsum('bsd,de->bse', h, w_in1[j]) + b_in1[j])
            y = spatial_gating_mixer(z, gm_ln_g[j], gm_ln_b[j], gm_ws[j], gm_bs[j])
            y = jnp.einsum('bse,ed->bsd', y, w_out1[j])
        x = x + g1 * y
        h = rmsnorm(x, norm_ffn_g[layer]) * (1.0 + sc2) + sh2
        hid = jnp.square(jax.nn.relu(jnp.einsum('bsd,df->bsf', h, w_ff1[layer])))
        x = x + g2 * jnp.einsum('bsf,fd->bsd', hid, w_ff2[layer])
    return rmsnorm(x, final_g)
```

```python
import functools

import numpy as np
import jax
import jax.numpy as jnp
from jax import lax
from jax.experimental import pallas as pl
from jax.experimental.pallas import tpu as pltpu

F32 = jnp.float32
BF16 = jnp.bfloat16

EPS = 1e-6
D_MODEL = 1024
D_FF = 4 * D_MODEL
A_WIDTH = 512
B_WIDTH = 512
HG_HEADS = 4
HG_DK = 128
HG_DV = 128
IN0_COLS = 3 * A_WIDTH + 4 * B_WIDTH
GM_WIDTH = D_MODEL
GM_GROUPS = 4
GM_CHUNK = 128
GM_GW = GM_WIDTH // GM_GROUPS

SUBLANES = 8
LANES = 128
VMEM_LIMIT_BYTES = 56 * 1024 * 1024

SEQ_TILE = 512
ROW_BLOCK = 64
HG_CHUNK = 64
HG_LEVELS = (32, 16, 8)
HG_DIRECT = 8

_SQRT_2_OVER_PI = float(np.sqrt(2.0 / np.pi))


def _sigmoid(v):
    return 1.0 / (1.0 + jnp.exp(-v))


def _dot(a, b):
    return jnp.dot(a, b, preferred_element_type=F32)


def _dot_nt(a, b):
    return lax.dot_general(a, b, (((1,), (1,)), ((), ())), preferred_element_type=F32)


def _dot_tn(a, b):
    return lax.dot_general(a, b, (((0,), (0,)), ((), ())), preferred_element_type=F32)


def _norm_mod_to(x_ref, hb_ref, gs, sh, rows):
    def body(i, carry):
        r = pl.multiple_of(i * ROW_BLOCK, ROW_BLOCK)
        xt = x_ref[pl.ds(r, ROW_BLOCK), :]
        inv = lax.rsqrt(jnp.mean(xt * xt, axis=-1, keepdims=True) + EPS)
        hb_ref[pl.ds(r, ROW_BLOCK), :] = ((xt * inv) * gs + sh).astype(BF16)
        return carry

    lax.fori_loop(0, rows // ROW_BLOCK, body, 0)


def _ada_kernel(c_ref, w_ref, b_ref, o_ref):
    c = c_ref[...]
    s = c * _sigmoid(c)
    o_ref[...] = jnp.dot(s, w_ref[...], preferred_element_type=F32,
                         precision=lax.Precision.HIGHEST) + b_ref[...]


def _ada_call(c, ada_w, ada_b):
    depth, d, e = ada_w.shape
    bsz = c.shape[0]
    tn = 1536
    return pl.pallas_call(
        _ada_kernel,
        grid=(depth, e // tn),
        in_specs=[
            pl.BlockSpec((bsz, d), lambda l, j: (0, 0)),
            pl.BlockSpec((None, d, tn), lambda l, j: (l, 0, j)),
            pl.BlockSpec((None, 1, tn), lambda l, j: (l, 0, j)),
        ],
        out_specs=pl.BlockSpec((None, bsz, tn), lambda l, j: (l, 0, j)),
        out_shape=jax.ShapeDtypeStruct((depth, bsz, e), F32),
        compiler_params=pltpu.CompilerParams(
            dimension_semantics=("arbitrary", "arbitrary"),
            vmem_limit_bytes=VMEM_LIMIT_BYTES),
        name="ada_modulation",
    )(c, ada_w, ada_b.reshape(depth, 1, e))


def _level_masks():
    ti = lax.broadcasted_iota(jnp.int32, (HG_CHUNK, HG_CHUNK), 0)
    si = lax.broadcasted_iota(jnp.int32, (HG_CHUNK, HG_CHUNK), 1)
    masks = []
    for m in HG_LEVELS:
        other_blk = (ti ^ si) & ~(2 * m - 1)
        masks.append((other_blk | ((ti & m) ^ m) | (si & m)) == 0)
    return masks


def _mix0_kernel(x_ref, mod_ref, ng_ref, win_ref, cw_ref, cb_ref, hglb_ref, gain_ref,
                 wout_ref, o_ref, hb_ref, p_ref, y_ref, st_ref, zc_ref, *, ts):
    @pl.when(pl.program_id(1) == 0)
    def _():
        st_ref[...] = jnp.zeros_like(st_ref)
        zc_ref[...] = jnp.zeros_like(zc_ref)

    sh1 = mod_ref[0:1, :]
    sc1 = mod_ref[1:2, :]
    g1 = mod_ref[2:3, :]
    _norm_mod_to(x_ref, hb_ref, ng_ref[...] * (1.0 + sc1), sh1, ts)

    p_ref[...] = _dot(hb_ref[...], win_ref[...])

    a0, a1, a2 = hglb_ref[0:1, :], hglb_ref[1:2, :], hglb_ref[2:3, :]
    amax = jnp.maximum(jnp.maximum(a0, a1), a2)
    e0, e1, e2 = jnp.exp(a0 - amax), jnp.exp(a1 - amax), jnp.exp(a2 - amax)
    lb = e0 / (e0 + e1 + e2)

    cw0, cw1, cw2 = cw_ref[0:1, :], cw_ref[1:2, :], cw_ref[2:3, :]
    cb = cb_ref[...]
    gain = gain_ref[...]
    row = lax.broadcasted_iota(jnp.int32, (HG_CHUNK, B_WIDTH), 0)
    row8 = lax.broadcasted_iota(jnp.int32, (SUBLANES, A_WIDTH), 0)
    in_blk = row % HG_DIRECT
    masks = _level_masks()

    def chunk_body(ci, carry):
        r = pl.multiple_of(ci * HG_CHUNK, HG_CHUNK)

        def col(j):
            return p_ref[pl.ds(r, HG_CHUNK), j * 512:(j + 1) * 512]

        z = col(1) * col(2)
        prev = zc_ref[...]
        zc_ref[...] = z[HG_CHUNK - SUBLANES:, :]

        def shifted(k):
            zr = pltpu.roll(z, k, axis=0)
            pr = pltpu.roll(prev, k, axis=0)
            head = jnp.where(row8 < k, pr, zr[:SUBLANES])
            return jnp.concatenate([head, zr[SUBLANES:]], axis=0)

        conv = cb + shifted(2) * cw0
        conv = conv + shifted(1) * cw1
        conv = conv + z * cw2
        y_ref[pl.ds(r, HG_CHUNK), 0:A_WIDTH] = (col(0) * conv).astype(BF16)

        q = col(3)
        f = lb + (1.0 - lb) * _sigmoid(col(4))
        lf = jnp.log(f)
        kk = 1.0 - f
        v = col(5)
        gate = col(6)

        cum = lf
        sh = 1
        while sh < HG_CHUNK:
            cum = cum + jnp.where(row >= sh, pltpu.roll(cum, sh, axis=0), 0.0)
            sh *= 2
        last = cum[HG_CHUNK - 1:HG_CHUNK, :]

        direct = [None] * HG_HEADS
        for dlt in range(HG_DIRECT):
            if dlt == 0:
                w = q * kk
                vs = v
            else:
                cum_s = pltpu.roll(cum, dlt, axis=0)
                kk_s = pltpu.roll(kk, dlt, axis=0)
                vs = pltpu.roll(v, dlt, axis=0)
                w = jnp.where(in_blk >= dlt, q * kk_s * jnp.exp(cum - cum_s), 0.0)
            for h in range(HG_HEADS):
                hs = slice(h * HG_DK, (h + 1) * HG_DK)
                t = jnp.sum(w[:, hs], axis=-1, keepdims=True) * vs[:, hs]
                direct[h] = t if direct[h] is None else direct[h] + t

        q_in = q * jnp.exp(cum)
        k_dec = kk * jnp.exp(last - cum)
        dec = jnp.exp(last)
        gsil = gate * _sigmoid(gate)

        for h in range(HG_HEADS):
            hs = slice(h * HG_DK, (h + 1) * HG_DK)
            q_h, k_h, cum_h = q[:, hs], kk[:, hs], cum[:, hs]
            v_b = v[:, hs].astype(BF16)
            st = st_ref[h]
            o = _dot_nt(q_in[:, hs].astype(BF16), st.astype(BF16))

            scores = None
            for m, mask in zip(HG_LEVELS, masks):
                blk = 2 * m
                bnd = jnp.concatenate(
                    [jnp.broadcast_to(cum_h[b * blk + m - 1:b * blk + m, :], (blk, HG_DK))
                     for b in range(HG_CHUNK // blk)], axis=0)
                qt = q_h * jnp.exp(jnp.minimum(cum_h - bnd, 0.0))
                kt = k_h * jnp.exp(jnp.minimum(bnd - cum_h, 0.0))
                s_l = jnp.where(mask, _dot_nt(qt.astype(BF16), kt.astype(BF16)), 0.0)
                scores = s_l if scores is None else scores + s_l
            o = o + _dot(scores.astype(BF16), v_b) + direct[h]

            st_ref[h] = st * dec[:, hs] + _dot_tn(v_b, k_dec[:, hs].astype(BF16))

            o = o * lax.rsqrt(jnp.mean(o * o, axis=-1, keepdims=True) + EPS)
            y_ref[pl.ds(r, HG_CHUNK), A_WIDTH + h * HG_DV:A_WIDTH + (h + 1) * HG_DV] = (
                o * gain[:, hs] * gsil[:, hs]).astype(BF16)
        return carry

    lax.fori_loop(0, ts // HG_CHUNK, chunk_body, 0)

    o_ref[...] = x_ref[...] + g1 * _dot(y_ref[...], wout_ref[...])


def _mix0_call(x, mod, ng, win, cw, cb, hglb, gain, wout, *, ts):
    bsz, seq, d = x.shape
    const2 = lambda b, s: (0, 0)
    return pl.pallas_call(
        functools.partial(_mix0_kernel, ts=ts),
        grid=(bsz, seq // ts),
        in_specs=[
            pl.BlockSpec((None, ts, d), lambda b, s: (b, s, 0)),
            pl.BlockSpec((None, 6, d), lambda b, s: (b, 0, 0)),
            pl.BlockSpec((1, d), const2),
            pl.BlockSpec((d, IN0_COLS), const2),
            pl.BlockSpec((3, A_WIDTH), const2),
            pl.BlockSpec((1, A_WIDTH), const2),
            pl.BlockSpec((3, B_WIDTH), const2),
            pl.BlockSpec((1, B_WIDTH), const2),
            pl.BlockSpec((d, d), const2),
        ],
        out_specs=pl.BlockSpec((None, ts, d), lambda b, s: (b, s, 0)),
        out_shape=jax.ShapeDtypeStruct((bsz, seq, d), F32),
        scratch_shapes=[
            pltpu.VMEM((ts, d), BF16),
            pltpu.VMEM((ts, IN0_COLS), F32),
            pltpu.VMEM((ts, d), BF16),
            pltpu.VMEM((HG_HEADS, HG_DV, HG_DK), F32),
            pltpu.VMEM((SUBLANES, A_WIDTH), F32),
        ],
        compiler_params=pltpu.CompilerParams(
            dimension_semantics=("parallel", "arbitrary"),
            vmem_limit_bytes=VMEM_LIMIT_BYTES),
        name="mixer_conv_hgrn2",
    )(x, mod, ng, win, cw, cb, hglb, gain, wout)


def _mix1_kernel(x_ref, mod_ref, ng_ref, win_ref, bin_ref, lng_ref, lnb_ref, ws_ref,
                 bst_ref, wout_ref, o_ref, hb_ref, z_ref, y_ref, *, ts):
    sh1 = mod_ref[0:1, :]
    sc1 = mod_ref[1:2, :]
    g1 = mod_ref[2:3, :]
    _norm_mod_to(x_ref, hb_ref, ng_ref[...] * (1.0 + sc1), sh1, ts)

    z_ref[...] = _dot(hb_ref[...], win_ref[...])

    b_in = bin_ref[...]
    ln_g = lng_ref[...]
    ln_b = lnb_ref[...]
    bst = bst_ref[...]
    ti = lax.broadcasted_iota(jnp.int32, (GM_CHUNK, GM_CHUNK), 0)
    si = lax.broadcasted_iota(jnp.int32, (GM_CHUNK, GM_CHUNK), 1)
    w_tri = [jnp.where(si <= ti, ws_ref[g], 0.0).astype(BF16) for g in range(GM_GROUPS)]

    def gelu(t):
        return t * (0.5 * (1.0 + jnp.tanh(_SQRT_2_OVER_PI * (t + 0.044715 * (t * t * t)))))

    def chunk_body(ci, carry):
        r = pl.multiple_of(ci * GM_CHUNK, GM_CHUNK)
        u = gelu(z_ref[pl.ds(r, GM_CHUNK), 0:GM_WIDTH] + b_in[:, 0:GM_WIDTH])
        v = gelu(z_ref[pl.ds(r, GM_CHUNK), GM_WIDTH:2 * GM_WIDTH] + b_in[:, GM_WIDTH:])
        mu = jnp.mean(v, axis=-1, keepdims=True)
        vc = v - mu
        inv = lax.rsqrt(jnp.mean(vc * vc, axis=-1, keepdims=True) + EPS)
        vn = ((vc * inv) * ln_g + ln_b).astype(BF16)
        for g in range(GM_GROUPS):
            gs = slice(g * GM_GW, (g + 1) * GM_GW)
            mixed = _dot(w_tri[g], vn[:, gs]) + bst[:, g:g + 1]
            y_ref[pl.ds(r, GM_CHUNK), gs] = (u[:, gs] * mixed).astype(BF16)
        return carry

    lax.fori_loop(0, ts // GM_CHUNK, chunk_body, 0)

    o_ref[...] = x_ref[...] + g1 * _dot(y_ref[...], wout_ref[...])


def _mix1_call(x, mod, ng, win, b_in, ln_g, ln_b, ws, bst, wout, *, ts):
    bsz, seq, d = x.shape
    const2 = lambda b, s: (0, 0)
    return pl.pallas_call(
        functools.partial(_mix1_kernel, ts=ts),
        grid=(bsz, seq // ts),
        in_specs=[
            pl.BlockSpec((None, ts, d), lambda b, s: (b, s, 0)),
            pl.BlockSpec((None, 6, d), lambda b, s: (b, 0, 0)),
            pl.BlockSpec((1, d), const2),
            pl.BlockSpec((d, 2 * GM_WIDTH), const2),
            pl.BlockSpec((1, 2 * GM_WIDTH), const2),
            pl.BlockSpec((1, GM_WIDTH), const2),
            pl.BlockSpec((1, GM_WIDTH), const2),
            pl.BlockSpec((GM_GROUPS, GM_CHUNK, GM_CHUNK), lambda b, s: (0, 0, 0)),
            pl.BlockSpec((GM_CHUNK, GM_GROUPS), const2),
            pl.BlockSpec((GM_WIDTH, d), const2),
        ],
        out_specs=pl.BlockSpec((None, ts, d), lambda b, s: (b, s, 0)),
        out_shape=jax.ShapeDtypeStruct((bsz, seq, d), F32),
        scratch_shapes=[
            pltpu.VMEM((ts, d), BF16),
            pltpu.VMEM((ts, 2 * GM_WIDTH), F32),
            pltpu.VMEM((ts, GM_WIDTH), BF16),
        ],
        compiler_params=pltpu.CompilerParams(
            dimension_semantics=("parallel", "parallel"),
            vmem_limit_bytes=VMEM_LIMIT_BYTES),
        name="mixer_spatial_gating",
    )(x, mod, ng, win, b_in, ln_g, ln_b, ws, bst, wout)


def _ffn_kernel(x_ref, mod_ref, ng_ref, w1_ref, w2_ref, fg_ref, o_ref, hb_ref, *, ts, final):
    sh2 = mod_ref[3:4, :]
    sc2 = mod_ref[4:5, :]
    g2 = mod_ref[5:6, :]
    _norm_mod_to(x_ref, hb_ref, ng_ref[...] * (1.0 + sc2), sh2, ts)

    acc = None
    fcols = D_MODEL
    for j in range(D_FF // fcols):
        hid = jnp.maximum(_dot(hb_ref[...], w1_ref[:, j * fcols:(j + 1) * fcols]), 0.0)
        part = _dot((hid * hid).astype(BF16), w2_ref[j * fcols:(j + 1) * fcols, :])
        acc = part if acc is None else acc + part

    out = x_ref[...] + g2 * acc
    if final:
        inv = lax.rsqrt(jnp.mean(out * out, axis=-1, keepdims=True) + EPS)
        out = (out * inv) * fg_ref[...]
    o_ref[...] = out


def _ffn_call(x, mod, ng, w1, w2, fg, *, ts, final):
    bsz, seq, d = x.shape
    const2 = lambda b, s: (0, 0)
    return pl.pallas_call(
        functools.partial(_ffn_kernel, ts=ts, final=final),
        grid=(bsz, seq // ts),
        in_specs=[
            pl.BlockSpec((None, ts, d), lambda b, s: (b, s, 0)),
            pl.BlockSpec((None, 6, d), lambda b, s: (b, 0, 0)),
            pl.BlockSpec((1, d), const2),
            pl.BlockSpec((d, D_FF), const2, pipeline_mode=pl.Buffered(1)),
            pl.BlockSpec((D_FF, d), const2, pipeline_mode=pl.Buffered(1)),
            pl.BlockSpec((1, d), const2),
        ],
        out_specs=pl.BlockSpec((None, ts, d), lambda b, s: (b, s, 0)),
        out_shape=jax.ShapeDtypeStruct((bsz, seq, d), F32),
        scratch_shapes=[pltpu.VMEM((ts, d), BF16)],
        compiler_params=pltpu.CompilerParams(
            dimension_semantics=("parallel", "parallel"),
            vmem_limit_bytes=VMEM_LIMIT_BYTES),
        name="ffn_final" if final else "ffn",
    )(x, mod, ng, w1, w2, fg)


def kernel(x, c, ada_w, ada_b, norm_mix_g, norm_ffn_g, w_in0, conv_w, conv_b, hg_lb, hg_gain,
           w_out0, w_in1, b_in1, gm_ln_g, gm_ln_b, gm_ws, gm_bs, w_out1, w_ff1, w_ff2, final_g):
    bsz, seq, d = x.shape
    ts = min(SEQ_TILE, seq)
    mod = _ada_call(c, ada_w, ada_b).reshape(ada_w.shape[0], bsz, 6, d)
    fg = final_g.reshape(1, d)

    x = _mix0_call(x, mod[0], norm_mix_g[0:1], w_in0[0].astype(BF16), conv_w[0], conv_b[0:1],
                   hg_lb, hg_gain[0:1], w_out0[0].astype(BF16), ts=ts)
    x = _ffn_call(x, mod[0], norm_ffn_g[0:1], w_ff1[0].astype(BF16), w_ff2[0].astype(BF16), fg,
                  ts=ts, final=False)
    x = _mix1_call(x, mod[1], norm_mix_g[1:2], w_in1[0].astype(BF16), b_in1[0:1], gm_ln_g[0:1],
                   gm_ln_b[0:1], gm_ws[0], gm_bs[0].T, w_out1[0].astype(BF16), ts=ts)
    x = _ffn_call(x, mod[1], norm_ffn_g[1:2], w_ff1[1].astype(BF16), w_ff2[1].astype(BF16), fg,
                  ts=ts, final=True)
    return x
```

```python
import functools

import numpy as np
import jax
import jax.numpy as jnp
from jax import lax
from jax.experimental import pallas as pl
from jax.experimental.pallas import tpu as pltpu

F32 = jnp.float32
BF16 = jnp.bfloat16

EPS = 1e-6
D_MODEL = 1024
D_FF = 4 * D_MODEL
A_WIDTH = 512
B_WIDTH = 512
HG_HEADS = 4
HG_DK = 128
HG_DV = 128
IN0_COLS = 3 * A_WIDTH + 4 * B_WIDTH
GM_WIDTH = D_MODEL
GM_GROUPS = 4
GM_CHUNK = 128
GM_GW = GM_WIDTH // GM_GROUPS

SUBLANES = 8
LANES = 128
VMEM_LIMIT_BYTES = 56 * 1024 * 1024

SEQ_TILE = 512
ROW_BLOCK = 64
HG_CHUNK = 64
HG_LEVELS = (32, 16, 8, 4, 2, 1)
HG_UNROLL = 8

_SQRT_2_OVER_PI = float(np.sqrt(2.0 / np.pi))


def _sigmoid(v):
    return 1.0 / (1.0 + jnp.exp(-v))


def _dot(a, b):
    return jnp.dot(a, b, preferred_element_type=F32)


def _dot_nt(a, b):
    return lax.dot_general(a, b, (((1,), (1,)), ((), ())), preferred_element_type=F32)


def _dot_tn(a, b):
    return lax.dot_general(a, b, (((0,), (0,)), ((), ())), preferred_element_type=F32)


def _norm_mod_to(x_ref, hb_ref, gs, sh, rows):
    def body(i, carry):
        r = pl.multiple_of(i * ROW_BLOCK, ROW_BLOCK)
        xt = x_ref[pl.ds(r, ROW_BLOCK), :]
        inv = lax.rsqrt(jnp.mean(xt * xt, axis=-1, keepdims=True) + EPS)
        hb_ref[pl.ds(r, ROW_BLOCK), :] = ((xt * inv) * gs + sh).astype(BF16)
        return carry

    lax.fori_loop(0, rows // ROW_BLOCK, body, 0)


def _ada_kernel(c_ref, w_ref, b_ref, o_ref):
    c = c_ref[...]
    s = c * _sigmoid(c)
    o_ref[...] = jnp.dot(s, w_ref[...], preferred_element_type=F32,
                         precision=lax.Precision.HIGHEST) + b_ref[...]


def _ada_call(c, ada_w, ada_b):
    depth, d, e = ada_w.shape
    bsz = c.shape[0]
    tn = 1536
    return pl.pallas_call(
        _ada_kernel,
        grid=(depth, e // tn),
        in_specs=[
            pl.BlockSpec((bsz, d), lambda l, j: (0, 0)),
            pl.BlockSpec((None, d, tn), lambda l, j: (l, 0, j)),
            pl.BlockSpec((None, 1, tn), lambda l, j: (l, 0, j)),
        ],
        out_specs=pl.BlockSpec((None, bsz, tn), lambda l, j: (l, 0, j)),
        out_shape=jax.ShapeDtypeStruct((depth, bsz, e), F32),
        compiler_params=pltpu.CompilerParams(
            dimension_semantics=("arbitrary", "arbitrary"),
            vmem_limit_bytes=VMEM_LIMIT_BYTES),
        name="ada_modulation",
    )(c, ada_w, ada_b.reshape(depth, 1, e))


def _level_masks():
    ti = lax.broadcasted_iota(jnp.int32, (HG_CHUNK, HG_CHUNK), 0)
    si = lax.broadcasted_iota(jnp.int32, (HG_CHUNK, HG_CHUNK), 1)
    masks = []
    for m in HG_LEVELS:
        other_blk = (ti ^ si) & ~(2 * m - 1)
        masks.append((other_blk | ((ti & m) ^ m) | (si & m)) == 0)
    return masks, ti == si


def _block_boundary(c, m):
    rows, w = c.shape
    if m >= SUBLANES:
        blk = 2 * m
        return jnp.concatenate(
            [jnp.broadcast_to(c[b * blk + m - 1:b * blk + m, :], (blk, w))
             for b in range(rows // blk)], axis=0)
    c3 = c.reshape(rows // SUBLANES, SUBLANES, w)
    if m == 4:
        bnd = jnp.broadcast_to(c3[:, 3:4, :], c3.shape)
    else:
        assert m == 2
        sub = lax.broadcasted_iota(jnp.int32, c3.shape, 1)
        bnd = jnp.where(sub < 4, jnp.broadcast_to(c3[:, 1:2, :], c3.shape),
                        jnp.broadcast_to(c3[:, 5:6, :], c3.shape))
    return bnd.reshape(rows, w)


def _chunk_cumsum(x):
    rows, w = x.shape
    nv = rows // SUBLANES
    x3 = x.reshape(nv, SUBLANES, w)
    sub = lax.broadcasted_iota(jnp.int32, x3.shape, 1)
    sh = 1
    while sh < SUBLANES:
        x3 = x3 + jnp.where(sub >= sh, pltpu.roll(x3, sh, axis=1), 0.0)
        sh *= 2
    tot = jnp.broadcast_to(x3[:, SUBLANES - 1:SUBLANES, :], x3.shape)
    outs = [x3[0]]
    acc = tot[0]
    for j in range(1, nv):
        outs.append(x3[j] + acc)
        if j + 1 < nv:
            acc = acc + tot[j]
    return jnp.concatenate(outs, axis=0)


def _mix0_kernel(x_ref, mod_ref, ng_ref, win_ref, cw_ref, cb_ref, hglb_ref, gain_ref,
                 wout_ref, o_ref, hb_ref, p_ref, y_ref, st_ref, zc_ref, *, ts):
    @pl.when(pl.program_id(1) == 0)
    def _():
        st_ref[...] = jnp.zeros_like(st_ref)
        zc_ref[...] = jnp.zeros_like(zc_ref)

    sh1 = mod_ref[0:1, :]
    sc1 = mod_ref[1:2, :]
    g1 = mod_ref[2:3, :]
    _norm_mod_to(x_ref, hb_ref, ng_ref[...] * (1.0 + sc1), sh1, ts)

    p_ref[...] = _dot(hb_ref[...], win_ref[...])

    a0, a1, a2 = hglb_ref[0:1, :], hglb_ref[1:2, :], hglb_ref[2:3, :]
    amax = jnp.maximum(jnp.maximum(a0, a1), a2)
    e0, e1, e2 = jnp.exp(a0 - amax), jnp.exp(a1 - amax), jnp.exp(a2 - amax)
    lb = e0 / (e0 + e1 + e2)

    cw0, cw1, cw2 = cw_ref[0:1, :], cw_ref[1:2, :], cw_ref[2:3, :]
    cb = cb_ref[...]
    gain = gain_ref[...]
    row8 = lax.broadcasted_iota(jnp.int32, (SUBLANES, A_WIDTH), 0)
    masks, diag = _level_masks()

    def chunk_body(ci, carry):
        r = pl.multiple_of(ci * HG_CHUNK, HG_CHUNK)

        def col(j):
            return p_ref[pl.ds(r, HG_CHUNK), j * 512:(j + 1) * 512]

        z = col(1) * col(2)
        prev = zc_ref[...]
        zc_ref[...] = z[HG_CHUNK - SUBLANES:, :]

        def shifted(k):
            zr = pltpu.roll(z, k, axis=0)
            pr = pltpu.roll(prev, k, axis=0)
            head = jnp.where(row8 < k, pr, zr[:SUBLANES])
            return jnp.concatenate([head, zr[SUBLANES:]], axis=0)

        conv = cb + shifted(2) * cw0
        conv = conv + shifted(1) * cw1
        conv = conv + z * cw2
        y_ref[pl.ds(r, HG_CHUNK), 0:A_WIDTH] = (col(0) * conv).astype(BF16)

        q = col(3)
        f = lb + (1.0 - lb) * _sigmoid(col(4))
        kk = 1.0 - f
        v = col(5)
        gate = col(6)
        cum = _chunk_cumsum(jnp.log(f))
        ck = cum - jnp.log(kk)
        last = cum[HG_CHUNK - 1:HG_CHUNK, :]

        q_in = q * jnp.exp(cum)
        k_dec = jnp.exp(last - ck)
        dec = jnp.exp(last)
        qf = q * f
        gsil = gate * _sigmoid(gate)

        for h in range(HG_HEADS):
            hs = slice(h * HG_DK, (h + 1) * HG_DK)
            q_h, cum_h, ck_h = q[:, hs], cum[:, hs], ck[:, hs]
            k_b = kk[:, hs].astype(BF16)
            v_b = v[:, hs].astype(BF16)
            st = st_ref[h]
            o = _dot_nt(q_in[:, hs].astype(BF16), st.astype(BF16))

            scores = jnp.where(diag, _dot_nt(q_h.astype(BF16), k_b), 0.0)
            for m, mask in zip(HG_LEVELS, masks):
                if m == 1:
                    qt, kt = qf[:, hs].astype(BF16), k_b
                else:
                    bnd = _block_boundary(cum_h, m)
                    qt = (q_h * jnp.exp(cum_h - bnd)).astype(BF16)
                    kt = jnp.exp(bnd - ck_h).astype(BF16)
                scores = jnp.where(mask, _dot_nt(qt, kt), scores)
            o = o + _dot(scores.astype(BF16), v_b)

            st_ref[h] = st * dec[:, hs] + _dot_tn(v_b, k_dec[:, hs].astype(BF16))

            o = o * lax.rsqrt(jnp.mean(o * o, axis=-1, keepdims=True) + EPS)
            y_ref[pl.ds(r, HG_CHUNK), A_WIDTH + h * HG_DV:A_WIDTH + (h + 1) * HG_DV] = (
                o * gain[:, hs] * gsil[:, hs]).astype(BF16)
        return carry

    lax.fori_loop(0, ts // HG_CHUNK, chunk_body, 0, unroll=HG_UNROLL)

    o_ref[...] = x_ref[...] + g1 * _dot(y_ref[...], wout_ref[...])


def _mix0_call(x, mod, ng, win, cw, cb, hglb, gain, wout, *, ts):
    bsz, seq, d = x.shape
    const2 = lambda b, s: (0, 0)
    return pl.pallas_call(
        functools.partial(_mix0_kernel, ts=ts),
        grid=(bsz, seq // ts),
        in_specs=[
            pl.BlockSpec((None, ts, d), lambda b, s: (b, s, 0)),
            pl.BlockSpec((None, 6, d), lambda b, s: (b, 0, 0)),
            pl.BlockSpec((1, d), const2),
            pl.BlockSpec((d, IN0_COLS), const2),
            pl.BlockSpec((3, A_WIDTH), const2),
            pl.BlockSpec((1, A_WIDTH), const2),
            pl.BlockSpec((3, B_WIDTH), const2),
            pl.BlockSpec((1, B_WIDTH), const2),
            pl.BlockSpec((d, d), const2),
        ],
        out_specs=pl.BlockSpec((None, ts, d), lambda b, s: (b, s, 0)),
        out_shape=jax.ShapeDtypeStruct((bsz, seq, d), F32),
        scratch_shapes=[
            pltpu.VMEM((ts, d), BF16),
            pltpu.VMEM((ts, IN0_COLS), F32),
            pltpu.VMEM((ts, d), BF16),
            pltpu.VMEM((HG_HEADS, HG_DV, HG_DK), F32),
            pltpu.VMEM((SUBLANES, A_WIDTH), F32),
        ],
        compiler_params=pltpu.CompilerParams(
            dimension_semantics=("parallel", "arbitrary"),
            vmem_limit_bytes=VMEM_LIMIT_BYTES),
        name="mixer_conv_hgrn2",
    )(x, mod, ng, win, cw, cb, hglb, gain, wout)


def _mix1_kernel(x_ref, mod_ref, ng_ref, win_ref, bin_ref, lng_ref, lnb_ref, ws_ref,
                 bst_ref, wout_ref, o_ref, hb_ref, z_ref, y_ref, *, ts):
    sh1 = mod_ref[0:1, :]
    sc1 = mod_ref[1:2, :]
    g1 = mod_ref[2:3, :]
    _norm_mod_to(x_ref, hb_ref, ng_ref[...] * (1.0 + sc1), sh1, ts)

    z_ref[...] = _dot(hb_ref[...], win_ref[...])

    b_in = bin_ref[...]
    ln_g = lng_ref[...]
    ln_b = lnb_ref[...]
    bst = bst_ref[...]
    ti = lax.broadcasted_iota(jnp.int32, (GM_CHUNK, GM_CHUNK), 0)
    si = lax.broadcasted_iota(jnp.int32, (GM_CHUNK, GM_CHUNK), 1)
    w_tri = [jnp.where(si <= ti, ws_ref[g], 0.0).astype(BF16) for g in range(GM_GROUPS)]

    def gelu(t):
        return t * (0.5 * (1.0 + jnp.tanh(_SQRT_2_OVER_PI * (t + 0.044715 * (t * t * t)))))

    def chunk_body(ci, carry):
        r = pl.multiple_of(ci * GM_CHUNK, GM_CHUNK)
        u = gelu(z_ref[pl.ds(r, GM_CHUNK), 0:GM_WIDTH] + b_in[:, 0:GM_WIDTH])
        v = gelu(z_ref[pl.ds(r, GM_CHUNK), GM_WIDTH:2 * GM_WIDTH] + b_in[:, GM_WIDTH:])
        mu = jnp.mean(v, axis=-1, keepdims=True)
        vc = v - mu
        inv = lax.rsqrt(jnp.mean(vc * vc, axis=-1, keepdims=True) + EPS)
        vn = ((vc * inv) * ln_g + ln_b).astype(BF16)
        for g in range(GM_GROUPS):
            gs = slice(g * GM_GW, (g + 1) * GM_GW)
            mixed = _dot(w_tri[g], vn[:, gs]) + bst[:, g:g + 1]
            y_ref[pl.ds(r, GM_CHUNK), gs] = (u[:, gs] * mixed).astype(BF16)
        return carry

    lax.fori_loop(0, ts // GM_CHUNK, chunk_body, 0)

    o_ref[...] = x_ref[...] + g1 * _dot(y_ref[...], wout_ref[...])


def _mix1_call(x, mod, ng, win, b_in, ln_g, ln_b, ws, bst, wout, *, ts):
    bsz, seq, d = x.shape
    const2 = lambda b, s: (0, 0)
    return pl.pallas_call(
        functools.partial(_mix1_kernel, ts=ts),
        grid=(bsz, seq // ts),
        in_specs=[
            pl.BlockSpec((None, ts, d), lambda b, s: (b, s, 0)),
            pl.BlockSpec((None, 6, d), lambda b, s: (b, 0, 0)),
            pl.BlockSpec((1, d), const2),
            pl.BlockSpec((d, 2 * GM_WIDTH), const2),
            pl.BlockSpec((1, 2 * GM_WIDTH), const2),
            pl.BlockSpec((1, GM_WIDTH), const2),
            pl.BlockSpec((1, GM_WIDTH), const2),
            pl.BlockSpec((GM_GROUPS, GM_CHUNK, GM_CHUNK), lambda b, s: (0, 0, 0)),
            pl.BlockSpec((GM_CHUNK, GM_GROUPS), const2),
            pl.BlockSpec((GM_WIDTH, d), const2),
        ],
        out_specs=pl.BlockSpec((None, ts, d), lambda b, s: (b, s, 0)),
        out_shape=jax.ShapeDtypeStruct((bsz, seq, d), F32),
        scratch_shapes=[
            pltpu.VMEM((ts, d), BF16),
            pltpu.VMEM((ts, 2 * GM_WIDTH), F32),
            pltpu.VMEM((ts, GM_WIDTH), BF16),
        ],
        compiler_params=pltpu.CompilerParams(
            dimension_semantics=("parallel", "parallel"),
            vmem_limit_bytes=VMEM_LIMIT_BYTES),
        name="mixer_spatial_gating",
    )(x, mod, ng, win, b_in, ln_g, ln_b, ws, bst, wout)


def _ffn_kernel(x_ref, mod_ref, ng_ref, w1_ref, w2_ref, fg_ref, o_ref, hb_ref, *, ts, final):
    sh2 = mod_ref[3:4, :]
    sc2 = mod_ref[4:5, :]
    g2 = mod_ref[5:6, :]
    _norm_mod_to(x_ref, hb_ref, ng_ref[...] * (1.0 + sc2), sh2, ts)

    acc = None
    fcols = D_MODEL
    for j in range(D_FF // fcols):
        hid = jnp.maximum(_dot(hb_ref[...], w1_ref[:, j * fcols:(j + 1) * fcols]), 0.0)
        part = _dot((hid * hid).astype(BF16), w2_ref[j * fcols:(j + 1) * fcols, :])
        acc = part if acc is None else acc + part

    out = x_ref[...] + g2 * acc
    if final:
        inv = lax.rsqrt(jnp.mean(out * out, axis=-1, keepdims=True) + EPS)
        out = (out * inv) * fg_ref[...]
    o_ref[...] = out


def _ffn_call(x, mod, ng, w1, w2, fg, *, ts, final):
    bsz, seq, d = x.shape
    const2 = lambda b, s: (0, 0)
    return pl.pallas_call(
        functools.partial(_ffn_kernel, ts=ts, final=final),
        grid=(bsz, seq // ts),
        in_specs=[
            pl.BlockSpec((None, ts, d), lambda b, s: (b, s, 0)),
            pl.BlockSpec((None, 6, d), lambda b, s: (b, 0, 0)),
            pl.BlockSpec((1, d), const2),
            pl.BlockSpec((d, D_FF), const2, pipeline_mode=pl.Buffered(1)),
            pl.BlockSpec((D_FF, d), const2, pipeline_mode=pl.Buffered(1)),
            pl.BlockSpec((1, d), const2),
        ],
        out_specs=pl.BlockSpec((None, ts, d), lambda b, s: (b, s, 0)),
        out_shape=jax.ShapeDtypeStruct((bsz, seq, d), F32),
        scratch_shapes=[pltpu.VMEM((ts, d), BF16)],
        compiler_params=pltpu.CompilerParams(
            dimension_semantics=("parallel", "parallel"),
            vmem_limit_bytes=VMEM_LIMIT_BYTES),
        name="ffn_final" if final else "ffn",
    )(x, mod, ng, w1, w2, fg)


def kernel(x, c, ada_w, ada_b, norm_mix_g, norm_ffn_g, w_in0, conv_w, conv_b, hg_lb, hg_gain,
           w_out0, w_in1, b_in1, gm_ln_g, gm_ln_b, gm_ws, gm_bs, w_out1, w_ff1, w_ff2, final_g):
    bsz, seq, d = x.shape
    ts = min(SEQ_TILE, seq)
    mod = _ada_call(c, ada_w, ada_b).reshape(ada_w.shape[0], bsz, 6, d)
    fg = final_g.reshape(1, d)

    x = _mix0_call(x, mod[0], norm_mix_g[0:1], w_in0[0].astype(BF16), conv_w[0], conv_b[0:1],
                   hg_lb, hg_gain[0:1], w_out0[0].astype(BF16), ts=ts)
    x = _ffn_call(x, mod[0], norm_ffn_g[0:1], w_ff1[0].astype(BF16), w_ff2[0].astype(BF16), fg,
                  ts=ts, final=False)
    x = _mix1_call(x, mod[1], norm_mix_g[1:2], w_in1[0].astype(BF16), b_in1[0:1], gm_ln_g[0:1],
                   gm_ln_b[0:1], gm_ws[0], gm_bs[0].T, w_out1[0].astype(BF16), ts=ts)
    x = _ffn_call(x, mod[1], norm_ffn_g[1:2], w_ff1[1].astype(BF16), w_ff2[1].astype(BF16), fg,
                  ts=ts, final=True)
    return x
```

```python
import functools

import numpy as np
import jax
import jax.numpy as jnp
from jax import lax
from jax.experimental import pallas as pl
from jax.experimental.pallas import tpu as pltpu

F32 = jnp.float32
BF16 = jnp.bfloat16

EPS = 1e-6
D_MODEL = 1024
D_FF = 4 * D_MODEL
A_WIDTH = 512
B_WIDTH = 512
HG_HEADS = 4
HG_DK = 128
HG_DV = 128
IN0_COLS = 3 * A_WIDTH + 4 * B_WIDTH
GM_WIDTH = D_MODEL
GM_GROUPS = 4
GM_CHUNK = 128
GM_GW = GM_WIDTH // GM_GROUPS

SUBLANES = 8
LANES = 128
VMEM_LIMIT_BYTES = 56 * 1024 * 1024

SEQ_TILE = 512
ROW_BLOCK = 64
HG_CHUNK = 64
HG_LEVELS = (32, 16, 8, 4, 2, 1)
PROJ_PIECE = 256
_SQRT_2_OVER_PI = float(np.sqrt(2.0 / np.pi))


def _sigmoid(v):
    return 1.0 / (1.0 + jnp.exp(-v))


def _dot(a, b):
    return jnp.dot(a, b, preferred_element_type=F32)


def _dot_nt(a, b):
    return lax.dot_general(a, b, (((1,), (1,)), ((), ())), preferred_element_type=F32)


def _dot_tn(a, b):
    return lax.dot_general(a, b, (((0,), (0,)), ((), ())), preferred_element_type=F32)


def _norm_mod_to(x_ref, hb_ref, gs, sh, rows):
    def body(i, carry):
        r = pl.multiple_of(i * ROW_BLOCK, ROW_BLOCK)
        xt = x_ref[pl.ds(r, ROW_BLOCK), :]
        inv = lax.rsqrt(jnp.mean(xt * xt, axis=-1, keepdims=True) + EPS)
        hb_ref[pl.ds(r, ROW_BLOCK), :] = ((xt * inv) * gs + sh).astype(BF16)
        return carry

    lax.fori_loop(0, rows // ROW_BLOCK, body, 0)


def _ada_kernel(c_ref, w_ref, b_ref, o_ref):
    c = c_ref[...]
    s = c * _sigmoid(c)
    o_ref[...] = jnp.dot(s, w_ref[...], preferred_element_type=F32,
                         precision=lax.Precision.HIGHEST) + b_ref[...]


def _ada_call(c, ada_w, ada_b):
    depth, d, e = ada_w.shape
    bsz = c.shape[0]
    tn = 1536
    return pl.pallas_call(
        _ada_kernel,
        grid=(depth, e // tn),
        in_specs=[
            pl.BlockSpec((bsz, d), lambda l, j: (0, 0)),
            pl.BlockSpec((None, d, tn), lambda l, j: (l, 0, j)),
            pl.BlockSpec((None, 1, tn), lambda l, j: (l, 0, j)),
        ],
        out_specs=pl.BlockSpec((None, bsz, tn), lambda l, j: (l, 0, j)),
        out_shape=jax.ShapeDtypeStruct((depth, bsz, e), F32),
        compiler_params=pltpu.CompilerParams(
            dimension_semantics=("arbitrary", "arbitrary"),
            vmem_limit_bytes=VMEM_LIMIT_BYTES),
        name="ada_modulation",
    )(c, ada_w, ada_b.reshape(depth, 1, e))


def _level_masks():
    ti = lax.broadcasted_iota(jnp.int32, (HG_CHUNK, HG_CHUNK), 0)
    si = lax.broadcasted_iota(jnp.int32, (HG_CHUNK, HG_CHUNK), 1)
    masks = []
    for m in HG_LEVELS:
        other_blk = (ti ^ si) & ~(2 * m - 1)
        masks.append((other_blk | ((ti & m) ^ m) | (si & m)) == 0)
    return masks, ti == si


def _block_boundary(c, m):
    rows, w = c.shape
    if m >= SUBLANES:
        blk = 2 * m
        return jnp.concatenate(
            [jnp.broadcast_to(c[b * blk + m - 1:b * blk + m, :], (blk, w))
             for b in range(rows // blk)], axis=0)
    c3 = c.reshape(rows // SUBLANES, SUBLANES, w)
    if m == 4:
        bnd = jnp.broadcast_to(c3[:, 3:4, :], c3.shape)
    else:
        assert m == 2
        sub = lax.broadcasted_iota(jnp.int32, c3.shape, 1)
        bnd = jnp.where(sub < 4, jnp.broadcast_to(c3[:, 1:2, :], c3.shape),
                        jnp.broadcast_to(c3[:, 5:6, :], c3.shape))
    return bnd.reshape(rows, w)


def _chunk_cumsum(x):
    rows, w = x.shape
    nv = rows // SUBLANES
    x3 = x.reshape(nv, SUBLANES, w)
    sub = lax.broadcasted_iota(jnp.int32, x3.shape, 1)
    sh = 1
    while sh < SUBLANES:
        x3 = x3 + jnp.where(sub >= sh, pltpu.roll(x3, sh, axis=1), 0.0)
        sh *= 2
    tot = jnp.broadcast_to(x3[:, SUBLANES - 1:SUBLANES, :], x3.shape)
    outs = [x3[0]]
    acc = tot[0]
    for j in range(1, nv):
        outs.append(x3[j] + acc)
        if j + 1 < nv:
            acc = acc + tot[j]
    return jnp.concatenate(outs, axis=0)


def _mix0_kernel(xc_ref, xp_ref, modc_ref, modp_ref, ng_ref, win_ref, cw_ref, cb_ref, hglb_ref,
                 gain_ref, wout_ref, o_ref, hb_ref, pa_ref, pb_ref, y_ref, st_ref, zc_ref, *, ts,
                 tiles_per_row):
    k = pl.program_id(0)

    @pl.when(jnp.maximum(k - 1, 0) % tiles_per_row == 0)
    def _():
        st_ref[...] = jnp.zeros_like(st_ref)
        zc_ref[...] = jnp.zeros_like(zc_ref)

    @pl.when(k == 0)
    def _():
        pb_ref[...] = jnp.zeros_like(pb_ref)

    step = functools.partial(_mix0_step, xc_ref, xp_ref, modc_ref, modp_ref, ng_ref, win_ref,
                             cw_ref, cb_ref, hglb_ref, gain_ref, wout_ref, o_ref, hb_ref, y_ref,
                             st_ref, zc_ref, ts)
    pl.when(k % 2 == 0)(functools.partial(step, pa_ref, pb_ref))
    pl.when(k % 2 == 1)(functools.partial(step, pb_ref, pa_ref))


def _mix0_step(xc_ref, xp_ref, modc_ref, modp_ref, ng_ref, win_ref, cw_ref, cb_ref, hglb_ref,
               gain_ref, wout_ref, o_ref, hb_ref, y_ref, st_ref, zc_ref, ts, p_cur, p_prv):
    n_chunks = ts // HG_CHUNK

    _norm_mod_to(xc_ref, hb_ref, ng_ref[...] * (1.0 + modc_ref[1:2, :]), modc_ref[0:1, :], ts)

    a0, a1, a2 = hglb_ref[0:1, :], hglb_ref[1:2, :], hglb_ref[2:3, :]
    amax = jnp.maximum(jnp.maximum(a0, a1), a2)
    e0, e1, e2 = jnp.exp(a0 - amax), jnp.exp(a1 - amax), jnp.exp(a2 - amax)
    lb = e0 / (e0 + e1 + e2)

    cw0, cw1, cw2 = cw_ref[0:1, :], cw_ref[1:2, :], cw_ref[2:3, :]
    cb = cb_ref[...]
    gain = gain_ref[...]
    row8 = lax.broadcasted_iota(jnp.int32, (SUBLANES, A_WIDTH), 0)
    masks, diag = _level_masks()

    def mix_chunk(ci, between):
        r = ci * HG_CHUNK
        between(0)

        def col(j):
            return p_prv[r:r + HG_CHUNK, j * 512:(j + 1) * 512]

        z = col(1) * col(2)
        prev = zc_ref[...]
        zc_ref[...] = z[HG_CHUNK - SUBLANES:, :]

        def shifted(k):
            zr = pltpu.roll(z, k, axis=0)
            pr = pltpu.roll(prev, k, axis=0)
            head = jnp.where(row8 < k, pr, zr[:SUBLANES])
            return jnp.concatenate([head, zr[SUBLANES:]], axis=0)

        conv = cb + shifted(2) * cw0
        conv = conv + shifted(1) * cw1
        conv = conv + z * cw2
        y_ref[r:r + HG_CHUNK, 0:A_WIDTH] = (col(0) * conv).astype(BF16)

        q = col(3)
        f = lb + (1.0 - lb) * _sigmoid(col(4))
        kk = 1.0 - f
        v = col(5)
        gate = col(6)
        cum = _chunk_cumsum(jnp.log(f))
        ck = cum - jnp.log(kk)
        last = cum[HG_CHUNK - 1:HG_CHUNK, :]

        q_in = q * jnp.exp(cum)
        k_dec = jnp.exp(last - ck)
        dec = jnp.exp(last)
        qf = q * f
        gsil = gate * _sigmoid(gate)

        for h in range(HG_HEADS):
            between(h + 1)
            hs = slice(h * HG_DK, (h + 1) * HG_DK)
            q_h, cum_h, ck_h = q[:, hs], cum[:, hs], ck[:, hs]
            k_b = kk[:, hs].astype(BF16)
            v_b = v[:, hs].astype(BF16)
            st = st_ref[h]
            o = _dot_nt(q_in[:, hs].astype(BF16), st.astype(BF16))

            scores = jnp.where(diag, _dot_nt(q_h.astype(BF16), k_b), 0.0)
            for m, mask in zip(HG_LEVELS, masks):
                if m == 1:
                    qt, kt = qf[:, hs].astype(BF16), k_b
                else:
                    bnd = _block_boundary(cum_h, m)
                    qt = (q_h * jnp.exp(cum_h - bnd)).astype(BF16)
                    kt = jnp.exp(bnd - ck_h).astype(BF16)
                scores = jnp.where(mask, _dot_nt(qt, kt), scores)
            o = o + _dot(scores.astype(BF16), v_b)

            st_ref[h] = st * dec[:, hs] + _dot_tn(v_b, k_dec[:, hs].astype(BF16))

            o = o * lax.rsqrt(jnp.mean(o * o, axis=-1, keepdims=True) + EPS)
            y_ref[r:r + HG_CHUNK, A_WIDTH + h * HG_DV:A_WIDTH + (h + 1) * HG_DV] = (
                o * gain[:, hs] * gsil[:, hs]).astype(BF16)

    pieces = [slice(c, c + PROJ_PIECE) for c in range(0, IN0_COLS, PROJ_PIECE)]
    slots = [(ci, s) for ci in range(n_chunks) for s in range(HG_HEADS + 1)]
    stride = len(slots) / len(pieces)
    at_slot = {slots[int(i * stride)]: cs for i, cs in enumerate(pieces)}
    assert len(at_slot) == len(pieces)

    def project_at(ci):
        def between(s):
            cs = at_slot.get((ci, s))
            if cs is not None:
                p_cur[:, cs] = _dot(hb_ref[...], win_ref[:, cs])
        return between

    for ci in range(n_chunks):
        mix_chunk(ci, project_at(ci))

    o_ref[...] = xp_ref[...] + modp_ref[2:3, :] * _dot(y_ref[...], wout_ref[...])


def _mix0_call(x, mod, ng, win, cw, cb, hglb, gain, wout, *, ts):
    bsz, seq, d = x.shape
    tiles_per_row = seq // ts
    n_tiles = bsz * tiles_per_row
    assert ts % HG_CHUNK == 0 and (ts // HG_CHUNK) * (HG_HEADS + 1) >= IN0_COLS // PROJ_PIECE
    cur = lambda k: jnp.minimum(k, n_tiles - 1)
    prv = lambda k: jnp.maximum(k - 1, 0)
    const2 = lambda k: (0, 0)
    out = pl.pallas_call(
        functools.partial(_mix0_kernel, ts=ts, tiles_per_row=tiles_per_row),
        grid=(n_tiles + 1,),
        in_specs=[
            pl.BlockSpec((None, ts, d), lambda k: (cur(k), 0, 0)),
            pl.BlockSpec((None, ts, d), lambda k: (prv(k), 0, 0)),
            pl.BlockSpec((None, 6, d), lambda k: (cur(k) // tiles_per_row, 0, 0)),
            pl.BlockSpec((None, 6, d), lambda k: (prv(k) // tiles_per_row, 0, 0)),
            pl.BlockSpec((1, d), const2),
            pl.BlockSpec((d, IN0_COLS), const2, pipeline_mode=pl.Buffered(1)),
            pl.BlockSpec((3, A_WIDTH), const2),
            pl.BlockSpec((1, A_WIDTH), const2),
            pl.BlockSpec((3, B_WIDTH), const2),
            pl.BlockSpec((1, B_WIDTH), const2),
            pl.BlockSpec((d, d), const2, pipeline_mode=pl.Buffered(1)),
        ],
        out_specs=pl.BlockSpec((None, ts, d), lambda k: (prv(k), 0, 0)),
        out_shape=jax.ShapeDtypeStruct((n_tiles, ts, d), F32),
        scratch_shapes=[
            pltpu.VMEM((ts, d), BF16),
            pltpu.VMEM((ts, IN0_COLS), F32),
            pltpu.VMEM((ts, IN0_COLS), F32),
            pltpu.VMEM((ts, d), BF16),
            pltpu.VMEM((HG_HEADS, HG_DV, HG_DK), F32),
            pltpu.VMEM((SUBLANES, A_WIDTH), F32),
        ],
        compiler_params=pltpu.CompilerParams(
            dimension_semantics=("arbitrary",),
            vmem_limit_bytes=VMEM_LIMIT_BYTES),
        name="mixer_conv_hgrn2",
    )(x.reshape(n_tiles, ts, d), x.reshape(n_tiles, ts, d), mod, mod, ng, win, cw, cb, hglb,
      gain, wout)
    return out.reshape(bsz, seq, d)


def _mix1_kernel(x_ref, mod_ref, ng_ref, win_ref, bin_ref, lng_ref, lnb_ref, ws_ref,
                 bst_ref, wout_ref, o_ref, hb_ref, z_ref, y_ref, *, ts):
    sh1 = mod_ref[0:1, :]
    sc1 = mod_ref[1:2, :]
    g1 = mod_ref[2:3, :]
    _norm_mod_to(x_ref, hb_ref, ng_ref[...] * (1.0 + sc1), sh1, ts)

    z_ref[...] = _dot(hb_ref[...], win_ref[...])

    b_in = bin_ref[...]
    ln_g = lng_ref[...]
    ln_b = lnb_ref[...]
    bst = bst_ref[...]
    ti = lax.broadcasted_iota(jnp.int32, (GM_CHUNK, GM_CHUNK), 0)
    si = lax.broadcasted_iota(jnp.int32, (GM_CHUNK, GM_CHUNK), 1)
    w_tri = [jnp.where(si <= ti, ws_ref[g], 0.0).astype(BF16) for g in range(GM_GROUPS)]

    def gelu(t):
        return t * (0.5 * (1.0 + jnp.tanh(_SQRT_2_OVER_PI * (t + 0.044715 * (t * t * t)))))

    def chunk_body(ci, carry):
        r = pl.multiple_of(ci * GM_CHUNK, GM_CHUNK)
        u = gelu(z_ref[pl.ds(r, GM_CHUNK), 0:GM_WIDTH] + b_in[:, 0:GM_WIDTH])
        v = gelu(z_ref[pl.ds(r, GM_CHUNK), GM_WIDTH:2 * GM_WIDTH] + b_in[:, GM_WIDTH:])
        mu = jnp.mean(v, axis=-1, keepdims=True)
        vc = v - mu
        inv = lax.rsqrt(jnp.mean(vc * vc, axis=-1, keepdims=True) + EPS)
        vn = ((vc * inv) * ln_g + ln_b).astype(BF16)
        for g in range(GM_GROUPS):
            gs = slice(g * GM_GW, (g + 1) * GM_GW)
            mixed = _dot(w_tri[g], vn[:, gs]) + bst[:, g:g + 1]
            y_ref[pl.ds(r, GM_CHUNK), gs] = (u[:, gs] * mixed).astype(BF16)
        return carry

    lax.fori_loop(0, ts // GM_CHUNK, chunk_body, 0)

    o_ref[...] = x_ref[...] + g1 * _dot(y_ref[...], wout_ref[...])


def _mix1_call(x, mod, ng, win, b_in, ln_g, ln_b, ws, bst, wout, *, ts):
    bsz, seq, d = x.shape
    const2 = lambda b, s: (0, 0)
    return pl.pallas_call(
        functools.partial(_mix1_kernel, ts=ts),
        grid=(bsz, seq // ts),
        in_specs=[
            pl.BlockSpec((None, ts, d), lambda b, s: (b, s, 0)),
            pl.BlockSpec((None, 6, d), lambda b, s: (b, 0, 0)),
            pl.BlockSpec((1, d), const2),
            pl.BlockSpec((d, 2 * GM_WIDTH), const2),
            pl.BlockSpec((1, 2 * GM_WIDTH), const2),
            pl.BlockSpec((1, GM_WIDTH), const2),
            pl.BlockSpec((1, GM_WIDTH), const2),
            pl.BlockSpec((GM_GROUPS, GM_CHUNK, GM_CHUNK), lambda b, s: (0, 0, 0)),
            pl.BlockSpec((GM_CHUNK, GM_GROUPS), const2),
            pl.BlockSpec((GM_WIDTH, d), const2),
        ],
        out_specs=pl.BlockSpec((None, ts, d), lambda b, s: (b, s, 0)),
        out_shape=jax.ShapeDtypeStruct((bsz, seq, d), F32),
        scratch_shapes=[
            pltpu.VMEM((ts, d), BF16),
            pltpu.VMEM((ts, 2 * GM_WIDTH), F32),
            pltpu.VMEM((ts, GM_WIDTH), BF16),
        ],
        compiler_params=pltpu.CompilerParams(
            dimension_semantics=("parallel", "parallel"),
            vmem_limit_bytes=VMEM_LIMIT_BYTES),
        name="mixer_spatial_gating",
    )(x, mod, ng, win, b_in, ln_g, ln_b, ws, bst, wout)


def _ffn_kernel(x_ref, mod_ref, ng_ref, w1_ref, w2_ref, fg_ref, o_ref, hb_ref, *, ts, final):
    sh2 = mod_ref[3:4, :]
    sc2 = mod_ref[4:5, :]
    g2 = mod_ref[5:6, :]
    _norm_mod_to(x_ref, hb_ref, ng_ref[...] * (1.0 + sc2), sh2, ts)

    acc = None
    fcols = D_MODEL
    for j in range(D_FF // fcols):
        hid = jnp.maximum(_dot(hb_ref[...], w1_ref[:, j * fcols:(j + 1) * fcols]), 0.0)
        part = _dot((hid * hid).astype(BF16), w2_ref[j * fcols:(j + 1) * fcols, :])
        acc = part if acc is None else acc + part

    out = x_ref[...] + g2 * acc
    if final:
        inv = lax.rsqrt(jnp.mean(out * out, axis=-1, keepdims=True) + EPS)
        out = (out * inv) * fg_ref[...]
    o_ref[...] = out


def _ffn_call(x, mod, ng, w1, w2, fg, *, ts, final):
    bsz, seq, d = x.shape
    const2 = lambda b, s: (0, 0)
    return pl.pallas_call(
        functools.partial(_ffn_kernel, ts=ts, final=final),
        grid=(bsz, seq // ts),
        in_specs=[
            pl.BlockSpec((None, ts, d), lambda b, s: (b, s, 0)),
            pl.BlockSpec((None, 6, d), lambda b, s: (b, 0, 0)),
            pl.BlockSpec((1, d), const2),
            pl.BlockSpec((d, D_FF), const2, pipeline_mode=pl.Buffered(1)),
            pl.BlockSpec((D_FF, d), const2, pipeline_mode=pl.Buffered(1)),
            pl.BlockSpec((1, d), const2),
        ],
        out_specs=pl.BlockSpec((None, ts, d), lambda b, s: (b, s, 0)),
        out_shape=jax.ShapeDtypeStruct((bsz, seq, d), F32),
        scratch_shapes=[pltpu.VMEM((ts, d), BF16)],
        compiler_params=pltpu.CompilerParams(
            dimension_semantics=("parallel", "parallel"),
            vmem_limit_bytes=VMEM_LIMIT_BYTES),
        name="ffn_final" if final else "ffn",
    )(x, mod, ng, w1, w2, fg)


def kernel(x, c, ada_w, ada_b, norm_mix_g, norm_ffn_g, w_in0, conv_w, conv_b, hg_lb, hg_gain,
           w_out0, w_in1, b_in1, gm_ln_g, gm_ln_b, gm_ws, gm_bs, w_out1, w_ff1, w_ff2, final_g):
    bsz, seq, d = x.shape
    ts = min(SEQ_TILE, seq)
    mod = _ada_call(c, ada_w, ada_b).reshape(ada_w.shape[0], bsz, 6, d)
    fg = final_g.reshape(1, d)

    x = _mix0_call(x, mod[0], norm_mix_g[0:1], w_in0[0].astype(BF16), conv_w[0], conv_b[0:1],
                   hg_lb, hg_gain[0:1], w_out0[0].astype(BF16), ts=ts)
    x = _ffn_call(x, mod[0], norm_ffn_g[0:1], w_ff1[0].astype(BF16), w_ff2[0].astype(BF16), fg,
                  ts=ts, final=False)
    x = _mix1_call(x, mod[1], norm_mix_g[1:2], w_in1[0].astype(BF16), b_in1[0:1], gm_ln_g[0:1],
                   gm_ln_b[0:1], gm_ws[0], gm_bs[0].T, w_out1[0].astype(BF16), ts=ts)
    x = _ffn_call(x, mod[1], norm_ffn_g[1:2], w_ff1[1].astype(BF16), w_ff2[1].astype(BF16), fg,
                  ts=ts, final=True)
    return x
```

```python
import functools

import numpy as np
import jax
import jax.numpy as jnp
from jax import lax
from jax.experimental import pallas as pl
from jax.experimental.pallas import tpu as pltpu

F32 = jnp.float32
BF16 = jnp.bfloat16

EPS = 1e-6
D_MODEL = 1024
D_FF = 4 * D_MODEL
A_WIDTH = 512
B_WIDTH = 512
HG_HEADS = 4
HG_DK = 128
HG_DV = 128
IN0_COLS = 3 * A_WIDTH + 4 * B_WIDTH
GM_WIDTH = D_MODEL
GM_GROUPS = 4
GM_CHUNK = 128
GM_GW = GM_WIDTH // GM_GROUPS

SUBLANES = 8
LANES = 128
VMEM_LIMIT_BYTES = 56 * 1024 * 1024

SEQ_TILE = 512
ROW_BLOCK = 64
HG_CHUNK = 64
HG_LEVELS = (32, 16, 8, 4, 2, 1)
PROJ_PIECE = 256
_SQRT_2_OVER_PI = float(np.sqrt(2.0 / np.pi))


def _sigmoid(v):
    return 1.0 / (1.0 + jnp.exp(-v))


def _dot(a, b):
    return jnp.dot(a, b, preferred_element_type=F32)


def _dot_nt(a, b):
    return lax.dot_general(a, b, (((1,), (1,)), ((), ())), preferred_element_type=F32)


def _dot_tn(a, b):
    return lax.dot_general(a, b, (((0,), (0,)), ((), ())), preferred_element_type=F32)


def _norm_mod_block(x_ref, hb_ref, gs, sh, r):
    xt = x_ref[pl.ds(r, ROW_BLOCK), :]
    inv = lax.rsqrt(jnp.mean(xt * xt, axis=-1, keepdims=True) + EPS)
    hb_ref[pl.ds(r, ROW_BLOCK), :] = ((xt * inv) * gs + sh).astype(BF16)


def _norm_mod_to(x_ref, hb_ref, gs, sh, rows):
    def body(i, carry):
        _norm_mod_block(x_ref, hb_ref, gs, sh, pl.multiple_of(i * ROW_BLOCK, ROW_BLOCK))
        return carry

    lax.fori_loop(0, rows // ROW_BLOCK, body, 0)


def _ada_kernel(c_ref, w_ref, b_ref, o_ref):
    c = c_ref[...]
    s = c * _sigmoid(c)
    o_ref[...] = jnp.dot(s, w_ref[...], preferred_element_type=F32,
                         precision=lax.Precision.HIGHEST) + b_ref[...]


def _ada_call(c, ada_w, ada_b):
    depth, d, e = ada_w.shape
    bsz = c.shape[0]
    tn = 1536
    return pl.pallas_call(
        _ada_kernel,
        grid=(depth, e // tn),
        in_specs=[
            pl.BlockSpec((bsz, d), lambda l, j: (0, 0)),
            pl.BlockSpec((None, d, tn), lambda l, j: (l, 0, j)),
            pl.BlockSpec((None, 1, tn), lambda l, j: (l, 0, j)),
        ],
        out_specs=pl.BlockSpec((None, bsz, tn), lambda l, j: (l, 0, j)),
        out_shape=jax.ShapeDtypeStruct((depth, bsz, e), F32),
        compiler_params=pltpu.CompilerParams(
            dimension_semantics=("arbitrary", "arbitrary"),
            vmem_limit_bytes=VMEM_LIMIT_BYTES),
        name="ada_modulation",
    )(c, ada_w, ada_b.reshape(depth, 1, e))


def _level_masks():
    ti = lax.broadcasted_iota(jnp.int32, (HG_CHUNK, HG_CHUNK), 0)
    si = lax.broadcasted_iota(jnp.int32, (HG_CHUNK, HG_CHUNK), 1)
    masks = []
    for m in HG_LEVELS:
        other_blk = (ti ^ si) & ~(2 * m - 1)
        masks.append((other_blk | ((ti & m) ^ m) | (si & m)) == 0)
    return masks, ti == si


def _block_boundary(c, m):
    rows, w = c.shape
    if m >= SUBLANES:
        blk = 2 * m
        return jnp.concatenate(
            [jnp.broadcast_to(c[b * blk + m - 1:b * blk + m, :], (blk, w))
             for b in range(rows // blk)], axis=0)
    c3 = c.reshape(rows // SUBLANES, SUBLANES, w)
    if m == 4:
        bnd = jnp.broadcast_to(c3[:, 3:4, :], c3.shape)
    else:
        assert m == 2
        sub = lax.broadcasted_iota(jnp.int32, c3.shape, 1)
        bnd = jnp.where(sub < 4, jnp.broadcast_to(c3[:, 1:2, :], c3.shape),
                        jnp.broadcast_to(c3[:, 5:6, :], c3.shape))
    return bnd.reshape(rows, w)


def _chunk_cumsum(x):
    rows, w = x.shape
    nv = rows // SUBLANES
    x3 = x.reshape(nv, SUBLANES, w)
    sub = lax.broadcasted_iota(jnp.int32, x3.shape, 1)
    sh = 1
    while sh < SUBLANES:
        x3 = x3 + jnp.where(sub >= sh, pltpu.roll(x3, sh, axis=1), 0.0)
        sh *= 2
    tot = jnp.broadcast_to(x3[:, SUBLANES - 1:SUBLANES, :], x3.shape)
    outs = [x3[0]]
    acc = tot[0]
    for j in range(1, nv):
        outs.append(x3[j] + acc)
        if j + 1 < nv:
            acc = acc + tot[j]
    return jnp.concatenate(outs, axis=0)


def _mix0_kernel(xc_ref, xp_ref, modc_ref, modp_ref, ng_ref, win_ref, cw_ref, cb_ref, hglb_ref,
                 gain_ref, wout_ref, o_ref, hb_ref, pa_ref, pb_ref, y_ref, st_ref, zc_ref, *, ts,
                 tiles_per_row):
    k = pl.program_id(0)

    @pl.when(jnp.maximum(k - 1, 0) % tiles_per_row == 0)
    def _():
        st_ref[...] = jnp.zeros_like(st_ref)
        zc_ref[...] = jnp.zeros_like(zc_ref)

    @pl.when(k == 0)
    def _():
        pb_ref[...] = jnp.zeros_like(pb_ref)

    step = functools.partial(_mix0_step, xc_ref, xp_ref, modc_ref, modp_ref, ng_ref, win_ref,
                             cw_ref, cb_ref, hglb_ref, gain_ref, wout_ref, o_ref, hb_ref, y_ref,
                             st_ref, zc_ref, ts)
    pl.when(k % 2 == 0)(functools.partial(step, pa_ref, pb_ref))
    pl.when(k % 2 == 1)(functools.partial(step, pb_ref, pa_ref))


def _mix0_step(xc_ref, xp_ref, modc_ref, modp_ref, ng_ref, win_ref, cw_ref, cb_ref, hglb_ref,
               gain_ref, wout_ref, o_ref, hb_ref, y_ref, st_ref, zc_ref, ts, p_cur, p_prv):
    n_chunks = ts // HG_CHUNK

    _norm_mod_to(xc_ref, hb_ref, ng_ref[...] * (1.0 + modc_ref[1:2, :]), modc_ref[0:1, :], ts)

    a0, a1, a2 = hglb_ref[0:1, :], hglb_ref[1:2, :], hglb_ref[2:3, :]
    amax = jnp.maximum(jnp.maximum(a0, a1), a2)
    e0, e1, e2 = jnp.exp(a0 - amax), jnp.exp(a1 - amax), jnp.exp(a2 - amax)
    lb = e0 / (e0 + e1 + e2)

    cw0, cw1, cw2 = cw_ref[0:1, :], cw_ref[1:2, :], cw_ref[2:3, :]
    cb = cb_ref[...]
    gain = gain_ref[...]
    row8 = lax.broadcasted_iota(jnp.int32, (SUBLANES, A_WIDTH), 0)
    masks, diag = _level_masks()

    def mix_chunk(ci, between):
        r = ci * HG_CHUNK
        between(0)

        def col(j):
            return p_prv[r:r + HG_CHUNK, j * 512:(j + 1) * 512]

        z = col(1) * col(2)
        prev = zc_ref[...]
        zc_ref[...] = z[HG_CHUNK - SUBLANES:, :]

        def shifted(k):
            zr = pltpu.roll(z, k, axis=0)
            pr = pltpu.roll(prev, k, axis=0)
            head = jnp.where(row8 < k, pr, zr[:SUBLANES])
            return jnp.concatenate([head, zr[SUBLANES:]], axis=0)

        conv = cb + shifted(2) * cw0
        conv = conv + shifted(1) * cw1
        conv = conv + z * cw2
        y_ref[r:r + HG_CHUNK, 0:A_WIDTH] = (col(0) * conv).astype(BF16)

        q = col(3)
        f = lb + (1.0 - lb) * _sigmoid(col(4))
        kk = 1.0 - f
        v = col(5)
        gate = col(6)
        cum = _chunk_cumsum(jnp.log(f))
        ck = cum - jnp.log(kk)
        last = cum[HG_CHUNK - 1:HG_CHUNK, :]

        q_in = q * jnp.exp(cum)
        k_dec = jnp.exp(last - ck)
        dec = jnp.exp(last)
        qf = q * f
        gsil = gate * _sigmoid(gate)

        for h in range(HG_HEADS):
            between(h + 1)
            hs = slice(h * HG_DK, (h + 1) * HG_DK)
            q_h, cum_h, ck_h = q[:, hs], cum[:, hs], ck[:, hs]
            k_b = kk[:, hs].astype(BF16)
            v_b = v[:, hs].astype(BF16)
            st = st_ref[h]
            o = _dot_nt(q_in[:, hs].astype(BF16), st.astype(BF16))

            scores = jnp.where(diag, _dot_nt(q_h.astype(BF16), k_b), 0.0)
            for m, mask in zip(HG_LEVELS, masks):
                if m == 1:
                    qt, kt = qf[:, hs].astype(BF16), k_b
                else:
                    bnd = _block_boundary(cum_h, m)
                    qt = (q_h * jnp.exp(cum_h - bnd)).astype(BF16)
                    kt = jnp.exp(bnd - ck_h).astype(BF16)
                scores = jnp.where(mask, _dot_nt(qt, kt), scores)
            o = o + _dot(scores.astype(BF16), v_b)

            st_ref[h] = st * dec[:, hs] + _dot_tn(v_b, k_dec[:, hs].astype(BF16))

            o = o * lax.rsqrt(jnp.mean(o * o, axis=-1, keepdims=True) + EPS)
            y_ref[r:r + HG_CHUNK, A_WIDTH + h * HG_DV:A_WIDTH + (h + 1) * HG_DV] = (
                o * gain[:, hs] * gsil[:, hs]).astype(BF16)

    pieces = [slice(c, c + PROJ_PIECE) for c in range(0, IN0_COLS, PROJ_PIECE)]
    slots = [(ci, s) for ci in range(n_chunks) for s in range(HG_HEADS + 1)]
    stride = len(slots) / len(pieces)
    at_slot = {slots[int(i * stride)]: cs for i, cs in enumerate(pieces)}
    assert len(at_slot) == len(pieces)

    def project_at(ci):
        def between(s):
            cs = at_slot.get((ci, s))
            if cs is not None:
                p_cur[:, cs] = _dot(hb_ref[...], win_ref[:, cs])
        return between

    for ci in range(n_chunks):
        mix_chunk(ci, project_at(ci))

    o_ref[...] = xp_ref[...] + modp_ref[2:3, :] * _dot(y_ref[...], wout_ref[...])


def _mix0_call(x, mod, ng, win, cw, cb, hglb, gain, wout, *, ts):
    bsz, seq, d = x.shape
    tiles_per_row = seq // ts
    n_tiles = bsz * tiles_per_row
    assert ts % HG_CHUNK == 0 and (ts // HG_CHUNK) * (HG_HEADS + 1) >= IN0_COLS // PROJ_PIECE
    cur = lambda k: jnp.minimum(k, n_tiles - 1)
    prv = lambda k: jnp.maximum(k - 1, 0)
    const2 = lambda k: (0, 0)
    out = pl.pallas_call(
        functools.partial(_mix0_kernel, ts=ts, tiles_per_row=tiles_per_row),
        grid=(n_tiles + 1,),
        in_specs=[
            pl.BlockSpec((None, ts, d), lambda k: (cur(k), 0, 0)),
            pl.BlockSpec((None, ts, d), lambda k: (prv(k), 0, 0)),
            pl.BlockSpec((None, 6, d), lambda k: (cur(k) // tiles_per_row, 0, 0)),
            pl.BlockSpec((None, 6, d), lambda k: (prv(k) // tiles_per_row, 0, 0)),
            pl.BlockSpec((1, d), const2),
            pl.BlockSpec((d, IN0_COLS), const2, pipeline_mode=pl.Buffered(1)),
            pl.BlockSpec((3, A_WIDTH), const2),
            pl.BlockSpec((1, A_WIDTH), const2),
            pl.BlockSpec((3, B_WIDTH), const2),
            pl.BlockSpec((1, B_WIDTH), const2),
            pl.BlockSpec((d, d), const2, pipeline_mode=pl.Buffered(1)),
        ],
        out_specs=pl.BlockSpec((None, ts, d), lambda k: (prv(k), 0, 0)),
        out_shape=jax.ShapeDtypeStruct((n_tiles, ts, d), F32),
        scratch_shapes=[
            pltpu.VMEM((ts, d), BF16),
            pltpu.VMEM((ts, IN0_COLS), F32),
            pltpu.VMEM((ts, IN0_COLS), F32),
            pltpu.VMEM((ts, d), BF16),
            pltpu.VMEM((HG_HEADS, HG_DV, HG_DK), F32),
            pltpu.VMEM((SUBLANES, A_WIDTH), F32),
        ],
        compiler_params=pltpu.CompilerParams(
            dimension_semantics=("arbitrary",),
            vmem_limit_bytes=VMEM_LIMIT_BYTES),
        name="mixer_conv_hgrn2",
    )(x.reshape(n_tiles, ts, d), x.reshape(n_tiles, ts, d), mod, mod, ng, win, cw, cb, hglb,
      gain, wout)
    return out.reshape(bsz, seq, d)


def _mix1_kernel(x_ref, mod_ref, ng_ref, win_ref, bin_ref, lng_ref, lnb_ref, ws_ref,
                 bst_ref, wout_ref, o_ref, hb_ref, z_ref, y_ref, *, ts):
    sh1 = mod_ref[0:1, :]
    sc1 = mod_ref[1:2, :]
    g1 = mod_ref[2:3, :]
    gs1 = ng_ref[...] * (1.0 + sc1)
    _norm_mod_to(x_ref, hb_ref, gs1, sh1, ts // 2)

    b_in = bin_ref[...]
    ln_g = lng_ref[...]
    ln_b = lnb_ref[...]
    bst = bst_ref[...]
    ti = lax.broadcasted_iota(jnp.int32, (GM_CHUNK, GM_CHUNK), 0)
    si = lax.broadcasted_iota(jnp.int32, (GM_CHUNK, GM_CHUNK), 1)
    w_tri = [jnp.where(si <= ti, ws_ref[g], 0.0).astype(BF16) for g in range(GM_GROUPS)]

    def gelu(t):
        return t * (0.5 * (1.0 + jnp.tanh(_SQRT_2_OVER_PI * (t + 0.044715 * (t * t * t)))))

    def project(rows, cs):
        z_ref[rows, cs] = _dot(hb_ref[rows, :], win_ref[:, cs])

    def emit_out(rows, cs):
        o_ref[rows, cs] = x_ref[rows, cs] + g1[:, cs] * _dot(y_ref[rows, :], wout_ref[:, cs])

    def gate_chunk(ci, mxu_work):
        rows = slice(ci * GM_CHUNK, (ci + 1) * GM_CHUNK)
        u = gelu(z_ref[rows, 0:GM_WIDTH] + b_in[:, 0:GM_WIDTH])
        mxu_work()
        v = gelu(z_ref[rows, GM_WIDTH:2 * GM_WIDTH] + b_in[:, GM_WIDTH:])
        mxu_work()
        mu = jnp.mean(v, axis=-1, keepdims=True)
        vc = v - mu
        inv = lax.rsqrt(jnp.mean(vc * vc, axis=-1, keepdims=True) + EPS)
        vn = ((vc * inv) * ln_g + ln_b).astype(BF16)
        mxu_work()
        for g in range(GM_GROUPS):
            gs = slice(g * GM_GW, (g + 1) * GM_GW)
            mixed = _dot(w_tri[g], vn[:, gs]) + bst[:, g:g + 1]
            y_ref[rows, gs] = (u[:, gs] * mixed).astype(BF16)
        mxu_work()

    half = ts // 2
    rows_a, rows_b = slice(0, half), slice(half, ts)
    chunks_per_half = half // GM_CHUNK
    in_pieces = [slice(c, c + PROJ_PIECE) for c in range(0, 2 * GM_WIDTH, PROJ_PIECE)]
    out_pieces = [slice(c, c + PROJ_PIECE) for c in range(0, x_ref.shape[1], PROJ_PIECE)]

    def drain(queue, per_call):
        def mxu_work():
            for _ in range(per_call):
                if queue:
                    queue.pop(0)()
        return mxu_work

    norm_b = [functools.partial(_norm_mod_block, x_ref, hb_ref, gs1, sh1, r)
              for r in range(half, ts, ROW_BLOCK)]
    for i, cs in enumerate(in_pieces):
        project(rows_a, cs)
        for blk in norm_b[i * len(norm_b) // len(in_pieces):
                          (i + 1) * len(norm_b) // len(in_pieces)]:
            blk()
    queue = [functools.partial(project, rows_b, cs) for cs in in_pieces]
    per_call = -(-len(queue) // (4 * chunks_per_half))
    for ci in range(chunks_per_half):
        gate_chunk(ci, drain(queue, per_call))
    assert not queue
    queue = [functools.partial(emit_out, rows_a, cs) for cs in out_pieces]
    per_call = -(-len(queue) // (4 * chunks_per_half))
    for ci in range(chunks_per_half, 2 * chunks_per_half):
        gate_chunk(ci, drain(queue, per_call))
    assert not queue
    for cs in out_pieces:
        emit_out(rows_b, cs)


def _mix1_call(x, mod, ng, win, b_in, ln_g, ln_b, ws, bst, wout, *, ts):
    bsz, seq, d = x.shape
    const2 = lambda b, s: (0, 0)
    return pl.pallas_call(
        functools.partial(_mix1_kernel, ts=ts),
        grid=(bsz, seq // ts),
        in_specs=[
            pl.BlockSpec((None, ts, d), lambda b, s: (b, s, 0)),
            pl.BlockSpec((None, 6, d), lambda b, s: (b, 0, 0)),
            pl.BlockSpec((1, d), const2),
            pl.BlockSpec((d, 2 * GM_WIDTH), const2),
            pl.BlockSpec((1, 2 * GM_WIDTH), const2),
            pl.BlockSpec((1, GM_WIDTH), const2),
            pl.BlockSpec((1, GM_WIDTH), const2),
            pl.BlockSpec((GM_GROUPS, GM_CHUNK, GM_CHUNK), lambda b, s: (0, 0, 0)),
            pl.BlockSpec((GM_CHUNK, GM_GROUPS), const2),
            pl.BlockSpec((GM_WIDTH, d), const2),
        ],
        out_specs=pl.BlockSpec((None, ts, d), lambda b, s: (b, s, 0)),
        out_shape=jax.ShapeDtypeStruct((bsz, seq, d), F32),
        scratch_shapes=[
            pltpu.VMEM((ts, d), BF16),
            pltpu.VMEM((ts, 2 * GM_WIDTH), F32),
            pltpu.VMEM((ts, GM_WIDTH), BF16),
        ],
        compiler_params=pltpu.CompilerParams(
            dimension_semantics=("parallel", "parallel"),
            vmem_limit_bytes=VMEM_LIMIT_BYTES),
        name="mixer_spatial_gating",
    )(x, mod, ng, win, b_in, ln_g, ln_b, ws, bst, wout)


def _ffn_kernel(x_ref, mod_ref, ng_ref, w1_ref, w2_ref, fg_ref, o_ref, hb_ref, *, ts, final):
    sh2 = mod_ref[3:4, :]
    sc2 = mod_ref[4:5, :]
    g2 = mod_ref[5:6, :]
    _norm_mod_to(x_ref, hb_ref, ng_ref[...] * (1.0 + sc2), sh2, ts)

    acc = None
    fcols = D_MODEL
    for j in range(D_FF // fcols):
        hid = jnp.maximum(_dot(hb_ref[...], w1_ref[:, j * fcols:(j + 1) * fcols]), 0.0)
        part = _dot((hid * hid).astype(BF16), w2_ref[j * fcols:(j + 1) * fcols, :])
        acc = part if acc is None else acc + part

    out = x_ref[...] + g2 * acc
    if final:
        inv = lax.rsqrt(jnp.mean(out * out, axis=-1, keepdims=True) + EPS)
        out = (out * inv) * fg_ref[...]
    o_ref[...] = out


def _ffn_call(x, mod, ng, w1, w2, fg, *, ts, final):
    bsz, seq, d = x.shape
    const2 = lambda b, s: (0, 0)
    return pl.pallas_call(
        functools.partial(_ffn_kernel, ts=ts, final=final),
        grid=(bsz, seq // ts),
        in_specs=[
            pl.BlockSpec((None, ts, d), lambda b, s: (b, s, 0)),
            pl.BlockSpec((None, 6, d), lambda b, s: (b, 0, 0)),
            pl.BlockSpec((1, d), const2),
            pl.BlockSpec((d, D_FF), const2, pipeline_mode=pl.Buffered(1)),
            pl.BlockSpec((D_FF, d), const2, pipeline_mode=pl.Buffered(1)),
            pl.BlockSpec((1, d), const2),
        ],
        out_specs=pl.BlockSpec((None, ts, d), lambda b, s: (b, s, 0)),
        out_shape=jax.ShapeDtypeStruct((bsz, seq, d), F32),
        scratch_shapes=[pltpu.VMEM((ts, d), BF16)],
        compiler_params=pltpu.CompilerParams(
            dimension_semantics=("parallel", "parallel"),
            vmem_limit_bytes=VMEM_LIMIT_BYTES),
        name="ffn_final" if final else "ffn",
    )(x, mod, ng, w1, w2, fg)


def kernel(x, c, ada_w, ada_b, norm_mix_g, norm_ffn_g, w_in0, conv_w, conv_b, hg_lb, hg_gain,
           w_out0, w_in1, b_in1, gm_ln_g, gm_ln_b, gm_ws, gm_bs, w_out1, w_ff1, w_ff2, final_g):
    bsz, seq, d = x.shape
    ts = min(SEQ_TILE, seq)
    mod = _ada_call(c, ada_w, ada_b).reshape(ada_w.shape[0], bsz, 6, d)
    fg = final_g.reshape(1, d)

    x = _mix0_call(x, mod[0], norm_mix_g[0:1], w_in0[0].astype(BF16), conv_w[0], conv_b[0:1],
                   hg_lb, hg_gain[0:1], w_out0[0].astype(BF16), ts=ts)
    x = _ffn_call(x, mod[0], norm_ffn_g[0:1], w_ff1[0].astype(BF16), w_ff2[0].astype(BF16), fg,
                  ts=ts, final=False)
    x = _mix1_call(x, mod[1], norm_mix_g[1:2], w_in1[0].astype(BF16), b_in1[0:1], gm_ln_g[0:1],
                   gm_ln_b[0:1], gm_ws[0], gm_bs[0].T, w_out1[0].astype(BF16), ts=ts)
    x = _ffn_call(x, mod[1], norm_ffn_g[1:2], w_ff1[1].astype(BF16), w_ff2[1].astype(BF16), fg,
                  ts=ts, final=True)
    return x
```

```python
import functools

import numpy as np
import jax
import jax.numpy as jnp
from jax import lax
from jax.experimental import pallas as pl
from jax.experimental.pallas import tpu as pltpu

F32 = jnp.float32
BF16 = jnp.bfloat16

EPS = 1e-6
D_MODEL = 1024
D_FF = 4 * D_MODEL
A_WIDTH = 512
B_WIDTH = 512
HG_HEADS = 4
HG_DK = 128
HG_DV = 128
IN0_COLS = 3 * A_WIDTH + 4 * B_WIDTH
GM_WIDTH = D_MODEL
GM_GROUPS = 4
GM_CHUNK = 128
GM_GW = GM_WIDTH // GM_GROUPS

SUBLANES = 8
LANES = 128
VMEM_LIMIT_BYTES = 56 * 1024 * 1024

SEQ_TILE = 512
ROW_BLOCK = 64
HG_CHUNK = 64
HG_LEVELS = (32, 16, 8, 4, 2, 1)
HG_BLOCK = 16
HG_EXP_GUARD = 80.0
PROJ_PIECE = 256
PROJ_EARLY = 4
_SQRT_2_OVER_PI = float(np.sqrt(2.0 / np.pi))


def _sigmoid(v):
    return 1.0 / (1.0 + jnp.exp(-v))


def _dot(a, b):
    return jnp.dot(a, b, preferred_element_type=F32)


def _dot_nt(a, b):
    return lax.dot_general(a, b, (((1,), (1,)), ((), ())), preferred_element_type=F32)


def _dot_tn(a, b):
    return lax.dot_general(a, b, (((0,), (0,)), ((), ())), preferred_element_type=F32)


def _norm_mod_block(x_ref, hb_ref, gs, sh, r):
    xt = x_ref[pl.ds(r, ROW_BLOCK), :]
    inv = lax.rsqrt(jnp.mean(xt * xt, axis=-1, keepdims=True) + EPS)
    hb_ref[pl.ds(r, ROW_BLOCK), :] = ((xt * inv) * gs + sh).astype(BF16)


def _norm_mod_to(x_ref, hb_ref, gs, sh, rows):
    def body(i, carry):
        _norm_mod_block(x_ref, hb_ref, gs, sh, pl.multiple_of(i * ROW_BLOCK, ROW_BLOCK))
        return carry

    lax.fori_loop(0, rows // ROW_BLOCK, body, 0)


def _ada_kernel(c_ref, w_ref, b_ref, o_ref):
    c = c_ref[...]
    s = c * _sigmoid(c)
    o_ref[...] = jnp.dot(s, w_ref[...], preferred_element_type=F32,
                         precision=lax.Precision.HIGHEST) + b_ref[...]


def _ada_call(c, ada_w, ada_b):
    depth, d, e = ada_w.shape
    bsz = c.shape[0]
    tn = 1536
    return pl.pallas_call(
        _ada_kernel,
        grid=(depth, e // tn),
        in_specs=[
            pl.BlockSpec((bsz, d), lambda l, j: (0, 0)),
            pl.BlockSpec((None, d, tn), lambda l, j: (l, 0, j)),
            pl.BlockSpec((None, 1, tn), lambda l, j: (l, 0, j)),
        ],
        out_specs=pl.BlockSpec((None, bsz, tn), lambda l, j: (l, 0, j)),
        out_shape=jax.ShapeDtypeStruct((depth, bsz, e), F32),
        compiler_params=pltpu.CompilerParams(
            dimension_semantics=("arbitrary", "arbitrary"),
            vmem_limit_bytes=VMEM_LIMIT_BYTES),
        name="ada_modulation",
    )(c, ada_w, ada_b.reshape(depth, 1, e))


def _level_masks():
    ti = lax.broadcasted_iota(jnp.int32, (HG_CHUNK, HG_CHUNK), 0)
    si = lax.broadcasted_iota(jnp.int32, (HG_CHUNK, HG_CHUNK), 1)
    masks = []
    for m in HG_LEVELS:
        other_blk = (ti ^ si) & ~(2 * m - 1)
        masks.append((other_blk | ((ti & m) ^ m) | (si & m)) == 0)
    blk_mask = (((ti ^ si) & ~(HG_BLOCK - 1)) | jnp.maximum(si - ti, 0)) == 0
    return masks, ti == si, blk_mask


def _block_start(c):
    rows, w = c.shape
    parts = [jnp.zeros((HG_BLOCK, w), c.dtype)]
    for b in range(1, rows // HG_BLOCK):
        parts.append(jnp.broadcast_to(c[b * HG_BLOCK - 1:b * HG_BLOCK, :], (HG_BLOCK, w)))
    return jnp.concatenate(parts, axis=0)


def _block_boundary(c, m):
    rows, w = c.shape
    if m >= SUBLANES:
        blk = 2 * m
        return jnp.concatenate(
            [jnp.broadcast_to(c[b * blk + m - 1:b * blk + m, :], (blk, w))
             for b in range(rows // blk)], axis=0)
    c3 = c.reshape(rows // SUBLANES, SUBLANES, w)
    if m == 4:
        bnd = jnp.broadcast_to(c3[:, 3:4, :], c3.shape)
    else:
        assert m == 2
        sub = lax.broadcasted_iota(jnp.int32, c3.shape, 1)
        bnd = jnp.where(sub < 4, jnp.broadcast_to(c3[:, 1:2, :], c3.shape),
                        jnp.broadcast_to(c3[:, 5:6, :], c3.shape))
    return bnd.reshape(rows, w)


def _chunk_cumsum(x):
    rows, w = x.shape
    nv = rows // SUBLANES
    x3 = x.reshape(nv, SUBLANES, w)
    sub = lax.broadcasted_iota(jnp.int32, x3.shape, 1)
    sh = 1
    while sh < SUBLANES:
        x3 = x3 + jnp.where(sub >= sh, pltpu.roll(x3, sh, axis=1), 0.0)
        sh *= 2
    tot = jnp.broadcast_to(x3[:, SUBLANES - 1:SUBLANES, :], x3.shape)
    outs = [x3[0]]
    acc = tot[0]
    for j in range(1, nv):
        outs.append(x3[j] + acc)
        if j + 1 < nv:
            acc = acc + tot[j]
    return jnp.concatenate(outs, axis=0)


def _mix0_kernel(xc_ref, xp_ref, modc_ref, modp_ref, ng_ref, win_ref, cw_ref, cb_ref, hglb_ref,
                 gain_ref, wout_ref, o_ref, hb_ref, pa_ref, pb_ref, y_ref, st_ref, zc_ref,
                 cum_ref, ck_ref, f_ref, *, ts, tiles_per_row):
    k = pl.program_id(0)

    @pl.when(jnp.maximum(k - 1, 0) % tiles_per_row == 0)
    def _():
        st_ref[...] = jnp.zeros_like(st_ref)
        zc_ref[...] = jnp.zeros_like(zc_ref)

    @pl.when(k == 0)
    def _():
        pb_ref[...] = jnp.zeros_like(pb_ref)

    step = functools.partial(_mix0_step, xc_ref, xp_ref, modc_ref, modp_ref, ng_ref, win_ref,
                             cw_ref, cb_ref, hglb_ref, gain_ref, wout_ref, o_ref, hb_ref, y_ref,
                             st_ref, zc_ref, cum_ref, ck_ref, f_ref, ts)
    pl.when(k % 2 == 0)(functools.partial(step, pa_ref, pb_ref))
    pl.when(k % 2 == 1)(functools.partial(step, pb_ref, pa_ref))


def _mix0_step(xc_ref, xp_ref, modc_ref, modp_ref, ng_ref, win_ref, cw_ref, cb_ref, hglb_ref,
               gain_ref, wout_ref, o_ref, hb_ref, y_ref, st_ref, zc_ref, cum_ref, ck_ref, f_ref,
               ts, p_cur, p_prv):
    n_chunks = ts // HG_CHUNK

    _norm_mod_to(xc_ref, hb_ref, ng_ref[...] * (1.0 + modc_ref[1:2, :]), modc_ref[0:1, :], ts)

    a0, a1, a2 = hglb_ref[0:1, :], hglb_ref[1:2, :], hglb_ref[2:3, :]
    amax = jnp.maximum(jnp.maximum(a0, a1), a2)
    e0, e1, e2 = jnp.exp(a0 - amax), jnp.exp(a1 - amax), jnp.exp(a2 - amax)
    lb = e0 / (e0 + e1 + e2)

    cw0, cw1, cw2 = cw_ref[0:1, :], cw_ref[1:2, :], cw_ref[2:3, :]
    cb = cb_ref[...]
    gain = gain_ref[...]
    row8 = lax.broadcasted_iota(jnp.int32, (SUBLANES, A_WIDTH), 0)
    masks, diag, blk_mask = _level_masks()

    pieces = [slice(c, c + PROJ_PIECE) for c in range(0, IN0_COLS, PROJ_PIECE)]

    def project(cs):
        p_cur[:, cs] = _dot(hb_ref[...], win_ref[:, cs])

    def drain(queue, slots):
        state = {"calls": 0, "done": 0}

        def mxu_work():
            state["calls"] += 1
            want = min(len(queue), -(-len(queue) * state["calls"] // slots))
            while state["done"] < want:
                project(queue[state["done"]])
                state["done"] += 1
        return mxu_work

    def gates_chunk(ci, mxu_work):
        rows = slice(ci * HG_CHUNK, (ci + 1) * HG_CHUNK)

        def col(j):
            return p_prv[rows, j * 512:(j + 1) * 512]

        z = col(1) * col(2)
        prev = zc_ref[...]
        zc_ref[...] = z[HG_CHUNK - SUBLANES:, :]

        def shifted(k):
            zr = pltpu.roll(z, k, axis=0)
            pr = pltpu.roll(prev, k, axis=0)
            head = jnp.where(row8 < k, pr, zr[:SUBLANES])
            return jnp.concatenate([head, zr[SUBLANES:]], axis=0)

        conv = cb + shifted(2) * cw0
        conv = conv + shifted(1) * cw1
        conv = conv + z * cw2
        y_ref[rows, 0:A_WIDTH] = (col(0) * conv).astype(BF16)
        mxu_work()

        f = lb + (1.0 - lb) * _sigmoid(col(4))
        cum = _chunk_cumsum(jnp.log(f))
        ck = cum - jnp.log(1.0 - f)
        f_ref[rows, :] = f
        cum_ref[rows, :] = cum
        ck_ref[rows, :] = ck
        mxu_work()
        return _block_start(cum) - ck

    def recur_chunk(ci, mxu_work, fast):
        rows = slice(ci * HG_CHUNK, (ci + 1) * HG_CHUNK)
        q = p_prv[rows, 3 * 512:4 * 512]
        v = p_prv[rows, 5 * 512:6 * 512]
        gate = p_prv[rows, 6 * 512:7 * 512]
        cum = cum_ref[rows, :]
        ck = ck_ref[rows, :]
        last = cum[HG_CHUNK - 1:HG_CHUNK, :]
        q_in = q * jnp.exp(cum)
        k_dec = jnp.exp(last - ck)
        dec = jnp.exp(last)
        gsil = gate * _sigmoid(gate)
        if not fast:
            f = f_ref[rows, :]
            kk = 1.0 - f
            qf = q * f

        for h in range(HG_HEADS):
            mxu_work()
            hs = slice(h * HG_DK, (h + 1) * HG_DK)
            q_h, cum_h, ck_h = q[:, hs], cum[:, hs], ck[:, hs]
            v_b = v[:, hs].astype(BF16)
            st = st_ref[h]
            o = _dot_nt(q_in[:, hs].astype(BF16), st.astype(BF16))

            if fast:
                bnd = _block_start(cum_h)
                qt = (q_h * jnp.exp(cum_h - bnd)).astype(BF16)
                kt = jnp.exp(bnd - ck_h).astype(BF16)
                scores = jnp.where(blk_mask, _dot_nt(qt, kt), 0.0)
                levels = [(m, mask) for m, mask in zip(HG_LEVELS, masks) if m >= HG_BLOCK]
            else:
                k_b = kk[:, hs].astype(BF16)
                scores = jnp.where(diag, _dot_nt(q_h.astype(BF16), k_b), 0.0)
                levels = list(zip(HG_LEVELS, masks))
            for m, mask in levels:
                if m == 1:
                    qt, kt = qf[:, hs].astype(BF16), k_b
                else:
                    bnd = _block_boundary(cum_h, m)
                    qt = (q_h * jnp.exp(cum_h - bnd)).astype(BF16)
                    kt = jnp.exp(bnd - ck_h).astype(BF16)
                scores = jnp.where(mask, _dot_nt(qt, kt), scores)
            o = o + _dot(scores.astype(BF16), v_b)

            st_ref[h] = st * dec[:, hs] + _dot_tn(v_b, k_dec[:, hs].astype(BF16))

            o = o * lax.rsqrt(jnp.mean(o * o, axis=-1, keepdims=True) + EPS)
            y_ref[rows, A_WIDTH + h * HG_DV:A_WIDTH + (h + 1) * HG_DV] = (
                o * gain[:, hs] * gsil[:, hs]).astype(BF16)

    early, late = pieces[:PROJ_EARLY], pieces[PROJ_EARLY:]
    work = drain(early, 2 * n_chunks)
    worst = None
    for ci in range(n_chunks):
        e = gates_chunk(ci, work)
        worst = e if worst is None else jnp.maximum(worst, e)
    fast_ok = jnp.max(worst) < HG_EXP_GUARD

    def recur_tile(fast):
        work = drain(late, HG_HEADS * n_chunks)
        for ci in range(n_chunks):
            recur_chunk(ci, work, fast)

    pl.when(fast_ok)(functools.partial(recur_tile, True))
    pl.when(jnp.logical_not(fast_ok))(functools.partial(recur_tile, False))

    o_ref[...] = xp_ref[...] + modp_ref[2:3, :] * _dot(y_ref[...], wout_ref[...])


def _mix0_call(x, mod, ng, win, cw, cb, hglb, gain, wout, *, ts):
    bsz, seq, d = x.shape
    tiles_per_row = seq // ts
    n_tiles = bsz * tiles_per_row
    assert ts % HG_CHUNK == 0
    cur = lambda k: jnp.minimum(k, n_tiles - 1)
    prv = lambda k: jnp.maximum(k - 1, 0)
    const2 = lambda k: (0, 0)
    out = pl.pallas_call(
        functools.partial(_mix0_kernel, ts=ts, tiles_per_row=tiles_per_row),
        grid=(n_tiles + 1,),
        in_specs=[
            pl.BlockSpec((None, ts, d), lambda k: (cur(k), 0, 0)),
            pl.BlockSpec((None, ts, d), lambda k: (prv(k), 0, 0)),
            pl.BlockSpec((None, 6, d), lambda k: (cur(k) // tiles_per_row, 0, 0)),
            pl.BlockSpec((None, 6, d), lambda k: (prv(k) // tiles_per_row, 0, 0)),
            pl.BlockSpec((1, d), const2),
            pl.BlockSpec((d, IN0_COLS), const2, pipeline_mode=pl.Buffered(1)),
            pl.BlockSpec((3, A_WIDTH), const2),
            pl.BlockSpec((1, A_WIDTH), const2),
            pl.BlockSpec((3, B_WIDTH), const2),
            pl.BlockSpec((1, B_WIDTH), const2),
            pl.BlockSpec((d, d), const2, pipeline_mode=pl.Buffered(1)),
        ],
        out_specs=pl.BlockSpec((None, ts, d), lambda k: (prv(k), 0, 0)),
        out_shape=jax.ShapeDtypeStruct((n_tiles, ts, d), F32),
        scratch_shapes=[
            pltpu.VMEM((ts, d), BF16),
            pltpu.VMEM((ts, IN0_COLS), F32),
            pltpu.VMEM((ts, IN0_COLS), F32),
            pltpu.VMEM((ts, d), BF16),
            pltpu.VMEM((HG_HEADS, HG_DV, HG_DK), F32),
            pltpu.VMEM((SUBLANES, A_WIDTH), F32),
            pltpu.VMEM((ts, B_WIDTH), F32),
            pltpu.VMEM((ts, B_WIDTH), F32),
            pltpu.VMEM((ts, B_WIDTH), F32),
        ],
        compiler_params=pltpu.CompilerParams(
            dimension_semantics=("arbitrary",),
            vmem_limit_bytes=VMEM_LIMIT_BYTES),
        name="mixer_conv_hgrn2",
    )(x.reshape(n_tiles, ts, d), x.reshape(n_tiles, ts, d), mod, mod, ng, win, cw, cb, hglb,
      gain, wout)
    return out.reshape(bsz, seq, d)


def _mix1_kernel(x_ref, mod_ref, ng_ref, win_ref, bin_ref, lng_ref, lnb_ref, ws_ref,
                 bst_ref, wout_ref, o_ref, hb_ref, z_ref, y_ref, *, ts):
    sh1 = mod_ref[0:1, :]
    sc1 = mod_ref[1:2, :]
    g1 = mod_ref[2:3, :]
    gs1 = ng_ref[...] * (1.0 + sc1)
    _norm_mod_to(x_ref, hb_ref, gs1, sh1, ts // 2)

    b_in = bin_ref[...]
    ln_g = lng_ref[...]
    ln_b = lnb_ref[...]
    bst = bst_ref[...]
    ti = lax.broadcasted_iota(jnp.int32, (GM_CHUNK, GM_CHUNK), 0)
    si = lax.broadcasted_iota(jnp.int32, (GM_CHUNK, GM_CHUNK), 1)
    w_tri = [jnp.where(si <= ti, ws_ref[g], 0.0).astype(BF16) for g in range(GM_GROUPS)]

    def gelu(t):
        return t * (0.5 * (1.0 + jnp.tanh(_SQRT_2_OVER_PI * (t + 0.044715 * (t * t * t)))))

    def project(rows, cs):
        z_ref[rows, cs] = _dot(hb_ref[rows, :], win_ref[:, cs])

    def emit_out(rows, cs):
        o_ref[rows, cs] = x_ref[rows, cs] + g1[:, cs] * _dot(y_ref[rows, :], wout_ref[:, cs])

    def gate_chunk(ci, mxu_work):
        rows = slice(ci * GM_CHUNK, (ci + 1) * GM_CHUNK)
        u = gelu(z_ref[rows, 0:GM_WIDTH] + b_in[:, 0:GM_WIDTH])
        mxu_work()
        v = gelu(z_ref[rows, GM_WIDTH:2 * GM_WIDTH] + b_in[:, GM_WIDTH:])
        mxu_work()
        mu = jnp.mean(v, axis=-1, keepdims=True)
        vc = v - mu
        inv = lax.rsqrt(jnp.mean(vc * vc, axis=-1, keepdims=True) + EPS)
        vn = ((vc * inv) * ln_g + ln_b).astype(BF16)
        mxu_work()
        for g in range(GM_GROUPS):
            gs = slice(g * GM_GW, (g + 1) * GM_GW)
            mixed = _dot(w_tri[g], vn[:, gs]) + bst[:, g:g + 1]
            y_ref[rows, gs] = (u[:, gs] * mixed).astype(BF16)
        mxu_work()

    half = ts // 2
    rows_a, rows_b = slice(0, half), slice(half, ts)
    chunks_per_half = half // GM_CHUNK
    in_pieces = [slice(c, c + PROJ_PIECE) for c in range(0, 2 * GM_WIDTH, PROJ_PIECE)]
    out_pieces = [slice(c, c + PROJ_PIECE) for c in range(0, x_ref.shape[1], PROJ_PIECE)]

    def drain(queue, per_call):
        def mxu_work():
            for _ in range(per_call):
                if queue:
                    queue.pop(0)()
        return mxu_work

    norm_b = [functools.partial(_norm_mod_block, x_ref, hb_ref, gs1, sh1, r)
              for r in range(half, ts, ROW_BLOCK)]
    for i, cs in enumerate(in_pieces):
        project(rows_a, cs)
        for blk in norm_b[i * len(norm_b) // len(in_pieces):
                          (i + 1) * len(norm_b) // len(in_pieces)]:
            blk()
    queue = [functools.partial(project, rows_b, cs) for cs in in_pieces]
    per_call = -(-len(queue) // (4 * chunks_per_half))
    for ci in range(chunks_per_half):
        gate_chunk(ci, drain(queue, per_call))
    assert not queue
    queue = [functools.partial(emit_out, rows_a, cs) for cs in out_pieces]
    per_call = -(-len(queue) // (4 * chunks_per_half))
    for ci in range(chunks_per_half, 2 * chunks_per_half):
        gate_chunk(ci, drain(queue, per_call))
    assert not queue
    for cs in out_pieces:
        emit_out(rows_b, cs)


def _mix1_call(x, mod, ng, win, b_in, ln_g, ln_b, ws, bst, wout, *, ts):
    bsz, seq, d = x.shape
    const2 = lambda b, s: (0, 0)
    return pl.pallas_call(
        functools.partial(_mix1_kernel, ts=ts),
        grid=(bsz, seq // ts),
        in_specs=[
            pl.BlockSpec((None, ts, d), lambda b, s: (b, s, 0)),
            pl.BlockSpec((None, 6, d), lambda b, s: (b, 0, 0)),
            pl.BlockSpec((1, d), const2),
            pl.BlockSpec((d, 2 * GM_WIDTH), const2),
            pl.BlockSpec((1, 2 * GM_WIDTH), const2),
            pl.BlockSpec((1, GM_WIDTH), const2),
            pl.BlockSpec((1, GM_WIDTH), const2),
            pl.BlockSpec((GM_GROUPS, GM_CHUNK, GM_CHUNK), lambda b, s: (0, 0, 0)),
            pl.BlockSpec((GM_CHUNK, GM_GROUPS), const2),
            pl.BlockSpec((GM_WIDTH, d), const2),
        ],
        out_specs=pl.BlockSpec((None, ts, d), lambda b, s: (b, s, 0)),
        out_shape=jax.ShapeDtypeStruct((bsz, seq, d), F32),
        scratch_shapes=[
            pltpu.VMEM((ts, d), BF16),
            pltpu.VMEM((ts, 2 * GM_WIDTH), F32),
            pltpu.VMEM((ts, GM_WIDTH), BF16),
        ],
        compiler_params=pltpu.CompilerParams(
            dimension_semantics=("parallel", "parallel"),
            vmem_limit_bytes=VMEM_LIMIT_BYTES),
        name="mixer_spatial_gating",
    )(x, mod, ng, win, b_in, ln_g, ln_b, ws, bst, wout)


def _ffn_kernel(x_ref, mod_ref, ng_ref, w1_ref, w2_ref, fg_ref, o_ref, hb_ref, *, ts, final):
    sh2 = mod_ref[3:4, :]
    sc2 = mod_ref[4:5, :]
    g2 = mod_ref[5:6, :]
    _norm_mod_to(x_ref, hb_ref, ng_ref[...] * (1.0 + sc2), sh2, ts)

    acc = None
    fcols = D_MODEL
    for j in range(D_FF // fcols):
        hid = jnp.maximum(_dot(hb_ref[...], w1_ref[:, j * fcols:(j + 1) * fcols]), 0.0)
        part = _dot((hid * hid).astype(BF16), w2_ref[j * fcols:(j + 1) * fcols, :])
        acc = part if acc is None else acc + part

    out = x_ref[...] + g2 * acc
    if final:
        inv = lax.rsqrt(jnp.mean(out * out, axis=-1, keepdims=True) + EPS)
        out = (out * inv) * fg_ref[...]
    o_ref[...] = out


def _ffn_call(x, mod, ng, w1, w2, fg, *, ts, final):
    bsz, seq, d = x.shape
    const2 = lambda b, s: (0, 0)
    return pl.pallas_call(
        functools.partial(_ffn_kernel, ts=ts, final=final),
        grid=(bsz, seq // ts),
        in_specs=[
            pl.BlockSpec((None, ts, d), lambda b, s: (b, s, 0)),
            pl.BlockSpec((None, 6, d), lambda b, s: (b, 0, 0)),
            pl.BlockSpec((1, d), const2),
            pl.BlockSpec((d, D_FF), const2, pipeline_mode=pl.Buffered(1)),
            pl.BlockSpec((D_FF, d), const2, pipeline_mode=pl.Buffered(1)),
            pl.BlockSpec((1, d), const2),
        ],
        out_specs=pl.BlockSpec((None, ts, d), lambda b, s: (b, s, 0)),
        out_shape=jax.ShapeDtypeStruct((bsz, seq, d), F32),
        scratch_shapes=[pltpu.VMEM((ts, d), BF16)],
        compiler_params=pltpu.CompilerParams(
            dimension_semantics=("parallel", "parallel"),
            vmem_limit_bytes=VMEM_LIMIT_BYTES),
        name="ffn_final" if final else "ffn",
    )(x, mod, ng, w1, w2, fg)


def kernel(x, c, ada_w, ada_b, norm_mix_g, norm_ffn_g, w_in0, conv_w, conv_b, hg_lb, hg_gain,
           w_out0, w_in1, b_in1, gm_ln_g, gm_ln_b, gm_ws, gm_bs, w_out1, w_ff1, w_ff2, final_g):
    bsz, seq, d = x.shape
    ts = min(SEQ_TILE, seq)
    mod = _ada_call(c, ada_w, ada_b).reshape(ada_w.shape[0], bsz, 6, d)
    fg = final_g.reshape(1, d)

    x = _mix0_call(x, mod[0], norm_mix_g[0:1], w_in0[0].astype(BF16), conv_w[0], conv_b[0:1],
                   hg_lb, hg_gain[0:1], w_out0[0].astype(BF16), ts=ts)
    x = _ffn_call(x, mod[0], norm_ffn_g[0:1], w_ff1[0].astype(BF16), w_ff2[0].astype(BF16), fg,
                  ts=ts, final=False)
    x = _mix1_call(x, mod[1], norm_mix_g[1:2], w_in1[0].astype(BF16), b_in1[0:1], gm_ln_g[0:1],
                   gm_ln_b[0:1], gm_ws[0], gm_bs[0].T, w_out1[0].astype(BF16), ts=ts)
    x = _ffn_call(x, mod[1], norm_ffn_g[1:2], w_ff1[1].astype(BF16), w_ff2[1].astype(BF16), fg,
                  ts=ts, final=True)
    return x
```

```python
import functools

import numpy as np
import jax
import jax.numpy as jnp
from jax import lax
from jax.experimental import pallas as pl
from jax.experimental.pallas import tpu as pltpu

F32 = jnp.float32
BF16 = jnp.bfloat16

EPS = 1e-6
D_MODEL = 1024
D_FF = 4 * D_MODEL
A_WIDTH = 512
B_WIDTH = 512
HG_HEADS = 4
HG_DK = 128
HG_DV = 128
IN0_COLS = 3 * A_WIDTH + 4 * B_WIDTH
GM_WIDTH = D_MODEL
GM_GROUPS = 4
GM_CHUNK = 128
GM_GW = GM_WIDTH // GM_GROUPS

SUBLANES = 8
LANES = 128
VMEM_LIMIT_BYTES = 56 * 1024 * 1024

SEQ_TILE = 512
ROW_BLOCK = 64
HG_CHUNK = 64
HG_LEVELS = (32, 16, 8, 4, 2, 1)
HG_BLOCK = 16
HG_EXP_GUARD = 80.0
PROJ_PIECE = 256

_SQRT_2_OVER_PI = float(np.sqrt(2.0 / np.pi))


def _sigmoid(v):
    return 1.0 / (1.0 + jnp.exp(-v))


def _dot(a, b):
    return jnp.dot(a, b, preferred_element_type=F32)


def _dot_nt(a, b):
    return lax.dot_general(a, b, (((1,), (1,)), ((), ())), preferred_element_type=F32)


def _dot_tn(a, b):
    return lax.dot_general(a, b, (((0,), (0,)), ((), ())), preferred_element_type=F32)


def _norm_mod_block(x_ref, hb_ref, gs, sh, r):
    xt = x_ref[pl.ds(r, ROW_BLOCK), :]
    inv = lax.rsqrt(jnp.mean(xt * xt, axis=-1, keepdims=True) + EPS)
    hb_ref[pl.ds(r, ROW_BLOCK), :] = ((xt * inv) * gs + sh).astype(BF16)


def _norm_mod_to(x_ref, hb_ref, gs, sh, rows):
    def body(i, carry):
        _norm_mod_block(x_ref, hb_ref, gs, sh, pl.multiple_of(i * ROW_BLOCK, ROW_BLOCK))
        return carry

    lax.fori_loop(0, rows // ROW_BLOCK, body, 0)


def _ada_kernel(c_ref, w_ref, b_ref, o_ref):
    c = c_ref[...]
    s = c * _sigmoid(c)
    o_ref[...] = jnp.dot(s, w_ref[...], preferred_element_type=F32,
                         precision=lax.Precision.HIGHEST) + b_ref[...]


def _ada_call(c, ada_w, ada_b):
    depth, d, e = ada_w.shape
    bsz = c.shape[0]
    tn = 1536
    return pl.pallas_call(
        _ada_kernel,
        grid=(depth, e // tn),
        in_specs=[
            pl.BlockSpec((bsz, d), lambda l, j: (0, 0)),
            pl.BlockSpec((None, d, tn), lambda l, j: (l, 0, j)),
            pl.BlockSpec((None, 1, tn), lambda l, j: (l, 0, j)),
        ],
        out_specs=pl.BlockSpec((None, bsz, tn), lambda l, j: (l, 0, j)),
        out_shape=jax.ShapeDtypeStruct((depth, bsz, e), F32),
        compiler_params=pltpu.CompilerParams(
            dimension_semantics=("arbitrary", "arbitrary"),
            vmem_limit_bytes=VMEM_LIMIT_BYTES),
        name="ada_modulation",
    )(c, ada_w, ada_b.reshape(depth, 1, e))


def _level_masks():
    ti = lax.broadcasted_iota(jnp.int32, (HG_CHUNK, HG_CHUNK), 0)
    si = lax.broadcasted_iota(jnp.int32, (HG_CHUNK, HG_CHUNK), 1)
    masks = []
    for m in HG_LEVELS:
        other_blk = (ti ^ si) & ~(2 * m - 1)
        masks.append((other_blk | ((ti & m) ^ m) | (si & m)) == 0)
    blk_mask = (((ti ^ si) & ~(HG_BLOCK - 1)) | jnp.maximum(si - ti, 0)) == 0
    return masks, ti == si, blk_mask


def _block_start(c):
    rows, w = c.shape
    parts = [jnp.zeros((HG_BLOCK, w), c.dtype)]
    for b in range(1, rows // HG_BLOCK):
        parts.append(jnp.broadcast_to(c[b * HG_BLOCK - 1:b * HG_BLOCK, :], (HG_BLOCK, w)))
    return jnp.concatenate(parts, axis=0)


def _block_boundary(c, m):
    rows, w = c.shape
    if m >= SUBLANES:
        blk = 2 * m
        return jnp.concatenate(
            [jnp.broadcast_to(c[b * blk + m - 1:b * blk + m, :], (blk, w))
             for b in range(rows // blk)], axis=0)
    c3 = c.reshape(rows // SUBLANES, SUBLANES, w)
    if m == 4:
        bnd = jnp.broadcast_to(c3[:, 3:4, :], c3.shape)
    else:
        assert m == 2
        sub = lax.broadcasted_iota(jnp.int32, c3.shape, 1)
        bnd = jnp.where(sub < 4, jnp.broadcast_to(c3[:, 1:2, :], c3.shape),
                        jnp.broadcast_to(c3[:, 5:6, :], c3.shape))
    return bnd.reshape(rows, w)


def _chunk_cumsum(x):
    rows, w = x.shape
    nv = rows // SUBLANES
    x3 = x.reshape(nv, SUBLANES, w)
    sub = lax.broadcasted_iota(jnp.int32, x3.shape, 1)
    sh = 1
    while sh < SUBLANES:
        x3 = x3 + jnp.where(sub >= sh, pltpu.roll(x3, sh, axis=1), 0.0)
        sh *= 2
    tot = jnp.broadcast_to(x3[:, SUBLANES - 1:SUBLANES, :], x3.shape)
    outs = [x3[0]]
    acc = tot[0]
    for j in range(1, nv):
        outs.append(x3[j] + acc)
        if j + 1 < nv:
            acc = acc + tot[j]
    return jnp.concatenate(outs, axis=0)


def _spread(items, slots):
    state = {"calls": 0, "done": 0}

    def run_due():
        state["calls"] += 1
        want = min(len(items), -(-len(items) * state["calls"] // slots))
        while state["done"] < want:
            items[state["done"]]()
            state["done"] += 1
    return run_due


def _mix0_kernel(x_ref, mod_ref, ng_ref, win_ref, cw_ref, cb_ref, hglb_ref, gain_ref, wout_ref,
                 o_ref, hb_ref, p_ref, y_ref, st_ref, zc_ref, cum_ref, ck_ref, f_ref, worst_ref,
                 *, ts):
    @pl.when(pl.program_id(1) == 0)
    def _():
        st_ref[...] = jnp.zeros_like(st_ref)
        zc_ref[...] = jnp.zeros_like(zc_ref)

    half = ts // 2
    rows_a, rows_b = slice(0, half), slice(half, ts)
    chunks_per_half = half // HG_CHUNK
    sh1 = mod_ref[0:1, :]
    gs1 = ng_ref[...] * (1.0 + mod_ref[1:2, :])
    g1 = mod_ref[2:3, :]
    _norm_mod_to(x_ref, hb_ref, gs1, sh1, half)

    a0, a1, a2 = hglb_ref[0:1, :], hglb_ref[1:2, :], hglb_ref[2:3, :]
    amax = jnp.maximum(jnp.maximum(a0, a1), a2)
    e0, e1, e2 = jnp.exp(a0 - amax), jnp.exp(a1 - amax), jnp.exp(a2 - amax)
    lb = e0 / (e0 + e1 + e2)

    cw0, cw1, cw2 = cw_ref[0:1, :], cw_ref[1:2, :], cw_ref[2:3, :]
    cb = cb_ref[...]
    gain = gain_ref[...]
    row8 = lax.broadcasted_iota(jnp.int32, (SUBLANES, A_WIDTH), 0)
    masks, diag, blk_mask = _level_masks()

    in_pieces = [slice(c, c + PROJ_PIECE) for c in range(0, IN0_COLS, PROJ_PIECE)]
    out_pieces = [slice(c, c + PROJ_PIECE) for c in range(0, D_MODEL, PROJ_PIECE)]

    def project(rows, cs):
        p_ref[rows, cs] = _dot(hb_ref[rows, :], win_ref[:, cs])

    def emit_out(rows, cs):
        o_ref[rows, cs] = x_ref[rows, cs] + g1[:, cs] * _dot(y_ref[rows, :], wout_ref[:, cs])

    def gates_chunk(ci, other_work):
        rows = slice(ci * HG_CHUNK, (ci + 1) * HG_CHUNK)

        def col(j):
            return p_ref[rows, j * 512:(j + 1) * 512]

        z = col(1) * col(2)
        prev = zc_ref[...]
        zc_ref[...] = z[HG_CHUNK - SUBLANES:, :]

        def shifted(k):
            zr = pltpu.roll(z, k, axis=0)
            pr = pltpu.roll(prev, k, axis=0)
            head = jnp.where(row8 < k, pr, zr[:SUBLANES])
            return jnp.concatenate([head, zr[SUBLANES:]], axis=0)

        conv = cb + shifted(2) * cw0
        conv = conv + shifted(1) * cw1
        conv = conv + z * cw2
        y_ref[rows, 0:A_WIDTH] = (col(0) * conv).astype(BF16)
        other_work()

        f = lb + (1.0 - lb) * _sigmoid(col(4))
        cum = _chunk_cumsum(jnp.log(f))
        ck = cum - jnp.log(1.0 - f)
        f_ref[rows, :] = f
        cum_ref[rows, :] = cum
        ck_ref[rows, :] = ck
        e = _block_start(cum) - ck
        worst_ref[...] = e if ci % chunks_per_half == 0 else jnp.maximum(worst_ref[...], e)
        other_work()


    def scores_stage(ci, fast):
        rows = slice(ci * HG_CHUNK, (ci + 1) * HG_CHUNK)
        q = p_ref[rows, 3 * 512:4 * 512]
        v = p_ref[rows, 5 * 512:6 * 512]
        cum = cum_ref[rows, :]
        ck = ck_ref[rows, :]
        last = cum[HG_CHUNK - 1:HG_CHUNK, :]
        q_in = q * jnp.exp(cum)
        k_dec = jnp.exp(last - ck)
        if not fast:
            f = f_ref[rows, :]
            kk = 1.0 - f
            qf = q * f

        heads = []
        for h in range(HG_HEADS):
            hs = slice(h * HG_DK, (h + 1) * HG_DK)
            q_h, cum_h, ck_h = q[:, hs], cum[:, hs], ck[:, hs]
            if fast:
                bnd = _block_start(cum_h)
                qt = (q_h * jnp.exp(cum_h - bnd)).astype(BF16)
                kt = jnp.exp(bnd - ck_h).astype(BF16)
                parts = [(blk_mask, _dot_nt(qt, kt))]
                levels = [(m, mask) for m, mask in zip(HG_LEVELS, masks) if m >= HG_BLOCK]
            else:
                k_b = kk[:, hs].astype(BF16)
                parts = [(diag, _dot_nt(q_h.astype(BF16), k_b))]
                levels = list(zip(HG_LEVELS, masks))
            for m, mask in levels:
                if m == 1:
                    qt, kt = qf[:, hs].astype(BF16), k_b
                else:
                    bnd = _block_boundary(cum_h, m)
                    qt = (q_h * jnp.exp(cum_h - bnd)).astype(BF16)
                    kt = jnp.exp(bnd - ck_h).astype(BF16)
                parts.append((mask, _dot_nt(qt, kt)))
            heads.append(dict(parts=parts, q_in=q_in[:, hs].astype(BF16),
                              k_dec=k_dec[:, hs].astype(BF16), v=v[:, hs].astype(BF16)))
        return dict(rows=rows, heads=heads, dec=jnp.exp(last))

    def state_stage(ctx):
        for h, hd in enumerate(ctx["heads"]):
            hs = slice(h * HG_DK, (h + 1) * HG_DK)
            scores = None
            for mask, s in hd["parts"]:
                scores = jnp.where(mask, s, 0.0 if scores is None else scores)
            st = st_ref[h]
            hd["o"] = _dot_nt(hd["q_in"], st.astype(BF16)) + _dot(scores.astype(BF16), hd["v"])
            st_ref[h] = st * ctx["dec"][:, hs] + _dot_tn(hd["v"], hd["k_dec"])

    def output_stage(ctx):
        rows = ctx["rows"]
        gate = p_ref[rows, 6 * 512:7 * 512]
        gsil = gate * _sigmoid(gate)
        for h, hd in enumerate(ctx["heads"]):
            hs = slice(h * HG_DK, (h + 1) * HG_DK)
            o = hd["o"]
            o = o * lax.rsqrt(jnp.mean(o * o, axis=-1, keepdims=True) + EPS)
            y_ref[rows, A_WIDTH + h * HG_DV:A_WIDTH + (h + 1) * HG_DV] = (
                o * gain[:, hs] * gsil[:, hs]).astype(BF16)

    chunks_a = range(chunks_per_half)
    chunks_b = range(chunks_per_half, 2 * chunks_per_half)
    no_work = lambda: None

    norm_b = _spread([functools.partial(_norm_mod_block, x_ref, hb_ref, gs1, sh1, r)
                      for r in range(half, ts, ROW_BLOCK)], len(in_pieces))
    for cs in in_pieces:
        project(rows_a, cs)
        norm_b()

    proj_b = _spread([functools.partial(project, rows_b, cs) for cs in in_pieces],
                     2 * chunks_per_half)
    for ci in chunks_a:
        gates_chunk(ci, proj_b)

    def recur_half(chunks, other_items, fast):
        chunks = list(chunks)
        n = len(chunks)
        other_work = _spread(other_items, 3 * (n + 2))
        ctx = {}
        for t in range(n + 2):
            if t < n:
                ctx[t] = scores_stage(chunks[t], fast)
            other_work()
            if 0 <= t - 1 < n:
                state_stage(ctx[t - 1])
            other_work()
            if 0 <= t - 2 < n:
                output_stage(ctx.pop(t - 2))
            other_work()

    def both_paths(chunks, other_items):
        fast_ok = jnp.max(worst_ref[...]) < HG_EXP_GUARD
        pl.when(fast_ok)(functools.partial(recur_half, chunks, other_items, True))
        pl.when(jnp.logical_not(fast_ok))(
            functools.partial(recur_half, chunks, other_items, False))

    both_paths(chunks_a, [functools.partial(gates_chunk, ci, no_work) for ci in chunks_b])
    both_paths(chunks_b, [functools.partial(emit_out, rows_a, cs) for cs in out_pieces])
    for cs in out_pieces:
        emit_out(rows_b, cs)


def _mix0_call(x, mod, ng, win, cw, cb, hglb, gain, wout, *, ts):
    bsz, seq, d = x.shape
    assert ts % (2 * HG_CHUNK) == 0
    const2 = lambda b, s: (0, 0)
    return pl.pallas_call(
        functools.partial(_mix0_kernel, ts=ts),
        grid=(bsz, seq // ts),
        in_specs=[
            pl.BlockSpec((None, ts, d), lambda b, s: (b, s, 0)),
            pl.BlockSpec((None, 6, d), lambda b, s: (b, 0, 0)),
            pl.BlockSpec((1, d), const2),
            pl.BlockSpec((d, IN0_COLS), const2, pipeline_mode=pl.Buffered(1)),
            pl.BlockSpec((3, A_WIDTH), const2),
            pl.BlockSpec((1, A_WIDTH), const2),
            pl.BlockSpec((3, B_WIDTH), const2),
            pl.BlockSpec((1, B_WIDTH), const2),
            pl.BlockSpec((d, d), const2, pipeline_mode=pl.Buffered(1)),
        ],
        out_specs=pl.BlockSpec((None, ts, d), lambda b, s: (b, s, 0)),
        out_shape=jax.ShapeDtypeStruct((bsz, seq, d), F32),
        scratch_shapes=[
            pltpu.VMEM((ts, d), BF16),
            pltpu.VMEM((ts, IN0_COLS), F32),
            pltpu.VMEM((ts, d), BF16),
            pltpu.VMEM((HG_HEADS, HG_DV, HG_DK), F32),
            pltpu.VMEM((SUBLANES, A_WIDTH), F32),
            pltpu.VMEM((ts, B_WIDTH), F32),
            pltpu.VMEM((ts, B_WIDTH), F32),
            pltpu.VMEM((ts, B_WIDTH), F32),
            pltpu.VMEM((HG_CHUNK, B_WIDTH), F32),
        ],
        compiler_params=pltpu.CompilerParams(
            dimension_semantics=("parallel", "arbitrary"),
            vmem_limit_bytes=VMEM_LIMIT_BYTES),
        name="mixer_conv_hgrn2",
    )(x, mod, ng, win, cw, cb, hglb, gain, wout)


def _mix1_kernel(x_ref, mod_ref, ng_ref, win_ref, bin_ref, lng_ref, lnb_ref, ws_ref,
                 bst_ref, wout_ref, o_ref, hb_ref, z_ref, y_ref, *, ts):
    sh1 = mod_ref[0:1, :]
    sc1 = mod_ref[1:2, :]
    g1 = mod_ref[2:3, :]
    gs1 = ng_ref[...] * (1.0 + sc1)
    _norm_mod_to(x_ref, hb_ref, gs1, sh1, ts // 2)

    b_in = bin_ref[...]
    ln_g = lng_ref[...]
    ln_b = lnb_ref[...]
    bst = bst_ref[...]
    ti = lax.broadcasted_iota(jnp.int32, (GM_CHUNK, GM_CHUNK), 0)
    si = lax.broadcasted_iota(jnp.int32, (GM_CHUNK, GM_CHUNK), 1)
    w_tri = [jnp.where(si <= ti, ws_ref[g], 0.0).astype(BF16) for g in range(GM_GROUPS)]

    def gelu(t):
        return t * (0.5 * (1.0 + jnp.tanh(_SQRT_2_OVER_PI * (t + 0.044715 * (t * t * t)))))

    def project(rows, cs):
        z_ref[rows, cs] = _dot(hb_ref[rows, :], win_ref[:, cs])

    def emit_out(rows, cs):
        o_ref[rows, cs] = x_ref[rows, cs] + g1[:, cs] * _dot(y_ref[rows, :], wout_ref[:, cs])

    def gate_chunk(ci, mxu_work):
        rows = slice(ci * GM_CHUNK, (ci + 1) * GM_CHUNK)
        u = gelu(z_ref[rows, 0:GM_WIDTH] + b_in[:, 0:GM_WIDTH])
        mxu_work()
        v = gelu(z_ref[rows, GM_WIDTH:2 * GM_WIDTH] + b_in[:, GM_WIDTH:])
        mxu_work()
        mu = jnp.mean(v, axis=-1, keepdims=True)
        vc = v - mu
        inv = lax.rsqrt(jnp.mean(vc * vc, axis=-1, keepdims=True) + EPS)
        vn = ((vc * inv) * ln_g + ln_b).astype(BF16)
        mxu_work()
        for g in range(GM_GROUPS):
            gs = slice(g * GM_GW, (g + 1) * GM_GW)
            mixed = _dot(w_tri[g], vn[:, gs]) + bst[:, g:g + 1]
            y_ref[rows, gs] = (u[:, gs] * mixed).astype(BF16)
        mxu_work()

    half = ts // 2
    rows_a, rows_b = slice(0, half), slice(half, ts)
    chunks_per_half = half // GM_CHUNK
    in_pieces = [slice(c, c + PROJ_PIECE) for c in range(0, 2 * GM_WIDTH, PROJ_PIECE)]
    out_pieces = [slice(c, c + PROJ_PIECE) for c in range(0, x_ref.shape[1], PROJ_PIECE)]

    def drain(queue, per_call):
        def mxu_work():
            for _ in range(per_call):
                if queue:
                    queue.pop(0)()
        return mxu_work

    norm_b = [functools.partial(_norm_mod_block, x_ref, hb_ref, gs1, sh1, r)
              for r in range(half, ts, ROW_BLOCK)]
    for i, cs in enumerate(in_pieces):
        project(rows_a, cs)
        for blk in norm_b[i * len(norm_b) // len(in_pieces):
                          (i + 1) * len(norm_b) // len(in_pieces)]:
            blk()
    queue = [functools.partial(project, rows_b, cs) for cs in in_pieces]
    per_call = -(-len(queue) // (4 * chunks_per_half))
    for ci in range(chunks_per_half):
        gate_chunk(ci, drain(queue, per_call))
    assert not queue
    queue = [functools.partial(emit_out, rows_a, cs) for cs in out_pieces]
    per_call = -(-len(queue) // (4 * chunks_per_half))
    for ci in range(chunks_per_half, 2 * chunks_per_half):
        gate_chunk(ci, drain(queue, per_call))
    assert not queue
    for cs in out_pieces:
        emit_out(rows_b, cs)


def _mix1_call(x, mod, ng, win, b_in, ln_g, ln_b, ws, bst, wout, *, ts):
    bsz, seq, d = x.shape
    const2 = lambda b, s: (0, 0)
    return pl.pallas_call(
        functools.partial(_mix1_kernel, ts=ts),
        grid=(bsz, seq // ts),
        in_specs=[
            pl.BlockSpec((None, ts, d), lambda b, s: (b, s, 0)),
            pl.BlockSpec((None, 6, d), lambda b, s: (b, 0, 0)),
            pl.BlockSpec((1, d), const2),
            pl.BlockSpec((d, 2 * GM_WIDTH), const2),
            pl.BlockSpec((1, 2 * GM_WIDTH), const2),
            pl.BlockSpec((1, GM_WIDTH), const2),
            pl.BlockSpec((1, GM_WIDTH), const2),
            pl.BlockSpec((GM_GROUPS, GM_CHUNK, GM_CHUNK), lambda b, s: (0, 0, 0)),
            pl.BlockSpec((GM_CHUNK, GM_GROUPS), const2),
            pl.BlockSpec((GM_WIDTH, d), const2),
        ],
        out_specs=pl.BlockSpec((None, ts, d), lambda b, s: (b, s, 0)),
        out_shape=jax.ShapeDtypeStruct((bsz, seq, d), F32),
        scratch_shapes=[
            pltpu.VMEM((ts, d), BF16),
            pltpu.VMEM((ts, 2 * GM_WIDTH), F32),
            pltpu.VMEM((ts, GM_WIDTH), BF16),
        ],
        compiler_params=pltpu.CompilerParams(
            dimension_semantics=("parallel", "parallel"),
            vmem_limit_bytes=VMEM_LIMIT_BYTES),
        name="mixer_spatial_gating",
    )(x, mod, ng, win, b_in, ln_g, ln_b, ws, bst, wout)


def _ffn_kernel(x_ref, mod_ref, ng_ref, w1_ref, w2_ref, fg_ref, o_ref, hb_ref, *, ts, final):
    sh2 = mod_ref[3:4, :]
    sc2 = mod_ref[4:5, :]
    g2 = mod_ref[5:6, :]
    _norm_mod_to(x_ref, hb_ref, ng_ref[...] * (1.0 + sc2), sh2, ts)

    acc = None
    fcols = D_MODEL
    for j in range(D_FF // fcols):
        hid = jnp.maximum(_dot(hb_ref[...], w1_ref[:, j * fcols:(j + 1) * fcols]), 0.0)
        part = _dot((hid * hid).astype(BF16), w2_ref[j * fcols:(j + 1) * fcols, :])
        acc = part if acc is None else acc + part

    out = x_ref[...] + g2 * acc
    if final:
        inv = lax.rsqrt(jnp.mean(out * out, axis=-1, keepdims=True) + EPS)
        out = (out * inv) * fg_ref[...]
    o_ref[...] = out


def _ffn_call(x, mod, ng, w1, w2, fg, *, ts, final):
    bsz, seq, d = x.shape
    const2 = lambda b, s: (0, 0)
    return pl.pallas_call(
        functools.partial(_ffn_kernel, ts=ts, final=final),
        grid=(bsz, seq // ts),
        in_specs=[
            pl.BlockSpec((None, ts, d), lambda b, s: (b, s, 0)),
            pl.BlockSpec((None, 6, d), lambda b, s: (b, 0, 0)),
            pl.BlockSpec((1, d), const2),
            pl.BlockSpec((d, D_FF), const2, pipeline_mode=pl.Buffered(1)),
            pl.BlockSpec((D_FF, d), const2, pipeline_mode=pl.Buffered(1)),
            pl.BlockSpec((1, d), const2),
        ],
        out_specs=pl.BlockSpec((None, ts, d), lambda b, s: (b, s, 0)),
        out_shape=jax.ShapeDtypeStruct((bsz, seq, d), F32),
        scratch_shapes=[pltpu.VMEM((ts, d), BF16)],
        compiler_params=pltpu.CompilerParams(
            dimension_semantics=("parallel", "parallel"),
            vmem_limit_bytes=VMEM_LIMIT_BYTES),
        name="ffn_final" if final else "ffn",
    )(x, mod, ng, w1, w2, fg)


def kernel(x, c, ada_w, ada_b, norm_mix_g, norm_ffn_g, w_in0, conv_w, conv_b, hg_lb, hg_gain,
           w_out0, w_in1, b_in1, gm_ln_g, gm_ln_b, gm_ws, gm_bs, w_out1, w_ff1, w_ff2, final_g):
    bsz, seq, d = x.shape
    ts = min(SEQ_TILE, seq)
    mod = _ada_call(c, ada_w, ada_b).reshape(ada_w.shape[0], bsz, 6, d)
    fg = final_g.reshape(1, d)

    x = _mix0_call(x, mod[0], norm_mix_g[0:1], w_in0[0].astype(BF16), conv_w[0], conv_b[0:1],
                   hg_lb, hg_gain[0:1], w_out0[0].astype(BF16), ts=ts)
    x = _ffn_call(x, mod[0], norm_ffn_g[0:1], w_ff1[0].astype(BF16), w_ff2[0].astype(BF16), fg,
                  ts=ts, final=False)
    x = _mix1_call(x, mod[1], norm_mix_g[1:2], w_in1[0].astype(BF16), b_in1[0:1], gm_ln_g[0:1],
                   gm_ln_b[0:1], gm_ws[0], gm_bs[0].T, w_out1[0].astype(BF16), ts=ts)
    x = _ffn_call(x, mod[1], norm_ffn_g[1:2], w_ff1[1].astype(BF16), w_ff2[1].astype(BF16), fg,
                  ts=ts, final=True)
    return x
```

```python
import functools

import numpy as np
import jax
import jax.numpy as jnp
from jax import lax
from jax.experimental import pallas as pl
from jax.experimental.pallas import tpu as pltpu

F32 = jnp.float32
BF16 = jnp.bfloat16

EPS = 1e-6
D_MODEL = 1024
D_FF = 4 * D_MODEL
A_WIDTH = 512
B_WIDTH = 512
HG_HEADS = 4
HG_DK = 128
HG_DV = 128
IN0_COLS = 3 * A_WIDTH + 4 * B_WIDTH
GM_WIDTH = D_MODEL
GM_GROUPS = 4
GM_CHUNK = 128
GM_GW = GM_WIDTH // GM_GROUPS

SUBLANES = 8
LANES = 128
VMEM_LIMIT_BYTES = 56 * 1024 * 1024

SEQ_TILE = 512
ROW_BLOCK = 64
HG_CHUNK = 64
HG_LEVELS = (32, 16, 8, 4, 2, 1)
HG_BLOCK = 16
HG_EXP_GUARD = 80.0
PROJ_PIECE = 256

_SQRT_2_OVER_PI = float(np.sqrt(2.0 / np.pi))


def _sigmoid(v):
    return 1.0 / (1.0 + jnp.exp(-v))


def _dot(a, b):
    return jnp.dot(a, b, preferred_element_type=F32)


def _dot_nt(a, b):
    return lax.dot_general(a, b, (((1,), (1,)), ((), ())), preferred_element_type=F32)


def _dot_tn(a, b):
    return lax.dot_general(a, b, (((0,), (0,)), ((), ())), preferred_element_type=F32)


def _norm_mod_block(x_ref, hb_ref, gs, sh, r):
    xt = x_ref[pl.ds(r, ROW_BLOCK), :]
    inv = lax.rsqrt(jnp.mean(xt * xt, axis=-1, keepdims=True) + EPS)
    hb_ref[pl.ds(r, ROW_BLOCK), :] = ((xt * inv) * gs + sh).astype(BF16)


def _norm_mod_to(x_ref, hb_ref, gs, sh, rows):
    def body(i, carry):
        _norm_mod_block(x_ref, hb_ref, gs, sh, pl.multiple_of(i * ROW_BLOCK, ROW_BLOCK))
        return carry

    lax.fori_loop(0, rows // ROW_BLOCK, body, 0)


def _ada_kernel(c_ref, w_ref, b_ref, o_ref):
    c = c_ref[...]
    s = c * _sigmoid(c)
    o_ref[...] = jnp.dot(s, w_ref[...], preferred_element_type=F32,
                         precision=lax.Precision.HIGHEST) + b_ref[...]


def _ada_call(c, ada_w, ada_b):
    depth, d, e = ada_w.shape
    bsz = c.shape[0]
    tn = 1536
    return pl.pallas_call(
        _ada_kernel,
        grid=(depth, e // tn),
        in_specs=[
            pl.BlockSpec((bsz, d), lambda l, j: (0, 0)),
            pl.BlockSpec((None, d, tn), lambda l, j: (l, 0, j)),
            pl.BlockSpec((None, 1, tn), lambda l, j: (l, 0, j)),
        ],
        out_specs=pl.BlockSpec((None, bsz, tn), lambda l, j: (l, 0, j)),
        out_shape=jax.ShapeDtypeStruct((depth, bsz, e), F32),
        compiler_params=pltpu.CompilerParams(
            dimension_semantics=("arbitrary", "arbitrary"),
            vmem_limit_bytes=VMEM_LIMIT_BYTES),
        name="ada_modulation",
    )(c, ada_w, ada_b.reshape(depth, 1, e))


def _level_masks():
    ti = lax.broadcasted_iota(jnp.int32, (HG_CHUNK, HG_CHUNK), 0)
    si = lax.broadcasted_iota(jnp.int32, (HG_CHUNK, HG_CHUNK), 1)
    masks = []
    for m in HG_LEVELS:
        other_blk = (ti ^ si) & ~(2 * m - 1)
        masks.append((other_blk | ((ti & m) ^ m) | (si & m)) == 0)
    blk_mask = (((ti ^ si) & ~(HG_BLOCK - 1)) | jnp.maximum(si - ti, 0)) == 0
    return masks, ti == si, blk_mask


def _block_start(c):
    rows, w = c.shape
    parts = [jnp.zeros((HG_BLOCK, w), c.dtype)]
    for b in range(1, rows // HG_BLOCK):
        parts.append(jnp.broadcast_to(c[b * HG_BLOCK - 1:b * HG_BLOCK, :], (HG_BLOCK, w)))
    return jnp.concatenate(parts, axis=0)


def _block_boundary(c, m):
    rows, w = c.shape
    if m >= SUBLANES:
        blk = 2 * m
        return jnp.concatenate(
            [jnp.broadcast_to(c[b * blk + m - 1:b * blk + m, :], (blk, w))
             for b in range(rows // blk)], axis=0)
    c3 = c.reshape(rows // SUBLANES, SUBLANES, w)
    if m == 4:
        bnd = jnp.broadcast_to(c3[:, 3:4, :], c3.shape)
    else:
        assert m == 2
        sub = lax.broadcasted_iota(jnp.int32, c3.shape, 1)
        bnd = jnp.where(sub < 4, jnp.broadcast_to(c3[:, 1:2, :], c3.shape),
                        jnp.broadcast_to(c3[:, 5:6, :], c3.shape))
    return bnd.reshape(rows, w)


def _chunk_cumsum(x):
    rows, w = x.shape
    nv = rows // SUBLANES
    x3 = x.reshape(nv, SUBLANES, w)
    sub = lax.broadcasted_iota(jnp.int32, x3.shape, 1)
    sh = 1
    while sh < SUBLANES:
        x3 = x3 + jnp.where(sub >= sh, pltpu.roll(x3, sh, axis=1), 0.0)
        sh *= 2
    tot = jnp.broadcast_to(x3[:, SUBLANES - 1:SUBLANES, :], x3.shape)
    outs = [x3[0]]
    acc = tot[0]
    for j in range(1, nv):
        outs.append(x3[j] + acc)
        if j + 1 < nv:
            acc = acc + tot[j]
    return jnp.concatenate(outs, axis=0)


def _spread(items, slots):
    state = {"calls": 0, "done": 0}

    def run_due():
        state["calls"] += 1
        want = min(len(items), -(-len(items) * state["calls"] // slots))
        while state["done"] < want:
            items[state["done"]]()
            state["done"] += 1
    return run_due


def _mix0_kernel(x_ref, mod_ref, ng_ref, win_ref, cw_ref, cb_ref, hglb_ref, gain_ref, wout_ref,
                 o_ref, hb_ref, p_ref, y_ref, st_ref, zc_ref, cum_ref, ck_ref, f_ref, worst_ref,
                 *, ts):
    @pl.when(pl.program_id(1) == 0)
    def _():
        st_ref[...] = jnp.zeros_like(st_ref)
        zc_ref[...] = jnp.zeros_like(zc_ref)

    half = ts // 2
    rows_a, rows_b = slice(0, half), slice(half, ts)
    chunks_per_half = half // HG_CHUNK
    sh1 = mod_ref[0:1, :]
    gs1 = ng_ref[...] * (1.0 + mod_ref[1:2, :])
    g1 = mod_ref[2:3, :]
    _norm_mod_to(x_ref, hb_ref, gs1, sh1, half)

    a0, a1, a2 = hglb_ref[0:1, :], hglb_ref[1:2, :], hglb_ref[2:3, :]
    amax = jnp.maximum(jnp.maximum(a0, a1), a2)
    e0, e1, e2 = jnp.exp(a0 - amax), jnp.exp(a1 - amax), jnp.exp(a2 - amax)
    lb = e0 / (e0 + e1 + e2)

    cw0, cw1, cw2 = cw_ref[0:1, :], cw_ref[1:2, :], cw_ref[2:3, :]
    cb = cb_ref[...]
    gain = gain_ref[...]
    row8 = lax.broadcasted_iota(jnp.int32, (SUBLANES, A_WIDTH), 0)
    masks, diag, blk_mask = _level_masks()

    in_pieces = [slice(c, c + PROJ_PIECE) for c in range(0, IN0_COLS, PROJ_PIECE)]
    out_pieces = [slice(c, c + PROJ_PIECE) for c in range(0, D_MODEL, PROJ_PIECE)]

    def project(rows, cs):
        p_ref[rows, cs] = _dot(hb_ref[rows, :], win_ref[:, cs])

    def emit_out(rows, cs):
        o_ref[rows, cs] = x_ref[rows, cs] + g1[:, cs] * _dot(y_ref[rows, :], wout_ref[:, cs])

    def gates_chunk(ci, other_work):
        rows = slice(ci * HG_CHUNK, (ci + 1) * HG_CHUNK)

        def col(j):
            return p_ref[rows, j * 512:(j + 1) * 512]

        z = col(1) * col(2)
        prev = zc_ref[...]
        zc_ref[...] = z[HG_CHUNK - SUBLANES:, :]

        def shifted(k):
            zr = pltpu.roll(z, k, axis=0)
            pr = pltpu.roll(prev, k, axis=0)
            head = jnp.where(row8 < k, pr, zr[:SUBLANES])
            return jnp.concatenate([head, zr[SUBLANES:]], axis=0)

        conv = cb + shifted(2) * cw0
        conv = conv + shifted(1) * cw1
        conv = conv + z * cw2
        y_ref[rows, 0:A_WIDTH] = (col(0) * conv).astype(BF16)
        other_work()

        f = lb + (1.0 - lb) * _sigmoid(col(4))
        cum = _chunk_cumsum(jnp.log(f))
        ck = cum - jnp.log(1.0 - f)
        f_ref[rows, :] = f
        cum_ref[rows, :] = cum
        ck_ref[rows, :] = ck
        e = _block_start(cum) - ck
        worst_ref[...] = e if ci % chunks_per_half == 0 else jnp.maximum(worst_ref[...], e)
        other_work()


    def scores_stage(ci, fast):
        rows = slice(ci * HG_CHUNK, (ci + 1) * HG_CHUNK)
        q = p_ref[rows, 3 * 512:4 * 512]
        v = p_ref[rows, 5 * 512:6 * 512]
        cum = cum_ref[rows, :]
        ck = ck_ref[rows, :]
        last = cum[HG_CHUNK - 1:HG_CHUNK, :]
        q_in = q * jnp.exp(cum)
        k_dec = jnp.exp(last - ck)
        if not fast:
            f = f_ref[rows, :]
            kk = 1.0 - f
            qf = q * f

        heads = []
        for h in range(HG_HEADS):
            hs = slice(h * HG_DK, (h + 1) * HG_DK)
            q_h, cum_h, ck_h = q[:, hs], cum[:, hs], ck[:, hs]
            if fast:
                bnd = _block_start(cum_h)
                qt = (q_h * jnp.exp(cum_h - bnd)).astype(BF16)
                kt = jnp.exp(bnd - ck_h).astype(BF16)
                parts = [(blk_mask, _dot_nt(qt, kt))]
                levels = [(m, mask) for m, mask in zip(HG_LEVELS, masks) if m >= HG_BLOCK]
            else:
                k_b = kk[:, hs].astype(BF16)
                parts = [(diag, _dot_nt(q_h.astype(BF16), k_b))]
                levels = list(zip(HG_LEVELS, masks))
            for m, mask in levels:
                if m == 1:
                    qt, kt = qf[:, hs].astype(BF16), k_b
                else:
                    bnd = _block_boundary(cum_h, m)
                    qt = (q_h * jnp.exp(cum_h - bnd)).astype(BF16)
                    kt = jnp.exp(bnd - ck_h).astype(BF16)
                parts.append((mask, _dot_nt(qt, kt)))
            heads.append(dict(parts=parts, q_in=q_in[:, hs].astype(BF16),
                              k_dec=k_dec[:, hs].astype(BF16), v=v[:, hs].astype(BF16)))
        return dict(rows=rows, heads=heads, dec=jnp.exp(last))

    def state_stage(ctx):
        for h, hd in enumerate(ctx["heads"]):
            hs = slice(h * HG_DK, (h + 1) * HG_DK)
            scores = None
            for mask, s in hd["parts"]:
                scores = jnp.where(mask, s, 0.0 if scores is None else scores)
            st = st_ref[h]
            hd["o"] = _dot_nt(hd["q_in"], st.astype(BF16)) + _dot(scores.astype(BF16), hd["v"])
            st_ref[h] = st * ctx["dec"][:, hs] + _dot_tn(hd["v"], hd["k_dec"])

    def output_stage(ctx):
        rows = ctx["rows"]
        gate = p_ref[rows, 6 * 512:7 * 512]
        gsil = gate * _sigmoid(gate)
        for h, hd in enumerate(ctx["heads"]):
            hs = slice(h * HG_DK, (h + 1) * HG_DK)
            o = hd["o"]
            o = o * lax.rsqrt(jnp.mean(o * o, axis=-1, keepdims=True) + EPS)
            y_ref[rows, A_WIDTH + h * HG_DV:A_WIDTH + (h + 1) * HG_DV] = (
                o * gain[:, hs] * gsil[:, hs]).astype(BF16)

    chunks_a = range(chunks_per_half)
    chunks_b = range(chunks_per_half, 2 * chunks_per_half)
    no_work = lambda: None

    norm_b = _spread([functools.partial(_norm_mod_block, x_ref, hb_ref, gs1, sh1, r)
                      for r in range(half, ts, ROW_BLOCK)], len(in_pieces))
    for cs in in_pieces:
        project(rows_a, cs)
        norm_b()

    proj_b = _spread([functools.partial(project, rows_b, cs) for cs in in_pieces],
                     2 * chunks_per_half)
    for ci in chunks_a:
        gates_chunk(ci, proj_b)

    def recur_half(chunks, other_items, fast):
        chunks = list(chunks)
        n = len(chunks)
        other_work = _spread(other_items, 3 * (n + 2))
        ctx = {}
        for t in range(n + 2):
            if t < n:
                ctx[t] = scores_stage(chunks[t], fast)
            other_work()
            if 0 <= t - 1 < n:
                state_stage(ctx[t - 1])
            other_work()
            if 0 <= t - 2 < n:
                output_stage(ctx.pop(t - 2))
            other_work()

    def both_paths(chunks, other_items):
        fast_ok = jnp.max(worst_ref[...]) < HG_EXP_GUARD
        pl.when(fast_ok)(functools.partial(recur_half, chunks, other_items, True))
        pl.when(jnp.logical_not(fast_ok))(
            functools.partial(recur_half, chunks, other_items, False))

    both_paths(chunks_a, [functools.partial(gates_chunk, ci, no_work) for ci in chunks_b])
    both_paths(chunks_b, [functools.partial(emit_out, rows_a, cs) for cs in out_pieces])
    for cs in out_pieces:
        emit_out(rows_b, cs)


def _mix0_call(x, mod, ng, win, cw, cb, hglb, gain, wout, *, ts):
    bsz, seq, d = x.shape
    assert ts % (2 * HG_CHUNK) == 0
    const2 = lambda b, s: (0, 0)
    return pl.pallas_call(
        functools.partial(_mix0_kernel, ts=ts),
        grid=(bsz, seq // ts),
        in_specs=[
            pl.BlockSpec((None, ts, d), lambda b, s: (b, s, 0)),
            pl.BlockSpec((None, 6, d), lambda b, s: (b, 0, 0)),
            pl.BlockSpec((1, d), const2),
            pl.BlockSpec((d, IN0_COLS), const2, pipeline_mode=pl.Buffered(1)),
            pl.BlockSpec((3, A_WIDTH), const2),
            pl.BlockSpec((1, A_WIDTH), const2),
            pl.BlockSpec((3, B_WIDTH), const2),
            pl.BlockSpec((1, B_WIDTH), const2),
            pl.BlockSpec((d, d), const2, pipeline_mode=pl.Buffered(1)),
        ],
        out_specs=pl.BlockSpec((None, ts, d), lambda b, s: (b, s, 0)),
        out_shape=jax.ShapeDtypeStruct((bsz, seq, d), F32),
        scratch_shapes=[
            pltpu.VMEM((ts, d), BF16),
            pltpu.VMEM((ts, IN0_COLS), F32),
            pltpu.VMEM((ts, d), BF16),
            pltpu.VMEM((HG_HEADS, HG_DV, HG_DK), F32),
            pltpu.VMEM((SUBLANES, A_WIDTH), F32),
            pltpu.VMEM((ts, B_WIDTH), F32),
            pltpu.VMEM((ts, B_WIDTH), F32),
            pltpu.VMEM((ts, B_WIDTH), F32),
            pltpu.VMEM((HG_CHUNK, B_WIDTH), F32),
        ],
        compiler_params=pltpu.CompilerParams(
            dimension_semantics=("parallel", "arbitrary"),
            vmem_limit_bytes=VMEM_LIMIT_BYTES),
        name="mixer_conv_hgrn2",
    )(x, mod, ng, win, cw, cb, hglb, gain, wout)


def _mix1_kernel(x_ref, mod_ref, ng_ref, win_ref, bin_ref, lng_ref, lnb_ref, ws_ref,
                 bst_ref, wout_ref, o_ref, hb_ref, z_ref, y_ref, *, ts):
    sh1 = mod_ref[0:1, :]
    sc1 = mod_ref[1:2, :]
    g1 = mod_ref[2:3, :]
    gs1 = ng_ref[...] * (1.0 + sc1)
    _norm_mod_to(x_ref, hb_ref, gs1, sh1, ts // 2)

    b_in = bin_ref[...]
    ln_g = lng_ref[...]
    ln_b = lnb_ref[...]
    bst = bst_ref[...]
    ti = lax.broadcasted_iota(jnp.int32, (GM_CHUNK, GM_CHUNK), 0)
    si = lax.broadcasted_iota(jnp.int32, (GM_CHUNK, GM_CHUNK), 1)
    w_tri = [jnp.where(si <= ti, ws_ref[g], 0.0).astype(BF16) for g in range(GM_GROUPS)]

    def gelu(t):
        return t * (0.5 * (1.0 + jnp.tanh(_SQRT_2_OVER_PI * (t + 0.044715 * (t * t * t)))))

    def project(rows, cs):
        z_ref[rows, cs] = _dot(hb_ref[rows, :], win_ref[:, cs])

    def emit_out(rows, cs):
        o_ref[rows, cs] = x_ref[rows, cs] + g1[:, cs] * _dot(y_ref[rows, :], wout_ref[:, cs])

    def gate_chunk(ci, mxu_work):
        rows = slice(ci * GM_CHUNK, (ci + 1) * GM_CHUNK)
        u = gelu(z_ref[rows, 0:GM_WIDTH] + b_in[:, 0:GM_WIDTH])
        mxu_work()
        v = gelu(z_ref[rows, GM_WIDTH:2 * GM_WIDTH] + b_in[:, GM_WIDTH:])
        mxu_work()
        mu = jnp.mean(v, axis=-1, keepdims=True)
        vc = v - mu
        inv = lax.rsqrt(jnp.mean(vc * vc, axis=-1, keepdims=True) + EPS)
        vn = ((vc * inv) * ln_g + ln_b).astype(BF16)
        mxu_work()
        for g in range(GM_GROUPS):
            gs = slice(g * GM_GW, (g + 1) * GM_GW)
            mixed = _dot(w_tri[g], vn[:, gs]) + bst[:, g:g + 1]
            y_ref[rows, gs] = (u[:, gs] * mixed).astype(BF16)
        mxu_work()

    half = ts // 2
    rows_a, rows_b = slice(0, half), slice(half, ts)
    chunks_per_half = half // GM_CHUNK
    in_pieces = [slice(c, c + PROJ_PIECE) for c in range(0, 2 * GM_WIDTH, PROJ_PIECE)]
    out_pieces = [slice(c, c + PROJ_PIECE) for c in range(0, x_ref.shape[1], PROJ_PIECE)]

    def drain(queue, per_call):
        def mxu_work():
            for _ in range(per_call):
                if queue:
                    queue.pop(0)()
        return mxu_work

    norm_b = [functools.partial(_norm_mod_block, x_ref, hb_ref, gs1, sh1, r)
              for r in range(half, ts, ROW_BLOCK)]
    for i, cs in enumerate(in_pieces):
        project(rows_a, cs)
        for blk in norm_b[i * len(norm_b) // len(in_pieces):
                          (i + 1) * len(norm_b) // len(in_pieces)]:
            blk()
    queue = [functools.partial(project, rows_b, cs) for cs in in_pieces]
    per_call = -(-len(queue) // (4 * chunks_per_half))
    for ci in range(chunks_per_half):
        gate_chunk(ci, drain(queue, per_call))
    assert not queue
    queue = [functools.partial(emit_out, rows_a, cs) for cs in out_pieces]
    per_call = -(-len(queue) // (4 * chunks_per_half))
    for ci in range(chunks_per_half, 2 * chunks_per_half):
        gate_chunk(ci, drain(queue, per_call))
    assert not queue
    for cs in out_pieces:
        emit_out(rows_b, cs)


def _mix1_call(x, mod, ng, win, b_in, ln_g, ln_b, ws, bst, wout, *, ts):
    bsz, seq, d = x.shape
    const2 = lambda b, s: (0, 0)
    return pl.pallas_call(
        functools.partial(_mix1_kernel, ts=ts),
        grid=(bsz, seq // ts),
        in_specs=[
            pl.BlockSpec((None, ts, d), lambda b, s: (b, s, 0)),
            pl.BlockSpec((None, 6, d), lambda b, s: (b, 0, 0)),
            pl.BlockSpec((1, d), const2),
            pl.BlockSpec((d, 2 * GM_WIDTH), const2),
            pl.BlockSpec((1, 2 * GM_WIDTH), const2),
            pl.BlockSpec((1, GM_WIDTH), const2),
            pl.BlockSpec((1, GM_WIDTH), const2),
            pl.BlockSpec((GM_GROUPS, GM_CHUNK, GM_CHUNK), lambda b, s: (0, 0, 0)),
            pl.BlockSpec((GM_CHUNK, GM_GROUPS), const2),
            pl.BlockSpec((GM_WIDTH, d), const2),
        ],
        out_specs=pl.BlockSpec((None, ts, d), lambda b, s: (b, s, 0)),
        out_shape=jax.ShapeDtypeStruct((bsz, seq, d), F32),
        scratch_shapes=[
            pltpu.VMEM((ts, d), BF16),
            pltpu.VMEM((ts, 2 * GM_WIDTH), F32),
            pltpu.VMEM((ts, GM_WIDTH), BF16),
        ],
        compiler_params=pltpu.CompilerParams(
            dimension_semantics=("parallel", "parallel"),
            vmem_limit_bytes=VMEM_LIMIT_BYTES),
        name="mixer_spatial_gating",
    )(x, mod, ng, win, b_in, ln_g, ln_b, ws, bst, wout)


def _ffn_kernel(x_ref, mod_ref, ng_ref, w1_ref, w2_ref, fg_ref, o_ref, hb_ref, *, ts, final):
    sh2 = mod_ref[3:4, :]
    sc2 = mod_ref[4:5, :]
    g2 = mod_ref[5:6, :]
    gs2 = ng_ref[...] * (1.0 + sc2)

    half = ts // 2
    _norm_mod_to(x_ref, hb_ref, gs2, sh2, half)
    fcols = D_MODEL
    n_f = D_FF // fcols
    norm_b = _spread([functools.partial(_norm_mod_block, x_ref, hb_ref, gs2, sh2, r)
                      for r in range(half, ts, ROW_BLOCK)], n_f)
    for rows, between in ((slice(0, half), norm_b), (slice(half, ts), lambda: None)):
        acc = None
        for j in range(n_f):
            hid = jnp.maximum(_dot(hb_ref[rows, :], w1_ref[:, j * fcols:(j + 1) * fcols]), 0.0)
            between()
            part = _dot((hid * hid).astype(BF16), w2_ref[j * fcols:(j + 1) * fcols, :])
            acc = part if acc is None else acc + part
        out = x_ref[rows, :] + g2 * acc
        if final:
            inv = lax.rsqrt(jnp.mean(out * out, axis=-1, keepdims=True) + EPS)
            out = (out * inv) * fg_ref[...]
        o_ref[rows, :] = out


def _ffn_call(x, mod, ng, w1, w2, fg, *, ts, final):
    bsz, seq, d = x.shape
    const2 = lambda b, s: (0, 0)
    return pl.pallas_call(
        functools.partial(_ffn_kernel, ts=ts, final=final),
        grid=(bsz, seq // ts),
        in_specs=[
            pl.BlockSpec((None, ts, d), lambda b, s: (b, s, 0)),
            pl.BlockSpec((None, 6, d), lambda b, s: (b, 0, 0)),
            pl.BlockSpec((1, d), const2),
            pl.BlockSpec((d, D_FF), const2, pipeline_mode=pl.Buffered(1)),
            pl.BlockSpec((D_FF, d), const2, pipeline_mode=pl.Buffered(1)),
            pl.BlockSpec((1, d), const2),
        ],
        out_specs=pl.BlockSpec((None, ts, d), lambda b, s: (b, s, 0)),
        out_shape=jax.ShapeDtypeStruct((bsz, seq, d), F32),
        scratch_shapes=[pltpu.VMEM((ts, d), BF16)],
        compiler_params=pltpu.CompilerParams(
            dimension_semantics=("parallel", "parallel"),
            vmem_limit_bytes=VMEM_LIMIT_BYTES),
        name="ffn_final" if final else "ffn",
    )(x, mod, ng, w1, w2, fg)


def kernel(x, c, ada_w, ada_b, norm_mix_g, norm_ffn_g, w_in0, conv_w, conv_b, hg_lb, hg_gain,
           w_out0, w_in1, b_in1, gm_ln_g, gm_ln_b, gm_ws, gm_bs, w_out1, w_ff1, w_ff2, final_g):
    bsz, seq, d = x.shape
    ts = min(SEQ_TILE, seq)
    mod = _ada_call(c, ada_w, ada_b).reshape(ada_w.shape[0], bsz, 6, d)
    fg = final_g.reshape(1, d)

    x = _mix0_call(x, mod[0], norm_mix_g[0:1], w_in0[0].astype(BF16), conv_w[0], conv_b[0:1],
                   hg_lb, hg_gain[0:1], w_out0[0].astype(BF16), ts=ts)
    x = _ffn_call(x, mod[0], norm_ffn_g[0:1], w_ff1[0].astype(BF16), w_ff2[0].astype(BF16), fg,
                  ts=ts, final=False)
    x = _mix1_call(x, mod[1], norm_mix_g[1:2], w_in1[0].astype(BF16), b_in1[0:1], gm_ln_g[0:1],
                   gm_ln_b[0:1], gm_ws[0], gm_bs[0].T, w_out1[0].astype(BF16), ts=ts)
    x = _ffn_call(x, mod[1], norm_ffn_g[1:2], w_ff1[1].astype(BF16), w_ff2[1].astype(BF16), fg,
                  ts=ts, final=True)
    return x
```

```python
import functools

import numpy as np
import jax
import jax.numpy as jnp
from jax import lax
from jax.experimental import pallas as pl
from jax.experimental.pallas import tpu as pltpu

F32 = jnp.float32
BF16 = jnp.bfloat16

EPS = 1e-6
D_MODEL = 1024
D_FF = 4 * D_MODEL
A_WIDTH = 512
B_WIDTH = 512
HG_HEADS = 4
HG_DK = 128
HG_DV = 128
IN0_COLS = 3 * A_WIDTH + 4 * B_WIDTH
GM_WIDTH = D_MODEL
GM_GROUPS = 4
GM_CHUNK = 128
GM_GW = GM_WIDTH // GM_GROUPS

SUBLANES = 8
LANES = 128
VMEM_LIMIT_BYTES = 56 * 1024 * 1024

SEQ_TILE = 512
GM_TILE = 1024
GM_SUB = 256
FFN_TILE = 1024
FFN_SUB = 256
ROW_BLOCK = 64
HG_CHUNK = 64
HG_LEVELS = (32, 16, 8, 4, 2, 1)
HG_BLOCK = 16
HG_EXP_GUARD = 80.0
PROJ_PIECE = 256

_SQRT_2_OVER_PI = float(np.sqrt(2.0 / np.pi))


def _sigmoid(v):
    return 1.0 / (1.0 + jnp.exp(-v))


def _dot(a, b):
    return jnp.dot(a, b, preferred_element_type=F32)


def _dot_nt(a, b):
    return lax.dot_general(a, b, (((1,), (1,)), ((), ())), preferred_element_type=F32)


def _dot_tn(a, b):
    return lax.dot_general(a, b, (((0,), (0,)), ((), ())), preferred_element_type=F32)


def _norm_mod_block(x_ref, hb_ref, gs, sh, r):
    xt = x_ref[pl.ds(r, ROW_BLOCK), :]
    inv = lax.rsqrt(jnp.mean(xt * xt, axis=-1, keepdims=True) + EPS)
    hb_ref[pl.ds(r, ROW_BLOCK), :] = ((xt * inv) * gs + sh).astype(BF16)


def _norm_mod_to(x_ref, hb_ref, gs, sh, rows):
    def body(i, carry):
        _norm_mod_block(x_ref, hb_ref, gs, sh, pl.multiple_of(i * ROW_BLOCK, ROW_BLOCK))
        return carry

    lax.fori_loop(0, rows // ROW_BLOCK, body, 0)


def _ada_kernel(c_ref, w_ref, b_ref, o_ref):
    c = c_ref[...]
    s = c * _sigmoid(c)
    o_ref[...] = jnp.dot(s, w_ref[...], preferred_element_type=F32,
                         precision=lax.Precision.HIGHEST) + b_ref[...]


def _ada_call(c, ada_w, ada_b):
    depth, d, e = ada_w.shape
    bsz = c.shape[0]
    tn = 1536
    return pl.pallas_call(
        _ada_kernel,
        grid=(depth, e // tn),
        in_specs=[
            pl.BlockSpec((bsz, d), lambda l, j: (0, 0)),
            pl.BlockSpec((None, d, tn), lambda l, j: (l, 0, j)),
            pl.BlockSpec((None, 1, tn), lambda l, j: (l, 0, j)),
        ],
        out_specs=pl.BlockSpec((None, bsz, tn), lambda l, j: (l, 0, j)),
        out_shape=jax.ShapeDtypeStruct((depth, bsz, e), F32),
        compiler_params=pltpu.CompilerParams(
            dimension_semantics=("arbitrary", "arbitrary"),
            vmem_limit_bytes=VMEM_LIMIT_BYTES),
        name="ada_modulation",
    )(c, ada_w, ada_b.reshape(depth, 1, e))


def _level_masks():
    ti = lax.broadcasted_iota(jnp.int32, (HG_CHUNK, HG_CHUNK), 0)
    si = lax.broadcasted_iota(jnp.int32, (HG_CHUNK, HG_CHUNK), 1)
    masks = []
    for m in HG_LEVELS:
        other_blk = (ti ^ si) & ~(2 * m - 1)
        masks.append((other_blk | ((ti & m) ^ m) | (si & m)) == 0)
    blk_mask = (((ti ^ si) & ~(HG_BLOCK - 1)) | jnp.maximum(si - ti, 0)) == 0
    return masks, ti == si, blk_mask


def _block_start(c):
    rows, w = c.shape
    parts = [jnp.zeros((HG_BLOCK, w), c.dtype)]
    for b in range(1, rows // HG_BLOCK):
        parts.append(jnp.broadcast_to(c[b * HG_BLOCK - 1:b * HG_BLOCK, :], (HG_BLOCK, w)))
    return jnp.concatenate(parts, axis=0)


def _block_boundary(c, m):
    rows, w = c.shape
    if m >= SUBLANES:
        blk = 2 * m
        return jnp.concatenate(
            [jnp.broadcast_to(c[b * blk + m - 1:b * blk + m, :], (blk, w))
             for b in range(rows // blk)], axis=0)
    c3 = c.reshape(rows // SUBLANES, SUBLANES, w)
    if m == 4:
        bnd = jnp.broadcast_to(c3[:, 3:4, :], c3.shape)
    else:
        assert m == 2
        sub = lax.broadcasted_iota(jnp.int32, c3.shape, 1)
        bnd = jnp.where(sub < 4, jnp.broadcast_to(c3[:, 1:2, :], c3.shape),
                        jnp.broadcast_to(c3[:, 5:6, :], c3.shape))
    return bnd.reshape(rows, w)


def _chunk_cumsum(x):
    rows, w = x.shape
    nv = rows // SUBLANES
    x3 = x.reshape(nv, SUBLANES, w)
    sub = lax.broadcasted_iota(jnp.int32, x3.shape, 1)
    sh = 1
    while sh < SUBLANES:
        x3 = x3 + jnp.where(sub >= sh, pltpu.roll(x3, sh, axis=1), 0.0)
        sh *= 2
    tot = jnp.broadcast_to(x3[:, SUBLANES - 1:SUBLANES, :], x3.shape)
    outs = [x3[0]]
    acc = tot[0]
    for j in range(1, nv):
        outs.append(x3[j] + acc)
        if j + 1 < nv:
            acc = acc + tot[j]
    return jnp.concatenate(outs, axis=0)


def _spread(items, slots):
    state = {"calls": 0, "done": 0}

    def run_due():
        state["calls"] += 1
        want = min(len(items), -(-len(items) * state["calls"] // slots))
        while state["done"] < want:
            items[state["done"]]()
            state["done"] += 1
    return run_due


def _interleave(lists):
    keyed = [((i + 0.5) / len(l), n, item) for n, l in enumerate(lists) for i, item in enumerate(l)]
    return [item for _, _, item in sorted(keyed, key=lambda k: k[:2])]


def _mix0_kernel(x_ref, mod_ref, ng_ref, win_ref, cw_ref, cb_ref, hglb_ref, gain_ref, wout_ref,
                 o_ref, hb_ref, p_ref, y_ref, st_ref, zc_ref, cum_ref, ck_ref, f_ref, worst_ref,
                 *, ts):
    @pl.when(pl.program_id(1) == 0)
    def _():
        st_ref[...] = jnp.zeros_like(st_ref)
        zc_ref[...] = jnp.zeros_like(zc_ref)

    half = ts // 2
    rows_a, rows_b = slice(0, half), slice(half, ts)
    chunks_per_half = half // HG_CHUNK
    sh1 = mod_ref[0:1, :]
    gs1 = ng_ref[...] * (1.0 + mod_ref[1:2, :])
    g1 = mod_ref[2:3, :]
    _norm_mod_to(x_ref, hb_ref, gs1, sh1, half)

    a0, a1, a2 = hglb_ref[0:1, :], hglb_ref[1:2, :], hglb_ref[2:3, :]
    amax = jnp.maximum(jnp.maximum(a0, a1), a2)
    e0, e1, e2 = jnp.exp(a0 - amax), jnp.exp(a1 - amax), jnp.exp(a2 - amax)
    lb = e0 / (e0 + e1 + e2)

    cw0, cw1, cw2 = cw_ref[0:1, :], cw_ref[1:2, :], cw_ref[2:3, :]
    cb = cb_ref[...]
    gain = gain_ref[...]
    row8 = lax.broadcasted_iota(jnp.int32, (SUBLANES, A_WIDTH), 0)
    masks, diag, blk_mask = _level_masks()

    in_pieces = [slice(c, c + PROJ_PIECE) for c in range(0, IN0_COLS, PROJ_PIECE)]
    out_pieces = [slice(c, c + PROJ_PIECE) for c in range(0, D_MODEL, PROJ_PIECE)]

    def project(rows, cs):
        p_ref[rows, cs] = _dot(hb_ref[rows, :], win_ref[:, cs])

    def emit_out(rows, cs):
        o_ref[rows, cs] = x_ref[rows, cs] + g1[:, cs] * _dot(y_ref[rows, :], wout_ref[:, cs])

    def gates_chunk(ci, other_work):
        rows = slice(ci * HG_CHUNK, (ci + 1) * HG_CHUNK)

        def col(j):
            return p_ref[rows, j * 512:(j + 1) * 512]

        z = col(1) * col(2)
        prev = zc_ref[...]
        zc_ref[...] = z[HG_CHUNK - SUBLANES:, :]

        def shifted(k):
            zr = pltpu.roll(z, k, axis=0)
            pr = pltpu.roll(prev, k, axis=0)
            head = jnp.where(row8 < k, pr, zr[:SUBLANES])
            return jnp.concatenate([head, zr[SUBLANES:]], axis=0)

        conv = cb + shifted(2) * cw0
        conv = conv + shifted(1) * cw1
        conv = conv + z * cw2
        y_ref[rows, 0:A_WIDTH] = (col(0) * conv).astype(BF16)
        other_work()

        f = lb + (1.0 - lb) * _sigmoid(col(4))
        cum = _chunk_cumsum(jnp.log(f))
        ck = cum - jnp.log(1.0 - f)
        f_ref[rows, :] = f
        cum_ref[rows, :] = cum
        ck_ref[rows, :] = ck
        e = _block_start(cum) - ck
        worst_ref[...] = e if ci % chunks_per_half == 0 else jnp.maximum(worst_ref[...], e)
        other_work()


    def scores_stage(ci, fast):
        rows = slice(ci * HG_CHUNK, (ci + 1) * HG_CHUNK)
        q = p_ref[rows, 3 * 512:4 * 512]
        v = p_ref[rows, 5 * 512:6 * 512]
        cum = cum_ref[rows, :]
        ck = ck_ref[rows, :]
        last = cum[HG_CHUNK - 1:HG_CHUNK, :]
        q_in = q * jnp.exp(cum)
        k_dec = jnp.exp(last - ck)
        if not fast:
            f = f_ref[rows, :]
            kk = 1.0 - f
            qf = q * f

        heads = []
        for h in range(HG_HEADS):
            hs = slice(h * HG_DK, (h + 1) * HG_DK)
            q_h, cum_h, ck_h = q[:, hs], cum[:, hs], ck[:, hs]
            if fast:
                bnd = _block_start(cum_h)
                qt = (q_h * jnp.exp(cum_h - bnd)).astype(BF16)
                kt = jnp.exp(bnd - ck_h).astype(BF16)
                parts = [(blk_mask, _dot_nt(qt, kt))]
                levels = [(m, mask) for m, mask in zip(HG_LEVELS, masks) if m >= HG_BLOCK]
            else:
                k_b = kk[:, hs].astype(BF16)
                parts = [(diag, _dot_nt(q_h.astype(BF16), k_b))]
                levels = list(zip(HG_LEVELS, masks))
            for m, mask in levels:
                if m == 1:
                    qt, kt = qf[:, hs].astype(BF16), k_b
                else:
                    bnd = _block_boundary(cum_h, m)
                    qt = (q_h * jnp.exp(cum_h - bnd)).astype(BF16)
                    kt = jnp.exp(bnd - ck_h).astype(BF16)
                parts.append((mask, _dot_nt(qt, kt)))
            heads.append(dict(parts=parts, q_in=q_in[:, hs].astype(BF16),
                              k_dec=k_dec[:, hs].astype(BF16), v=v[:, hs].astype(BF16)))
        return dict(rows=rows, heads=heads, dec=jnp.exp(last))

    def state_stage(ctx):
        for h, hd in enumerate(ctx["heads"]):
            hs = slice(h * HG_DK, (h + 1) * HG_DK)
            scores = None
            for mask, s in hd["parts"]:
                scores = jnp.where(mask, s, 0.0 if scores is None else scores)
            st = st_ref[h]
            hd["o"] = _dot_nt(hd["q_in"], st.astype(BF16)) + _dot(scores.astype(BF16), hd["v"])
            st_ref[h] = st * ctx["dec"][:, hs] + _dot_tn(hd["v"], hd["k_dec"])

    def output_stage(ctx):
        rows = ctx["rows"]
        gate = p_ref[rows, 6 * 512:7 * 512]
        gsil = gate * _sigmoid(gate)
        for h, hd in enumerate(ctx["heads"]):
            hs = slice(h * HG_DK, (h + 1) * HG_DK)
            o = hd["o"]
            o = o * lax.rsqrt(jnp.mean(o * o, axis=-1, keepdims=True) + EPS)
            y_ref[rows, A_WIDTH + h * HG_DV:A_WIDTH + (h + 1) * HG_DV] = (
                o * gain[:, hs] * gsil[:, hs]).astype(BF16)

    chunks_a = range(chunks_per_half)
    chunks_b = range(chunks_per_half, 2 * chunks_per_half)
    no_work = lambda: None

    norm_b = _spread([functools.partial(_norm_mod_block, x_ref, hb_ref, gs1, sh1, r)
                      for r in range(half, ts, ROW_BLOCK)], len(in_pieces))
    for cs in in_pieces:
        project(rows_a, cs)
        norm_b()

    proj_b = _spread([functools.partial(project, rows_b, cs) for cs in in_pieces],
                     2 * chunks_per_half)
    for ci in chunks_a:
        gates_chunk(ci, proj_b)

    def recur_half(chunks, other_items, fast):
        chunks = list(chunks)
        n = len(chunks)
        other_work = _spread(other_items, 3 * (n + 2))
        ctx = {}
        for t in range(n + 2):
            if t < n:
                ctx[t] = scores_stage(chunks[t], fast)
            other_work()
            if 0 <= t - 1 < n:
                state_stage(ctx[t - 1])
            other_work()
            if 0 <= t - 2 < n:
                output_stage(ctx.pop(t - 2))
            other_work()

    def both_paths(chunks, other_items):
        fast_ok = jnp.max(worst_ref[...]) < HG_EXP_GUARD
        pl.when(fast_ok)(functools.partial(recur_half, chunks, other_items, True))
        pl.when(jnp.logical_not(fast_ok))(
            functools.partial(recur_half, chunks, other_items, False))

    both_paths(chunks_a, [functools.partial(gates_chunk, ci, no_work) for ci in chunks_b])
    both_paths(chunks_b, [functools.partial(emit_out, rows_a, cs) for cs in out_pieces])
    for cs in out_pieces:
        emit_out(rows_b, cs)


def _mix0_call(x, mod, ng, win, cw, cb, hglb, gain, wout, *, ts):
    bsz, seq, d = x.shape
    assert ts % (2 * HG_CHUNK) == 0
    const2 = lambda b, s: (0, 0)
    return pl.pallas_call(
        functools.partial(_mix0_kernel, ts=ts),
        grid=(bsz, seq // ts),
        in_specs=[
            pl.BlockSpec((None, ts, d), lambda b, s: (b, s, 0)),
            pl.BlockSpec((None, 6, d), lambda b, s: (b, 0, 0)),
            pl.BlockSpec((1, d), const2),
            pl.BlockSpec((d, IN0_COLS), const2, pipeline_mode=pl.Buffered(1)),
            pl.BlockSpec((3, A_WIDTH), const2),
            pl.BlockSpec((1, A_WIDTH), const2),
            pl.BlockSpec((3, B_WIDTH), const2),
            pl.BlockSpec((1, B_WIDTH), const2),
            pl.BlockSpec((d, d), const2, pipeline_mode=pl.Buffered(1)),
        ],
        out_specs=pl.BlockSpec((None, ts, d), lambda b, s: (b, s, 0)),
        out_shape=jax.ShapeDtypeStruct((bsz, seq, d), F32),
        scratch_shapes=[
            pltpu.VMEM((ts, d), BF16),
            pltpu.VMEM((ts, IN0_COLS), F32),
            pltpu.VMEM((ts, d), BF16),
            pltpu.VMEM((HG_HEADS, HG_DV, HG_DK), F32),
            pltpu.VMEM((SUBLANES, A_WIDTH), F32),
            pltpu.VMEM((ts, B_WIDTH), F32),
            pltpu.VMEM((ts, B_WIDTH), F32),
            pltpu.VMEM((ts, B_WIDTH), F32),
            pltpu.VMEM((HG_CHUNK, B_WIDTH), F32),
        ],
        compiler_params=pltpu.CompilerParams(
            dimension_semantics=("parallel", "arbitrary"),
            vmem_limit_bytes=VMEM_LIMIT_BYTES),
        name="mixer_conv_hgrn2",
    )(x, mod, ng, win, cw, cb, hglb, gain, wout)


def _mix1_kernel(x_ref, mod_ref, ng_ref, win_ref, bin_ref, lng_ref, lnb_ref, ws_ref,
                 bst_ref, wout_ref, o_ref, hb_ref, z_ref, y_ref, *, ts):
    sh1 = mod_ref[0:1, :]
    sc1 = mod_ref[1:2, :]
    g1 = mod_ref[2:3, :]
    gs1 = ng_ref[...] * (1.0 + sc1)
    sub = min(GM_SUB, ts)
    _norm_mod_to(x_ref, hb_ref, gs1, sh1, sub)

    b_in = bin_ref[...]
    ln_g = lng_ref[...]
    ln_b = lnb_ref[...]
    bst = bst_ref[...]
    ti = lax.broadcasted_iota(jnp.int32, (GM_CHUNK, GM_CHUNK), 0)
    si = lax.broadcasted_iota(jnp.int32, (GM_CHUNK, GM_CHUNK), 1)
    w_tri = [jnp.where(si <= ti, ws_ref[g], 0.0).astype(BF16) for g in range(GM_GROUPS)]

    def gelu(t):
        return t * (0.5 * (1.0 + jnp.tanh(_SQRT_2_OVER_PI * (t + 0.044715 * (t * t * t)))))

    def project(rows, cs):
        z_ref[rows, cs] = _dot(hb_ref[rows, :], win_ref[:, cs])

    def emit_out(rows, cs):
        o_ref[rows, cs] = x_ref[rows, cs] + g1[:, cs] * _dot(y_ref[rows, :], wout_ref[:, cs])

    def gate_chunk(ci, mxu_work):
        rows = slice(ci * GM_CHUNK, (ci + 1) * GM_CHUNK)
        u = gelu(z_ref[rows, 0:GM_WIDTH] + b_in[:, 0:GM_WIDTH])
        mxu_work()
        v = gelu(z_ref[rows, GM_WIDTH:2 * GM_WIDTH] + b_in[:, GM_WIDTH:])
        mxu_work()
        mu = jnp.mean(v, axis=-1, keepdims=True)
        vc = v - mu
        inv = lax.rsqrt(jnp.mean(vc * vc, axis=-1, keepdims=True) + EPS)
        vn = ((vc * inv) * ln_g + ln_b).astype(BF16)
        mxu_work()
        for g in range(GM_GROUPS):
            gs = slice(g * GM_GW, (g + 1) * GM_GW)
            mixed = _dot(w_tri[g], vn[:, gs]) + bst[:, g:g + 1]
            y_ref[rows, gs] = (u[:, gs] * mixed).astype(BF16)
        mxu_work()

    n_sub = ts // sub
    chunks_per_sub = sub // GM_CHUNK
    in_pieces = [slice(c, c + PROJ_PIECE) for c in range(0, 2 * GM_WIDTH, PROJ_PIECE)]
    out_pieces = [slice(c, c + PROJ_PIECE) for c in range(0, x_ref.shape[1], PROJ_PIECE)]

    def rows_of(i):
        return slice(i * sub, (i + 1) * sub)

    def side_items(t):
        lists = []
        if t < n_sub:
            lists.append([functools.partial(project, rows_of(t), cs) for cs in in_pieces])
        if t + 1 < n_sub:
            lists.append([functools.partial(_norm_mod_block, x_ref, hb_ref, gs1, sh1, r)
                          for r in range((t + 1) * sub, (t + 2) * sub, ROW_BLOCK)])
        if 0 <= t - 2 < n_sub:
            lists.append([functools.partial(emit_out, rows_of(t - 2), cs) for cs in out_pieces])
        return _interleave(lists)

    for t in range(n_sub + 2):
        items = side_items(t)
        if 0 <= t - 1 < n_sub:
            work = _spread(items, 4 * chunks_per_sub)
            for ci in range((t - 1) * chunks_per_sub, t * chunks_per_sub):
                gate_chunk(ci, work)
        else:
            for item in items:
                item()


def _mix1_call(x, mod, ng, win, b_in, ln_g, ln_b, ws, bst, wout, *, ts):
    bsz, seq, d = x.shape
    const2 = lambda b, s: (0, 0)
    return pl.pallas_call(
        functools.partial(_mix1_kernel, ts=ts),
        grid=(bsz, seq // ts),
        in_specs=[
            pl.BlockSpec((None, ts, d), lambda b, s: (b, s, 0)),
            pl.BlockSpec((None, 6, d), lambda b, s: (b, 0, 0)),
            pl.BlockSpec((1, d), const2),
            pl.BlockSpec((d, 2 * GM_WIDTH), const2, pipeline_mode=pl.Buffered(1)),
            pl.BlockSpec((1, 2 * GM_WIDTH), const2),
            pl.BlockSpec((1, GM_WIDTH), const2),
            pl.BlockSpec((1, GM_WIDTH), const2),
            pl.BlockSpec((GM_GROUPS, GM_CHUNK, GM_CHUNK), lambda b, s: (0, 0, 0)),
            pl.BlockSpec((GM_CHUNK, GM_GROUPS), const2),
            pl.BlockSpec((GM_WIDTH, d), const2, pipeline_mode=pl.Buffered(1)),
        ],
        out_specs=pl.BlockSpec((None, ts, d), lambda b, s: (b, s, 0)),
        out_shape=jax.ShapeDtypeStruct((bsz, seq, d), F32),
        scratch_shapes=[
            pltpu.VMEM((ts, d), BF16),
            pltpu.VMEM((ts, 2 * GM_WIDTH), F32),
            pltpu.VMEM((ts, GM_WIDTH), BF16),
        ],
        compiler_params=pltpu.CompilerParams(
            dimension_semantics=("parallel", "parallel"),
            vmem_limit_bytes=VMEM_LIMIT_BYTES),
        name="mixer_spatial_gating",
    )(x, mod, ng, win, b_in, ln_g, ln_b, ws, bst, wout)


def _ffn_kernel(x_ref, mod_ref, ng_ref, w1_ref, w2_ref, fg_ref, o_ref, hb_ref, *, ts, final):
    sh2 = mod_ref[3:4, :]
    sc2 = mod_ref[4:5, :]
    g2 = mod_ref[5:6, :]
    gs2 = ng_ref[...] * (1.0 + sc2)

    sub = min(FFN_SUB, ts)
    _norm_mod_to(x_ref, hb_ref, gs2, sh2, sub)
    fcols = D_MODEL
    n_f = D_FF // fcols
    for r0 in range(0, ts, sub):
        rows = slice(r0, r0 + sub)
        between = _spread([functools.partial(_norm_mod_block, x_ref, hb_ref, gs2, sh2, r)
                           for r in range(r0 + sub, min(r0 + 2 * sub, ts), ROW_BLOCK)], n_f)
        acc = None
        for j in range(n_f):
            hid = jnp.maximum(_dot(hb_ref[rows, :], w1_ref[:, j * fcols:(j + 1) * fcols]), 0.0)
            between()
            part = _dot((hid * hid).astype(BF16), w2_ref[j * fcols:(j + 1) * fcols, :])
            acc = part if acc is None else acc + part
        out = x_ref[rows, :] + g2 * acc
        if final:
            inv = lax.rsqrt(jnp.mean(out * out, axis=-1, keepdims=True) + EPS)
            out = (out * inv) * fg_ref[...]
        o_ref[rows, :] = out


def _ffn_call(x, mod, ng, w1, w2, fg, *, ts, final):
    bsz, seq, d = x.shape
    const2 = lambda b, s: (0, 0)
    return pl.pallas_call(
        functools.partial(_ffn_kernel, ts=ts, final=final),
        grid=(bsz, seq // ts),
        in_specs=[
            pl.BlockSpec((None, ts, d), lambda b, s: (b, s, 0)),
            pl.BlockSpec((None, 6, d), lambda b, s: (b, 0, 0)),
            pl.BlockSpec((1, d), const2),
            pl.BlockSpec((d, D_FF), const2, pipeline_mode=pl.Buffered(1)),
            pl.BlockSpec((D_FF, d), const2, pipeline_mode=pl.Buffered(1)),
            pl.BlockSpec((1, d), const2),
        ],
        out_specs=pl.BlockSpec((None, ts, d), lambda b, s: (b, s, 0)),
        out_shape=jax.ShapeDtypeStruct((bsz, seq, d), F32),
        scratch_shapes=[pltpu.VMEM((ts, d), BF16)],
        compiler_params=pltpu.CompilerParams(
            dimension_semantics=("parallel", "parallel"),
            vmem_limit_bytes=VMEM_LIMIT_BYTES),
        name="ffn_final" if final else "ffn",
    )(x, mod, ng, w1, w2, fg)


def kernel(x, c, ada_w, ada_b, norm_mix_g, norm_ffn_g, w_in0, conv_w, conv_b, hg_lb, hg_gain,
           w_out0, w_in1, b_in1, gm_ln_g, gm_ln_b, gm_ws, gm_bs, w_out1, w_ff1, w_ff2, final_g):
    bsz, seq, d = x.shape
    ts = min(SEQ_TILE, seq)
    mod = _ada_call(c, ada_w, ada_b).reshape(ada_w.shape[0], bsz, 6, d)
    fg = final_g.reshape(1, d)

    x = _mix0_call(x, mod[0], norm_mix_g[0:1], w_in0[0].astype(BF16), conv_w[0], conv_b[0:1],
                   hg_lb, hg_gain[0:1], w_out0[0].astype(BF16), ts=ts)
    ts_ffn = min(FFN_TILE, seq)
    x = _ffn_call(x, mod[0], norm_ffn_g[0:1], w_ff1[0].astype(BF16), w_ff2[0].astype(BF16), fg,
                  ts=ts_ffn, final=False)
    x = _mix1_call(x, mod[1], norm_mix_g[1:2], w_in1[0].astype(BF16), b_in1[0:1], gm_ln_g[0:1],
                   gm_ln_b[0:1], gm_ws[0], gm_bs[0].T, w_out1[0].astype(BF16),
                   ts=min(GM_TILE, seq))
    x = _ffn_call(x, mod[1], norm_ffn_g[1:2], w_ff1[1].astype(BF16), w_ff2[1].astype(BF16), fg,
                  ts=ts_ffn, final=True)
    return x
```

```python
import functools

import numpy as np
import jax
import jax.numpy as jnp
from jax import lax
from jax.experimental import pallas as pl
from jax.experimental.pallas import tpu as pltpu

F32 = jnp.float32
BF16 = jnp.bfloat16

EPS = 1e-6
D_MODEL = 1024
D_FF = 4 * D_MODEL
A_WIDTH = 512
B_WIDTH = 512
HG_HEADS = 4
HG_DK = 128
HG_DV = 128
IN0_COLS = 3 * A_WIDTH + 4 * B_WIDTH
GM_WIDTH = D_MODEL
GM_GROUPS = 4
GM_CHUNK = 128
GM_GW = GM_WIDTH // GM_GROUPS

SUBLANES = 8
LANES = 128
VMEM_LIMIT_BYTES = 56 * 1024 * 1024

SEQ_TILE = 512
GM_TILE = 1024
GM_SUB = 256
FFN_TILE = 1024
FFN_SUB = 256
ROW_BLOCK = 64
HG_CHUNK = 64
HG_LEVELS = (32, 16, 8, 4, 2, 1)
HG_BLOCK = 16
HG_EXP_GUARD = 80.0
PROJ_PIECE = 256

_SQRT_2_OVER_PI = float(np.sqrt(2.0 / np.pi))


def _sigmoid(v):
    return 1.0 / (1.0 + jnp.exp(-v))


def _dot(a, b):
    return jnp.dot(a, b, preferred_element_type=F32)


def _dot_nt(a, b):
    return lax.dot_general(a, b, (((1,), (1,)), ((), ())), preferred_element_type=F32)


def _dot_tn(a, b):
    return lax.dot_general(a, b, (((0,), (0,)), ((), ())), preferred_element_type=F32)


def _norm_mod_block(x_ref, hb_ref, gs, sh, r):
    xt = x_ref[pl.ds(r, ROW_BLOCK), :]
    inv = lax.rsqrt(jnp.mean(xt * xt, axis=-1, keepdims=True) + EPS)
    hb_ref[pl.ds(r, ROW_BLOCK), :] = ((xt * inv) * gs + sh).astype(BF16)


def _norm_mod_to(x_ref, hb_ref, gs, sh, rows):
    def body(i, carry):
        _norm_mod_block(x_ref, hb_ref, gs, sh, pl.multiple_of(i * ROW_BLOCK, ROW_BLOCK))
        return carry

    lax.fori_loop(0, rows // ROW_BLOCK, body, 0, unroll=True)


def _ada_kernel(c_ref, w_ref, b_ref, o_ref):
    c = c_ref[...]
    s = c * _sigmoid(c)
    o_ref[...] = jnp.dot(s, w_ref[...], preferred_element_type=F32,
                         precision=lax.Precision.HIGHEST) + b_ref[...]


def _ada_call(c, ada_w, ada_b):
    depth, d, e = ada_w.shape
    bsz = c.shape[0]
    tn = 1536
    return pl.pallas_call(
        _ada_kernel,
        grid=(depth, e // tn),
        in_specs=[
            pl.BlockSpec((bsz, d), lambda l, j: (0, 0)),
            pl.BlockSpec((None, d, tn), lambda l, j: (l, 0, j)),
            pl.BlockSpec((None, 1, tn), lambda l, j: (l, 0, j)),
        ],
        out_specs=pl.BlockSpec((None, bsz, tn), lambda l, j: (l, 0, j)),
        out_shape=jax.ShapeDtypeStruct((depth, bsz, e), F32),
        compiler_params=pltpu.CompilerParams(
            dimension_semantics=("arbitrary", "arbitrary"),
            vmem_limit_bytes=VMEM_LIMIT_BYTES),
        name="ada_modulation",
    )(c, ada_w, ada_b.reshape(depth, 1, e))


def _level_masks():
    ti = lax.broadcasted_iota(jnp.int32, (HG_CHUNK, HG_CHUNK), 0)
    si = lax.broadcasted_iota(jnp.int32, (HG_CHUNK, HG_CHUNK), 1)
    masks = []
    for m in HG_LEVELS:
        other_blk = (ti ^ si) & ~(2 * m - 1)
        masks.append((other_blk | ((ti & m) ^ m) | (si & m)) == 0)
    blk_mask = (((ti ^ si) & ~(HG_BLOCK - 1)) | jnp.maximum(si - ti, 0)) == 0
    return masks, ti == si, blk_mask


def _block_start(c):
    rows, w = c.shape
    parts = [jnp.zeros((HG_BLOCK, w), c.dtype)]
    for b in range(1, rows // HG_BLOCK):
        parts.append(jnp.broadcast_to(c[b * HG_BLOCK - 1:b * HG_BLOCK, :], (HG_BLOCK, w)))
    return jnp.concatenate(parts, axis=0)


def _block_boundary(c, m):
    rows, w = c.shape
    if m >= SUBLANES:
        blk = 2 * m
        return jnp.concatenate(
            [jnp.broadcast_to(c[b * blk + m - 1:b * blk + m, :], (blk, w))
             for b in range(rows // blk)], axis=0)
    c3 = c.reshape(rows // SUBLANES, SUBLANES, w)
    if m == 4:
        bnd = jnp.broadcast_to(c3[:, 3:4, :], c3.shape)
    else:
        assert m == 2
        sub = lax.broadcasted_iota(jnp.int32, c3.shape, 1)
        bnd = jnp.where(sub < 4, jnp.broadcast_to(c3[:, 1:2, :], c3.shape),
                        jnp.broadcast_to(c3[:, 5:6, :], c3.shape))
    return bnd.reshape(rows, w)


def _chunk_cumsum(x):
    rows, w = x.shape
    nv = rows // SUBLANES
    x3 = x.reshape(nv, SUBLANES, w)
    sub = lax.broadcasted_iota(jnp.int32, x3.shape, 1)
    sh = 1
    while sh < SUBLANES:
        x3 = x3 + jnp.where(sub >= sh, pltpu.roll(x3, sh, axis=1), 0.0)
        sh *= 2
    tot = jnp.broadcast_to(x3[:, SUBLANES - 1:SUBLANES, :], x3.shape)
    outs = [x3[0]]
    acc = tot[0]
    for j in range(1, nv):
        outs.append(x3[j] + acc)
        if j + 1 < nv:
            acc = acc + tot[j]
    return jnp.concatenate(outs, axis=0)


def _spread(items, slots):
    state = {"calls": 0, "done": 0}

    def run_due():
        state["calls"] += 1
        want = min(len(items), -(-len(items) * state["calls"] // slots))
        while state["done"] < want:
            items[state["done"]]()
            state["done"] += 1
    return run_due


def _interleave(lists):
    keyed = [((i + 0.5) / len(l), n, item) for n, l in enumerate(lists) for i, item in enumerate(l)]
    return [item for _, _, item in sorted(keyed, key=lambda k: k[:2])]


def _mix0_kernel(x_ref, mod_ref, ng_ref, win_ref, cw_ref, cb_ref, hglb_ref, gain_ref, wout_ref,
                 o_ref, hb_ref, p_ref, y_ref, st_ref, zc_ref, cum_ref, ck_ref, f_ref, worst_ref,
                 *, ts):
    @pl.when(pl.program_id(1) == 0)
    def _():
        st_ref[...] = jnp.zeros_like(st_ref)
        zc_ref[...] = jnp.zeros_like(zc_ref)

    half = ts // 2
    rows_a, rows_b = slice(0, half), slice(half, ts)
    chunks_per_half = half // HG_CHUNK
    sh1 = mod_ref[0:1, :]
    gs1 = ng_ref[...] * (1.0 + mod_ref[1:2, :])
    g1 = mod_ref[2:3, :]
    _norm_mod_to(x_ref, hb_ref, gs1, sh1, half)

    a0, a1, a2 = hglb_ref[0:1, :], hglb_ref[1:2, :], hglb_ref[2:3, :]
    amax = jnp.maximum(jnp.maximum(a0, a1), a2)
    e0, e1, e2 = jnp.exp(a0 - amax), jnp.exp(a1 - amax), jnp.exp(a2 - amax)
    lb = e0 / (e0 + e1 + e2)

    cw0, cw1, cw2 = cw_ref[0:1, :], cw_ref[1:2, :], cw_ref[2:3, :]
    cb = cb_ref[...]
    gain = gain_ref[...]
    row8 = lax.broadcasted_iota(jnp.int32, (SUBLANES, A_WIDTH), 0)
    masks, diag, blk_mask = _level_masks()

    in_pieces = [slice(c, c + PROJ_PIECE) for c in range(0, IN0_COLS, PROJ_PIECE)]
    out_pieces = [slice(c, c + PROJ_PIECE) for c in range(0, D_MODEL, PROJ_PIECE)]

    def project(rows, cs):
        p_ref[rows, cs] = _dot(hb_ref[rows, :], win_ref[:, cs])

    def emit_out(rows, cs):
        o_ref[rows, cs] = x_ref[rows, cs] + g1[:, cs] * _dot(y_ref[rows, :], wout_ref[:, cs])

    def gates_chunk(ci, other_work):
        rows = slice(ci * HG_CHUNK, (ci + 1) * HG_CHUNK)

        def col(j):
            return p_ref[rows, j * 512:(j + 1) * 512]

        z = col(1) * col(2)
        prev = zc_ref[...]
        zc_ref[...] = z[HG_CHUNK - SUBLANES:, :]

        def shifted(k):
            zr = pltpu.roll(z, k, axis=0)
            pr = pltpu.roll(prev, k, axis=0)
            head = jnp.where(row8 < k, pr, zr[:SUBLANES])
            return jnp.concatenate([head, zr[SUBLANES:]], axis=0)

        conv = cb + shifted(2) * cw0
        conv = conv + shifted(1) * cw1
        conv = conv + z * cw2
        y_ref[rows, 0:A_WIDTH] = (col(0) * conv).astype(BF16)
        other_work()

        f = lb + (1.0 - lb) * _sigmoid(col(4))
        cum = _chunk_cumsum(jnp.log(f))
        ck = cum - jnp.log(1.0 - f)
        f_ref[rows, :] = f
        cum_ref[rows, :] = cum
        ck_ref[rows, :] = ck
        e = _block_start(cum) - ck
        worst_ref[...] = e if ci % chunks_per_half == 0 else jnp.maximum(worst_ref[...], e)
        other_work()


    def scores_stage(ci, fast):
        rows = slice(ci * HG_CHUNK, (ci + 1) * HG_CHUNK)
        q = p_ref[rows, 3 * 512:4 * 512]
        v = p_ref[rows, 5 * 512:6 * 512]
        cum = cum_ref[rows, :]
        ck = ck_ref[rows, :]
        last = cum[HG_CHUNK - 1:HG_CHUNK, :]
        q_in = q * jnp.exp(cum)
        k_dec = jnp.exp(last - ck)
        if not fast:
            f = f_ref[rows, :]
            kk = 1.0 - f
            qf = q * f

        heads = []
        for h in range(HG_HEADS):
            hs = slice(h * HG_DK, (h + 1) * HG_DK)
            q_h, cum_h, ck_h = q[:, hs], cum[:, hs], ck[:, hs]
            if fast:
                bnd = _block_start(cum_h)
                qt = (q_h * jnp.exp(cum_h - bnd)).astype(BF16)
                kt = jnp.exp(bnd - ck_h).astype(BF16)
                parts = [(blk_mask, _dot_nt(qt, kt))]
                levels = [(m, mask) for m, mask in zip(HG_LEVELS, masks) if m >= HG_BLOCK]
            else:
                k_b = kk[:, hs].astype(BF16)
                parts = [(diag, _dot_nt(q_h.astype(BF16), k_b))]
                levels = list(zip(HG_LEVELS, masks))
            for m, mask in levels:
                if m == 1:
                    qt, kt = qf[:, hs].astype(BF16), k_b
                else:
                    bnd = _block_boundary(cum_h, m)
                    qt = (q_h * jnp.exp(cum_h - bnd)).astype(BF16)
                    kt = jnp.exp(bnd - ck_h).astype(BF16)
                parts.append((mask, _dot_nt(qt, kt)))
            heads.append(dict(parts=parts, q_in=q_in[:, hs].astype(BF16),
                              k_dec=k_dec[:, hs].astype(BF16), v=v[:, hs].astype(BF16)))
        return dict(rows=rows, heads=heads, dec=jnp.exp(last))

    def state_stage(ctx):
        for h, hd in enumerate(ctx["heads"]):
            hs = slice(h * HG_DK, (h + 1) * HG_DK)
            scores = None
            for mask, s in hd["parts"]:
                scores = jnp.where(mask, s, 0.0 if scores is None else scores)
            st = st_ref[h]
            hd["o"] = _dot_nt(hd["q_in"], st.astype(BF16)) + _dot(scores.astype(BF16), hd["v"])
            st_ref[h] = st * ctx["dec"][:, hs] + _dot_tn(hd["v"], hd["k_dec"])

    def output_stage(ctx):
        rows = ctx["rows"]
        gate = p_ref[rows, 6 * 512:7 * 512]
        gsil = gate * _sigmoid(gate)
        for h, hd in enumerate(ctx["heads"]):
            hs = slice(h * HG_DK, (h + 1) * HG_DK)
            o = hd["o"]
            o = o * lax.rsqrt(jnp.mean(o * o, axis=-1, keepdims=True) + EPS)
            y_ref[rows, A_WIDTH + h * HG_DV:A_WIDTH + (h + 1) * HG_DV] = (
                o * gain[:, hs] * gsil[:, hs]).astype(BF16)

    chunks_a = range(chunks_per_half)
    chunks_b = range(chunks_per_half, 2 * chunks_per_half)

    def is_gate_col(cs):
        return cs.start < 3 * A_WIDTH or 3 * A_WIDTH + B_WIDTH <= cs.start < 3 * A_WIDTH + 2 * B_WIDTH

    gate_pieces = [cs for cs in in_pieces if is_gate_col(cs)]
    rest_pieces = [cs for cs in in_pieces if not is_gate_col(cs)]

    norm_b = _spread([functools.partial(_norm_mod_block, x_ref, hb_ref, gs1, sh1, r)
                      for r in range(half, ts, ROW_BLOCK)], len(gate_pieces))
    for cs in gate_pieces:
        project(rows_a, cs)
        norm_b()

    work = _spread([functools.partial(project, rows_b, cs) for cs in gate_pieces],
                   2 * chunks_per_half)
    for ci in chunks_a:
        gates_chunk(ci, work)
    fast_a = jnp.max(worst_ref[...]) < HG_EXP_GUARD

    work = _spread([functools.partial(project, rows, cs)
                    for rows in (rows_a, rows_b) for cs in rest_pieces], 2 * chunks_per_half)
    for ci in chunks_b:
        gates_chunk(ci, work)
    fast_b = jnp.max(worst_ref[...]) < HG_EXP_GUARD

    def recur_half(chunks, other_items, fast):
        chunks = list(chunks)
        n = len(chunks)
        other_work = _spread(other_items, 3 * (n + 2))
        ctx = {}
        for t in range(n + 2):
            if t < n:
                ctx[t] = scores_stage(chunks[t], fast)
            other_work()
            if 0 <= t - 1 < n:
                state_stage(ctx[t - 1])
            other_work()
            if 0 <= t - 2 < n:
                output_stage(ctx.pop(t - 2))
            other_work()

    def both_paths(fast_ok, chunks, other_items):
        pl.when(fast_ok)(functools.partial(recur_half, chunks, other_items, True))
        pl.when(jnp.logical_not(fast_ok))(
            functools.partial(recur_half, chunks, other_items, False))

    both_paths(fast_a, chunks_a, [])
    both_paths(fast_b, chunks_b, [functools.partial(emit_out, rows_a, cs) for cs in out_pieces])
    for cs in out_pieces:
        emit_out(rows_b, cs)


def _mix0_call(x, mod, ng, win, cw, cb, hglb, gain, wout, *, ts):
    bsz, seq, d = x.shape
    assert ts % (2 * HG_CHUNK) == 0
    const2 = lambda b, s: (0, 0)
    return pl.pallas_call(
        functools.partial(_mix0_kernel, ts=ts),
        grid=(bsz, seq // ts),
        in_specs=[
            pl.BlockSpec((None, ts, d), lambda b, s: (b, s, 0)),
            pl.BlockSpec((None, 6, d), lambda b, s: (b, 0, 0)),
            pl.BlockSpec((1, d), const2),
            pl.BlockSpec((d, IN0_COLS), const2, pipeline_mode=pl.Buffered(1)),
            pl.BlockSpec((3, A_WIDTH), const2),
            pl.BlockSpec((1, A_WIDTH), const2),
            pl.BlockSpec((3, B_WIDTH), const2),
            pl.BlockSpec((1, B_WIDTH), const2),
            pl.BlockSpec((d, d), const2, pipeline_mode=pl.Buffered(1)),
        ],
        out_specs=pl.BlockSpec((None, ts, d), lambda b, s: (b, s, 0)),
        out_shape=jax.ShapeDtypeStruct((bsz, seq, d), F32),
        scratch_shapes=[
            pltpu.VMEM((ts, d), BF16),
            pltpu.VMEM((ts, IN0_COLS), F32),
            pltpu.VMEM((ts, d), BF16),
            pltpu.VMEM((HG_HEADS, HG_DV, HG_DK), F32),
            pltpu.VMEM((SUBLANES, A_WIDTH), F32),
            pltpu.VMEM((ts, B_WIDTH), F32),
            pltpu.VMEM((ts, B_WIDTH), F32),
            pltpu.VMEM((ts, B_WIDTH), F32),
            pltpu.VMEM((HG_CHUNK, B_WIDTH), F32),
        ],
        compiler_params=pltpu.CompilerParams(
            dimension_semantics=("parallel", "arbitrary"),
            vmem_limit_bytes=VMEM_LIMIT_BYTES),
        name="mixer_conv_hgrn2",
    )(x, mod, ng, win, cw, cb, hglb, gain, wout)


def _mix1_kernel(x_ref, mod_ref, ng_ref, win_ref, bin_ref, lng_ref, lnb_ref, ws_ref,
                 bst_ref, wout_ref, o_ref, hb_ref, z_ref, y_ref, *, ts):
    sh1 = mod_ref[0:1, :]
    sc1 = mod_ref[1:2, :]
    g1 = mod_ref[2:3, :]
    gs1 = ng_ref[...] * (1.0 + sc1)
    sub = min(GM_SUB, ts)
    _norm_mod_to(x_ref, hb_ref, gs1, sh1, sub)

    b_in = bin_ref[...]
    ln_g = lng_ref[...]
    ln_b = lnb_ref[...]
    bst = bst_ref[...]
    ti = lax.broadcasted_iota(jnp.int32, (GM_CHUNK, GM_CHUNK), 0)
    si = lax.broadcasted_iota(jnp.int32, (GM_CHUNK, GM_CHUNK), 1)
    w_tri = [jnp.where(si <= ti, ws_ref[g], 0.0).astype(BF16) for g in range(GM_GROUPS)]

    def gelu(t):
        return t * (0.5 * (1.0 + jnp.tanh(_SQRT_2_OVER_PI * (t + 0.044715 * (t * t * t)))))

    def project(rows, cs):
        z_ref[rows, cs] = _dot(hb_ref[rows, :], win_ref[:, cs])

    def emit_out(rows, cs):
        o_ref[rows, cs] = x_ref[rows, cs] + g1[:, cs] * _dot(y_ref[rows, :], wout_ref[:, cs])

    def gate_chunk(ci, mxu_work):
        rows = slice(ci * GM_CHUNK, (ci + 1) * GM_CHUNK)
        u = gelu(z_ref[rows, 0:GM_WIDTH] + b_in[:, 0:GM_WIDTH])
        mxu_work()
        v = gelu(z_ref[rows, GM_WIDTH:2 * GM_WIDTH] + b_in[:, GM_WIDTH:])
        mxu_work()
        mu = jnp.mean(v, axis=-1, keepdims=True)
        vc = v - mu
        inv = lax.rsqrt(jnp.mean(vc * vc, axis=-1, keepdims=True) + EPS)
        vn = ((vc * inv) * ln_g + ln_b).astype(BF16)
        mxu_work()
        for g in range(GM_GROUPS):
            gs = slice(g * GM_GW, (g + 1) * GM_GW)
            mixed = _dot(w_tri[g], vn[:, gs]) + bst[:, g:g + 1]
            y_ref[rows, gs] = (u[:, gs] * mixed).astype(BF16)
        mxu_work()

    n_sub = ts // sub
    chunks_per_sub = sub // GM_CHUNK
    in_pieces = [slice(c, c + PROJ_PIECE) for c in range(0, 2 * GM_WIDTH, PROJ_PIECE)]
    out_pieces = [slice(c, c + PROJ_PIECE) for c in range(0, x_ref.shape[1], PROJ_PIECE)]

    def rows_of(i):
        return slice(i * sub, (i + 1) * sub)

    def side_items(t):
        lists = []
        if t < n_sub:
            lists.append([functools.partial(project, rows_of(t), cs) for cs in in_pieces])
        if t + 1 < n_sub:
            lists.append([functools.partial(_norm_mod_block, x_ref, hb_ref, gs1, sh1, r)
                          for r in range((t + 1) * sub, (t + 2) * sub, ROW_BLOCK)])
        if 0 <= t - 2 < n_sub:
            lists.append([functools.partial(emit_out, rows_of(t - 2), cs) for cs in out_pieces])
        return _interleave(lists)

    for t in range(n_sub + 2):
        items = side_items(t)
        if 0 <= t - 1 < n_sub:
            work = _spread(items, 4 * chunks_per_sub)
            for ci in range((t - 1) * chunks_per_sub, t * chunks_per_sub):
                gate_chunk(ci, work)
        else:
            for item in items:
                item()


def _mix1_call(x, mod, ng, win, b_in, ln_g, ln_b, ws, bst, wout, *, ts):
    bsz, seq, d = x.shape
    const2 = lambda b, s: (0, 0)
    return pl.pallas_call(
        functools.partial(_mix1_kernel, ts=ts),
        grid=(bsz, seq // ts),
        in_specs=[
            pl.BlockSpec((None, ts, d), lambda b, s: (b, s, 0)),
            pl.BlockSpec((None, 6, d), lambda b, s: (b, 0, 0)),
            pl.BlockSpec((1, d), const2),
            pl.BlockSpec((d, 2 * GM_WIDTH), const2, pipeline_mode=pl.Buffered(1)),
            pl.BlockSpec((1, 2 * GM_WIDTH), const2),
            pl.BlockSpec((1, GM_WIDTH), const2),
            pl.BlockSpec((1, GM_WIDTH), const2),
            pl.BlockSpec((GM_GROUPS, GM_CHUNK, GM_CHUNK), lambda b, s: (0, 0, 0)),
            pl.BlockSpec((GM_CHUNK, GM_GROUPS), const2),
            pl.BlockSpec((GM_WIDTH, d), const2, pipeline_mode=pl.Buffered(1)),
        ],
        out_specs=pl.BlockSpec((None, ts, d), lambda b, s: (b, s, 0)),
        out_shape=jax.ShapeDtypeStruct((bsz, seq, d), F32),
        scratch_shapes=[
            pltpu.VMEM((ts, d), BF16),
            pltpu.VMEM((ts, 2 * GM_WIDTH), F32),
            pltpu.VMEM((ts, GM_WIDTH), BF16),
        ],
        compiler_params=pltpu.CompilerParams(
            dimension_semantics=("parallel", "parallel"),
            vmem_limit_bytes=VMEM_LIMIT_BYTES),
        name="mixer_spatial_gating",
    )(x, mod, ng, win, b_in, ln_g, ln_b, ws, bst, wout)


def _ffn_kernel(x_ref, mod_ref, ng_ref, w1_ref, w2_ref, fg_ref, o_ref, hb_ref, *, ts, final):
    sh2 = mod_ref[3:4, :]
    sc2 = mod_ref[4:5, :]
    g2 = mod_ref[5:6, :]
    gs2 = ng_ref[...] * (1.0 + sc2)

    sub = min(FFN_SUB, ts)
    _norm_mod_to(x_ref, hb_ref, gs2, sh2, sub)
    fcols = D_MODEL
    n_f = D_FF // fcols
    for r0 in range(0, ts, sub):
        rows = slice(r0, r0 + sub)
        between = _spread([functools.partial(_norm_mod_block, x_ref, hb_ref, gs2, sh2, r)
                           for r in range(r0 + sub, min(r0 + 2 * sub, ts), ROW_BLOCK)], n_f)
        acc = None
        for j in range(n_f):
            hid = jnp.maximum(_dot(hb_ref[rows, :], w1_ref[:, j * fcols:(j + 1) * fcols]), 0.0)
            between()
            part = _dot((hid * hid).astype(BF16), w2_ref[j * fcols:(j + 1) * fcols, :])
            acc = part if acc is None else acc + part
        out = x_ref[rows, :] + g2 * acc
        if final:
            inv = lax.rsqrt(jnp.mean(out * out, axis=-1, keepdims=True) + EPS)
            out = (out * inv) * fg_ref[...]
        o_ref[rows, :] = out


def _ffn_call(x, mod, ng, w1, w2, fg, *, ts, final):
    bsz, seq, d = x.shape
    const2 = lambda b, s: (0, 0)
    return pl.pallas_call(
        functools.partial(_ffn_kernel, ts=ts, final=final),
        grid=(bsz, seq // ts),
        in_specs=[
            pl.BlockSpec((None, ts, d), lambda b, s: (b, s, 0)),
            pl.BlockSpec((None, 6, d), lambda b, s: (b, 0, 0)),
            pl.BlockSpec((1, d), const2),
            pl.BlockSpec((d, D_FF), const2, pipeline_mode=pl.Buffered(1)),
            pl.BlockSpec((D_FF, d), const2, pipeline_mode=pl.Buffered(1)),
            pl.BlockSpec((1, d), const2),
        ],
        out_specs=pl.BlockSpec((None, ts, d), lambda b, s: (b, s, 0)),
        out_shape=jax.ShapeDtypeStruct((bsz, seq, d), F32),
        scratch_shapes=[pltpu.VMEM((ts, d), BF16)],
        compiler_params=pltpu.CompilerParams(
            dimension_semantics=("parallel", "parallel"),
            vmem_limit_bytes=VMEM_LIMIT_BYTES),
        name="ffn_final" if final else "ffn",
    )(x, mod, ng, w1, w2, fg)


def kernel(x, c, ada_w, ada_b, norm_mix_g, norm_ffn_g, w_in0, conv_w, conv_b, hg_lb, hg_gain,
           w_out0, w_in1, b_in1, gm_ln_g, gm_ln_b, gm_ws, gm_bs, w_out1, w_ff1, w_ff2, final_g):
    bsz, seq, d = x.shape
    ts = min(SEQ_TILE, seq)
    mod = _ada_call(c, ada_w, ada_b).reshape(ada_w.shape[0], bsz, 6, d)
    fg = final_g.reshape(1, d)

    x = _mix0_call(x, mod[0], norm_mix_g[0:1], w_in0[0].astype(BF16), conv_w[0], conv_b[0:1],
                   hg_lb, hg_gain[0:1], w_out0[0].astype(BF16), ts=ts)
    ts_ffn = min(FFN_TILE, seq)
    x = _ffn_call(x, mod[0], norm_ffn_g[0:1], w_ff1[0].astype(BF16), w_ff2[0].astype(BF16), fg,
                  ts=ts_ffn, final=False)
    x = _mix1_call(x, mod[1], norm_mix_g[1:2], w_in1[0].astype(BF16), b_in1[0:1], gm_ln_g[0:1],
                   gm_ln_b[0:1], gm_ws[0], gm_bs[0].T, w_out1[0].astype(BF16),
                   ts=min(GM_TILE, seq))
    x = _ffn_call(x, mod[1], norm_ffn_g[1:2], w_ff1[1].astype(BF16), w_ff2[1].astype(BF16), fg,
                  ts=ts_ffn, final=True)
    return x
```

```python
import functools

import numpy as np
import jax
import jax.numpy as jnp
from jax import lax
from jax.experimental import pallas as pl
from jax.experimental.pallas import tpu as pltpu

F32 = jnp.float32
BF16 = jnp.bfloat16

EPS = 1e-6
D_MODEL = 1024
D_FF = 4 * D_MODEL
A_WIDTH = 512
B_WIDTH = 512
HG_HEADS = 4
HG_DK = 128
HG_DV = 128
IN0_COLS = 3 * A_WIDTH + 4 * B_WIDTH
GM_WIDTH = D_MODEL
GM_GROUPS = 4
GM_CHUNK = 128
GM_GW = GM_WIDTH // GM_GROUPS

SUBLANES = 8
LANES = 128
VMEM_LIMIT_BYTES = 56 * 1024 * 1024

SEQ_TILE = 512
GM_TILE = 1024
GM_SUB = 256
FFN_TILE = 1024
FFN_SUB = 256
ROW_BLOCK = 64
HG_CHUNK = 64
HG_LEVELS = (32, 16, 8, 4, 2, 1)
HG_BLOCK = 16
HG_EXP_GUARD = 80.0
PROJ_PIECE = 256

_SQRT_2_OVER_PI = float(np.sqrt(2.0 / np.pi))


def _sigmoid(v):
    return 1.0 / (1.0 + jnp.exp(-v))


def _dot(a, b):
    return jnp.dot(a, b, preferred_element_type=F32)


def _dot_nt(a, b):
    return lax.dot_general(a, b, (((1,), (1,)), ((), ())), preferred_element_type=F32)


def _dot_tn(a, b):
    return lax.dot_general(a, b, (((0,), (0,)), ((), ())), preferred_element_type=F32)


def _norm_mod_block(x_ref, hb_ref, gs, sh, r):
    xt = x_ref[pl.ds(r, ROW_BLOCK), :]
    inv = lax.rsqrt(jnp.mean(xt * xt, axis=-1, keepdims=True) + EPS)
    hb_ref[pl.ds(r, ROW_BLOCK), :] = ((xt * inv) * gs + sh).astype(BF16)


def _norm_mod_to(x_ref, hb_ref, gs, sh, rows):
    def body(i, carry):
        _norm_mod_block(x_ref, hb_ref, gs, sh, pl.multiple_of(i * ROW_BLOCK, ROW_BLOCK))
        return carry

    lax.fori_loop(0, rows // ROW_BLOCK, body, 0, unroll=True)


def _ada_kernel(c_ref, w_ref, b_ref, o_ref):
    c = c_ref[...]
    s = c * _sigmoid(c)
    o_ref[...] = jnp.dot(s, w_ref[...], preferred_element_type=F32,
                         precision=lax.Precision.HIGHEST) + b_ref[...]


def _ada_call(c, ada_w, ada_b):
    depth, d, e = ada_w.shape
    bsz = c.shape[0]
    tn = 1536
    return pl.pallas_call(
        _ada_kernel,
        grid=(depth, e // tn),
        in_specs=[
            pl.BlockSpec((bsz, d), lambda l, j: (0, 0)),
            pl.BlockSpec((None, d, tn), lambda l, j: (l, 0, j)),
            pl.BlockSpec((None, 1, tn), lambda l, j: (l, 0, j)),
        ],
        out_specs=pl.BlockSpec((None, bsz, tn), lambda l, j: (l, 0, j)),
        out_shape=jax.ShapeDtypeStruct((depth, bsz, e), F32),
        compiler_params=pltpu.CompilerParams(
            dimension_semantics=("arbitrary", "arbitrary"),
            vmem_limit_bytes=VMEM_LIMIT_BYTES),
        name="ada_modulation",
    )(c, ada_w, ada_b.reshape(depth, 1, e))


def _level_masks():
    ti = lax.broadcasted_iota(jnp.int32, (HG_CHUNK, HG_CHUNK), 0)
    si = lax.broadcasted_iota(jnp.int32, (HG_CHUNK, HG_CHUNK), 1)
    masks = []
    for m in HG_LEVELS:
        other_blk = (ti ^ si) & ~(2 * m - 1)
        masks.append((other_blk | ((ti & m) ^ m) | (si & m)) == 0)
    blk_mask = (((ti ^ si) & ~(HG_BLOCK - 1)) | jnp.maximum(si - ti, 0)) == 0
    return masks, ti == si, blk_mask


def _block_start(c):
    rows, w = c.shape
    parts = [jnp.zeros((HG_BLOCK, w), c.dtype)]
    for b in range(1, rows // HG_BLOCK):
        parts.append(jnp.broadcast_to(c[b * HG_BLOCK - 1:b * HG_BLOCK, :], (HG_BLOCK, w)))
    return jnp.concatenate(parts, axis=0)


def _block_boundary(c, m):
    rows, w = c.shape
    if m >= SUBLANES:
        blk = 2 * m
        return jnp.concatenate(
            [jnp.broadcast_to(c[b * blk + m - 1:b * blk + m, :], (blk, w))
             for b in range(rows // blk)], axis=0)
    c3 = c.reshape(rows // SUBLANES, SUBLANES, w)
    if m == 4:
        bnd = jnp.broadcast_to(c3[:, 3:4, :], c3.shape)
    else:
        assert m == 2
        sub = lax.broadcasted_iota(jnp.int32, c3.shape, 1)
        bnd = jnp.where(sub < 4, jnp.broadcast_to(c3[:, 1:2, :], c3.shape),
                        jnp.broadcast_to(c3[:, 5:6, :], c3.shape))
    return bnd.reshape(rows, w)


def _chunk_cumsum(x):
    rows, w = x.shape
    nv = rows // SUBLANES
    x3 = x.reshape(nv, SUBLANES, w)
    sub = lax.broadcasted_iota(jnp.int32, x3.shape, 1)
    sh = 1
    while sh < SUBLANES:
        x3 = x3 + jnp.where(sub >= sh, pltpu.roll(x3, sh, axis=1), 0.0)
        sh *= 2
    tot = jnp.broadcast_to(x3[:, SUBLANES - 1:SUBLANES, :], x3.shape)
    outs = [x3[0]]
    acc = tot[0]
    for j in range(1, nv):
        outs.append(x3[j] + acc)
        if j + 1 < nv:
            acc = acc + tot[j]
    return jnp.concatenate(outs, axis=0)


def _spread(items, slots):
    state = {"calls": 0, "done": 0}

    def run_due():
        state["calls"] += 1
        want = min(len(items), -(-len(items) * state["calls"] // slots))
        while state["done"] < want:
            items[state["done"]]()
            state["done"] += 1
    return run_due


def _interleave(lists):
    keyed = [((i + 0.5) / len(l), n, item) for n, l in enumerate(lists) for i, item in enumerate(l)]
    return [item for _, _, item in sorted(keyed, key=lambda k: k[:2])]


def _mix0_kernel(x_ref, mod_ref, ng_ref, win_ref, cw_ref, cb_ref, hglb_ref, gain_ref, wout_ref,
                 o_ref, hb_ref, p_ref, y_ref, st_ref, zc_ref, cum_ref, ck_ref, f_ref, worst_ref,
                 *, ts):
    @pl.when(pl.program_id(1) == 0)
    def _():
        st_ref[...] = jnp.zeros_like(st_ref)
        zc_ref[...] = jnp.zeros_like(zc_ref)

    half = ts // 2
    rows_a, rows_b = slice(0, half), slice(half, ts)
    chunks_per_half = half // HG_CHUNK
    sh1 = mod_ref[0:1, :]
    gs1 = ng_ref[...] * (1.0 + mod_ref[1:2, :])
    g1 = mod_ref[2:3, :]
    _norm_mod_to(x_ref, hb_ref, gs1, sh1, half)

    a0, a1, a2 = hglb_ref[0:1, :], hglb_ref[1:2, :], hglb_ref[2:3, :]
    amax = jnp.maximum(jnp.maximum(a0, a1), a2)
    e0, e1, e2 = jnp.exp(a0 - amax), jnp.exp(a1 - amax), jnp.exp(a2 - amax)
    lb = e0 / (e0 + e1 + e2)

    cw0, cw1, cw2 = cw_ref[0:1, :], cw_ref[1:2, :], cw_ref[2:3, :]
    cb = cb_ref[...]
    gain = gain_ref[...]
    row8 = lax.broadcasted_iota(jnp.int32, (SUBLANES, A_WIDTH), 0)
    masks, diag, blk_mask = _level_masks()

    in_pieces = [slice(c, c + PROJ_PIECE) for c in range(0, IN0_COLS, PROJ_PIECE)]
    out_pieces = [slice(c, c + PROJ_PIECE) for c in range(0, D_MODEL, PROJ_PIECE)]

    def project(rows, cs):
        p_ref[rows, cs] = _dot(hb_ref[rows, :], win_ref[:, cs])

    def emit_out(rows, cs):
        o_ref[rows, cs] = x_ref[rows, cs] + g1[:, cs] * _dot(y_ref[rows, :], wout_ref[:, cs])

    def gates_chunk(ci, other_work):
        rows = slice(ci * HG_CHUNK, (ci + 1) * HG_CHUNK)

        def col(j):
            return p_ref[rows, j * 512:(j + 1) * 512]

        z = col(1) * col(2)
        prev = zc_ref[...]
        zc_ref[...] = z[HG_CHUNK - SUBLANES:, :]

        def shifted(k):
            zr = pltpu.roll(z, k, axis=0)
            pr = pltpu.roll(prev, k, axis=0)
            head = jnp.where(row8 < k, pr, zr[:SUBLANES])
            return jnp.concatenate([head, zr[SUBLANES:]], axis=0)

        conv = cb + shifted(2) * cw0
        conv = conv + shifted(1) * cw1
        conv = conv + z * cw2
        y_ref[rows, 0:A_WIDTH] = (col(0) * conv).astype(BF16)
        other_work()

        f = lb + (1.0 - lb) * _sigmoid(col(4))
        cum = _chunk_cumsum(jnp.log(f))
        ck = cum - jnp.log(1.0 - f)
        f_ref[rows, :] = f
        cum_ref[rows, :] = cum
        ck_ref[rows, :] = ck
        e = _block_start(cum) - ck
        worst_ref[...] = e if ci == 0 else jnp.maximum(worst_ref[...], e)
        other_work()


    def scores_stage(ci, fast):
        rows = slice(ci * HG_CHUNK, (ci + 1) * HG_CHUNK)
        q = p_ref[rows, 3 * 512:4 * 512]
        v = p_ref[rows, 5 * 512:6 * 512]
        cum = cum_ref[rows, :]
        ck = ck_ref[rows, :]
        last = cum[HG_CHUNK - 1:HG_CHUNK, :]
        q_in = q * jnp.exp(cum)
        k_dec = jnp.exp(last - ck)
        if not fast:
            f = f_ref[rows, :]
            kk = 1.0 - f
            qf = q * f

        heads = []
        for h in range(HG_HEADS):
            hs = slice(h * HG_DK, (h + 1) * HG_DK)
            q_h, cum_h, ck_h = q[:, hs], cum[:, hs], ck[:, hs]
            if fast:
                bnd = _block_start(cum_h)
                qt = (q_h * jnp.exp(cum_h - bnd)).astype(BF16)
                kt = jnp.exp(bnd - ck_h).astype(BF16)
                parts = [(blk_mask, _dot_nt(qt, kt))]
                levels = [(m, mask) for m, mask in zip(HG_LEVELS, masks) if m >= HG_BLOCK]
            else:
                k_b = kk[:, hs].astype(BF16)
                parts = [(diag, _dot_nt(q_h.astype(BF16), k_b))]
                levels = list(zip(HG_LEVELS, masks))
            for m, mask in levels:
                if m == 1:
                    qt, kt = qf[:, hs].astype(BF16), k_b
                else:
                    bnd = _block_boundary(cum_h, m)
                    qt = (q_h * jnp.exp(cum_h - bnd)).astype(BF16)
                    kt = jnp.exp(bnd - ck_h).astype(BF16)
                parts.append((mask, _dot_nt(qt, kt)))
            heads.append(dict(parts=parts, q_in=q_in[:, hs].astype(BF16),
                              k_dec=k_dec[:, hs].astype(BF16), v=v[:, hs].astype(BF16)))
        return dict(rows=rows, heads=heads, dec=jnp.exp(last))

    def state_stage(ctx):
        for h, hd in enumerate(ctx["heads"]):
            hs = slice(h * HG_DK, (h + 1) * HG_DK)
            scores = None
            for mask, s in hd["parts"]:
                scores = jnp.where(mask, s, 0.0 if scores is None else scores)
            st = st_ref[h]
            hd["o"] = _dot_nt(hd["q_in"], st.astype(BF16)) + _dot(scores.astype(BF16), hd["v"])
            st_ref[h] = st * ctx["dec"][:, hs] + _dot_tn(hd["v"], hd["k_dec"])

    def output_stage(ctx):
        rows = ctx["rows"]
        gate = p_ref[rows, 6 * 512:7 * 512]
        gsil = gate * _sigmoid(gate)
        for h, hd in enumerate(ctx["heads"]):
            hs = slice(h * HG_DK, (h + 1) * HG_DK)
            o = hd["o"]
            o = o * lax.rsqrt(jnp.mean(o * o, axis=-1, keepdims=True) + EPS)
            y_ref[rows, A_WIDTH + h * HG_DV:A_WIDTH + (h + 1) * HG_DV] = (
                o * gain[:, hs] * gsil[:, hs]).astype(BF16)

    chunks_a = range(chunks_per_half)
    chunks_b = range(chunks_per_half, 2 * chunks_per_half)

    def is_gate_col(cs):
        return cs.start < 3 * A_WIDTH or 3 * A_WIDTH + B_WIDTH <= cs.start < 3 * A_WIDTH + 2 * B_WIDTH

    gate_pieces = [cs for cs in in_pieces if is_gate_col(cs)]
    rest_pieces = [cs for cs in in_pieces if not is_gate_col(cs)]

    norm_b = _spread([functools.partial(_norm_mod_block, x_ref, hb_ref, gs1, sh1, r)
                      for r in range(half, ts, ROW_BLOCK)], len(gate_pieces))
    for cs in gate_pieces:
        project(rows_a, cs)
        norm_b()

    work = _spread([functools.partial(project, rows_b, cs) for cs in gate_pieces],
                   2 * chunks_per_half)
    for ci in chunks_a:
        gates_chunk(ci, work)

    work = _spread([functools.partial(project, rows, cs)
                    for rows in (rows_a, rows_b) for cs in rest_pieces], 2 * chunks_per_half)
    for ci in chunks_b:
        gates_chunk(ci, work)
    fast_ok = jnp.max(worst_ref[...]) < HG_EXP_GUARD

    def recur_tile(fast):
        n = 2 * chunks_per_half
        ready = []
        ctx = {}

        def other_work():
            if ready:
                ready.pop(0)()

        for t in range(n + 2):
            if t < n:
                ctx[t] = scores_stage(t, fast)
            other_work()
            if 0 <= t - 1 < n:
                state_stage(ctx[t - 1])
            other_work()
            if 0 <= t - 2 < n:
                output_stage(ctx.pop(t - 2))
                if t - 2 == chunks_per_half - 1:
                    ready += [functools.partial(emit_out, rows_a, cs) for cs in out_pieces]
            other_work()
        for item in ready:
            item()

    pl.when(fast_ok)(functools.partial(recur_tile, True))
    pl.when(jnp.logical_not(fast_ok))(functools.partial(recur_tile, False))
    for cs in out_pieces:
        emit_out(rows_b, cs)


def _mix0_call(x, mod, ng, win, cw, cb, hglb, gain, wout, *, ts):
    bsz, seq, d = x.shape
    assert ts % (2 * HG_CHUNK) == 0
    const2 = lambda b, s: (0, 0)
    return pl.pallas_call(
        functools.partial(_mix0_kernel, ts=ts),
        grid=(bsz, seq // ts),
        in_specs=[
            pl.BlockSpec((None, ts, d), lambda b, s: (b, s, 0)),
            pl.BlockSpec((None, 6, d), lambda b, s: (b, 0, 0)),
            pl.BlockSpec((1, d), const2),
            pl.BlockSpec((d, IN0_COLS), const2, pipeline_mode=pl.Buffered(1)),
            pl.BlockSpec((3, A_WIDTH), const2),
            pl.BlockSpec((1, A_WIDTH), const2),
            pl.BlockSpec((3, B_WIDTH), const2),
            pl.BlockSpec((1, B_WIDTH), const2),
            pl.BlockSpec((d, d), const2, pipeline_mode=pl.Buffered(1)),
        ],
        out_specs=pl.BlockSpec((None, ts, d), lambda b, s: (b, s, 0)),
        out_shape=jax.ShapeDtypeStruct((bsz, seq, d), F32),
        scratch_shapes=[
            pltpu.VMEM((ts, d), BF16),
            pltpu.VMEM((ts, IN0_COLS), F32),
            pltpu.VMEM((ts, d), BF16),
            pltpu.VMEM((HG_HEADS, HG_DV, HG_DK), F32),
            pltpu.VMEM((SUBLANES, A_WIDTH), F32),
            pltpu.VMEM((ts, B_WIDTH), F32),
            pltpu.VMEM((ts, B_WIDTH), F32),
            pltpu.VMEM((ts, B_WIDTH), F32),
            pltpu.VMEM((HG_CHUNK, B_WIDTH), F32),
        ],
        compiler_params=pltpu.CompilerParams(
            dimension_semantics=("parallel", "arbitrary"),
            vmem_limit_bytes=VMEM_LIMIT_BYTES),
        name="mixer_conv_hgrn2",
    )(x, mod, ng, win, cw, cb, hglb, gain, wout)


def _mix1_kernel(x_ref, mod_ref, ng_ref, win_ref, bin_ref, lng_ref, lnb_ref, ws_ref,
                 bst_ref, wout_ref, o_ref, hb_ref, z_ref, y_ref, *, ts):
    sh1 = mod_ref[0:1, :]
    sc1 = mod_ref[1:2, :]
    g1 = mod_ref[2:3, :]
    gs1 = ng_ref[...] * (1.0 + sc1)
    sub = min(GM_SUB, ts)
    _norm_mod_to(x_ref, hb_ref, gs1, sh1, sub)

    b_in = bin_ref[...]
    ln_g = lng_ref[...]
    ln_b = lnb_ref[...]
    bst = bst_ref[...]
    ti = lax.broadcasted_iota(jnp.int32, (GM_CHUNK, GM_CHUNK), 0)
    si = lax.broadcasted_iota(jnp.int32, (GM_CHUNK, GM_CHUNK), 1)
    w_tri = [jnp.where(si <= ti, ws_ref[g], 0.0).astype(BF16) for g in range(GM_GROUPS)]

    def gelu(t):
        return t * (0.5 * (1.0 + jnp.tanh(_SQRT_2_OVER_PI * (t + 0.044715 * (t * t * t)))))

    def project(rows, cs):
        z_ref[rows, cs] = _dot(hb_ref[rows, :], win_ref[:, cs])

    def emit_out(rows, cs):
        o_ref[rows, cs] = x_ref[rows, cs] + g1[:, cs] * _dot(y_ref[rows, :], wout_ref[:, cs])

    def gate_chunk(ci, mxu_work):
        rows = slice(ci * GM_CHUNK, (ci + 1) * GM_CHUNK)
        u = gelu(z_ref[rows, 0:GM_WIDTH] + b_in[:, 0:GM_WIDTH])
        mxu_work()
        v = gelu(z_ref[rows, GM_WIDTH:2 * GM_WIDTH] + b_in[:, GM_WIDTH:])
        mxu_work()
        mu = jnp.mean(v, axis=-1, keepdims=True)
        vc = v - mu
        inv = lax.rsqrt(jnp.mean(vc * vc, axis=-1, keepdims=True) + EPS)
        vn = ((vc * inv) * ln_g + ln_b).astype(BF16)
        mxu_work()
        for g in range(GM_GROUPS):
            gs = slice(g * GM_GW, (g + 1) * GM_GW)
            mixed = _dot(w_tri[g], vn[:, gs]) + bst[:, g:g + 1]
            y_ref[rows, gs] = (u[:, gs] * mixed).astype(BF16)
        mxu_work()

    n_sub = ts // sub
    chunks_per_sub = sub // GM_CHUNK
    in_pieces = [slice(c, c + PROJ_PIECE) for c in range(0, 2 * GM_WIDTH, PROJ_PIECE)]
    out_pieces = [slice(c, c + PROJ_PIECE) for c in range(0, x_ref.shape[1], PROJ_PIECE)]

    def rows_of(i):
        return slice(i * sub, (i + 1) * sub)

    def side_items(t):
        lists = []
        if t < n_sub:
            lists.append([functools.partial(project, rows_of(t), cs) for cs in in_pieces])
        if t + 1 < n_sub:
            lists.append([functools.partial(_norm_mod_block, x_ref, hb_ref, gs1, sh1, r)
                          for r in range((t + 1) * sub, (t + 2) * sub, ROW_BLOCK)])
        if 0 <= t - 2 < n_sub:
            lists.append([functools.partial(emit_out, rows_of(t - 2), cs) for cs in out_pieces])
        return _interleave(lists)

    for t in range(n_sub + 2):
        items = side_items(t)
        if 0 <= t - 1 < n_sub:
            work = _spread(items, 4 * chunks_per_sub)
            for ci in range((t - 1) * chunks_per_sub, t * chunks_per_sub):
                gate_chunk(ci, work)
        else:
            for item in items:
                item()


def _mix1_call(x, mod, ng, win, b_in, ln_g, ln_b, ws, bst, wout, *, ts):
    bsz, seq, d = x.shape
    const2 = lambda b, s: (0, 0)
    return pl.pallas_call(
        functools.partial(_mix1_kernel, ts=ts),
        grid=(bsz, seq // ts),
        in_specs=[
            pl.BlockSpec((None, ts, d), lambda b, s: (b, s, 0)),
            pl.BlockSpec((None, 6, d), lambda b, s: (b, 0, 0)),
            pl.BlockSpec((1, d), const2),
            pl.BlockSpec((d, 2 * GM_WIDTH), const2, pipeline_mode=pl.Buffered(1)),
            pl.BlockSpec((1, 2 * GM_WIDTH), const2),
            pl.BlockSpec((1, GM_WIDTH), const2),
            pl.BlockSpec((1, GM_WIDTH), const2),
            pl.BlockSpec((GM_GROUPS, GM_CHUNK, GM_CHUNK), lambda b, s: (0, 0, 0)),
            pl.BlockSpec((GM_CHUNK, GM_GROUPS), const2),
            pl.BlockSpec((GM_WIDTH, d), const2, pipeline_mode=pl.Buffered(1)),
        ],
        out_specs=pl.BlockSpec((None, ts, d), lambda b, s: (b, s, 0)),
        out_shape=jax.ShapeDtypeStruct((bsz, seq, d), F32),
        scratch_shapes=[
            pltpu.VMEM((ts, d), BF16),
            pltpu.VMEM((ts, 2 * GM_WIDTH), F32),
            pltpu.VMEM((ts, GM_WIDTH), BF16),
        ],
        compiler_params=pltpu.CompilerParams(
            dimension_semantics=("parallel", "parallel"),
            vmem_limit_bytes=VMEM_LIMIT_BYTES),
        name="mixer_spatial_gating",
    )(x, mod, ng, win, b_in, ln_g, ln_b, ws, bst, wout)


def _ffn_kernel(x_ref, mod_ref, ng_ref, w1_ref, w2_ref, fg_ref, o_ref, hb_ref, *, ts, final):
    sh2 = mod_ref[3:4, :]
    sc2 = mod_ref[4:5, :]
    g2 = mod_ref[5:6, :]
    gs2 = ng_ref[...] * (1.0 + sc2)

    sub = min(FFN_SUB, ts)
    _norm_mod_to(x_ref, hb_ref, gs2, sh2, sub)
    fcols = D_MODEL
    n_f = D_FF // fcols
    for r0 in range(0, ts, sub):
        rows = slice(r0, r0 + sub)
        between = _spread([functools.partial(_norm_mod_block, x_ref, hb_ref, gs2, sh2, r)
                           for r in range(r0 + sub, min(r0 + 2 * sub, ts), ROW_BLOCK)], n_f)
        acc = None
        for j in range(n_f):
            hid = jnp.maximum(_dot(hb_ref[rows, :], w1_ref[:, j * fcols:(j + 1) * fcols]), 0.0)
            between()
            part = _dot((hid * hid).astype(BF16), w2_ref[j * fcols:(j + 1) * fcols, :])
            acc = part if acc is None else acc + part
        out = x_ref[rows, :] + g2 * acc
        if final:
            inv = lax.rsqrt(jnp.mean(out * out, axis=-1, keepdims=True) + EPS)
            out = (out * inv) * fg_ref[...]
        o_ref[rows, :] = out


def _ffn_call(x, mod, ng, w1, w2, fg, *, ts, final):
    bsz, seq, d = x.shape
    const2 = lambda b, s: (0, 0)
    return pl.pallas_call(
        functools.partial(_ffn_kernel, ts=ts, final=final),
        grid=(bsz, seq // ts),
        in_specs=[
            pl.BlockSpec((None, ts, d), lambda b, s: (b, s, 0)),
            pl.BlockSpec((None, 6, d), lambda b, s: (b, 0, 0)),
            pl.BlockSpec((1, d), const2),
            pl.BlockSpec((d, D_FF), const2, pipeline_mode=pl.Buffered(1)),
            pl.BlockSpec((D_FF, d), const2, pipeline_mode=pl.Buffered(1)),
            pl.BlockSpec((1, d), const2),
        ],
        out_specs=pl.BlockSpec((None, ts, d), lambda b, s: (b, s, 0)),
        out_shape=jax.ShapeDtypeStruct((bsz, seq, d), F32),
        scratch_shapes=[pltpu.VMEM((ts, d), BF16)],
        compiler_params=pltpu.CompilerParams(
            dimension_semantics=("parallel", "parallel"),
            vmem_limit_bytes=VMEM_LIMIT_BYTES),
        name="ffn_final" if final else "ffn",
    )(x, mod, ng, w1, w2, fg)


def kernel(x, c, ada_w, ada_b, norm_mix_g, norm_ffn_g, w_in0, conv_w, conv_b, hg_lb, hg_gain,
           w_out0, w_in1, b_in1, gm_ln_g, gm_ln_b, gm_ws, gm_bs, w_out1, w_ff1, w_ff2, final_g):
    bsz, seq, d = x.shape
    ts = min(SEQ_TILE, seq)
    mod = _ada_call(c, ada_w, ada_b).reshape(ada_w.shape[0], bsz, 6, d)
    fg = final_g.reshape(1, d)

    x = _mix0_call(x, mod[0], norm_mix_g[0:1], w_in0[0].astype(BF16), conv_w[0], conv_b[0:1],
                   hg_lb, hg_gain[0:1], w_out0[0].astype(BF16), ts=ts)
    ts_ffn = min(FFN_TILE, seq)
    x = _ffn_call(x, mod[0], norm_ffn_g[0:1], w_ff1[0].astype(BF16), w_ff2[0].astype(BF16), fg,
                  ts=ts_ffn, final=False)
    x = _mix1_call(x, mod[1], norm_mix_g[1:2], w_in1[0].astype(BF16), b_in1[0:1], gm_ln_g[0:1],
                   gm_ln_b[0:1], gm_ws[0], gm_bs[0].T, w_out1[0].astype(BF16),
                   ts=min(GM_TILE, seq))
    x = _ffn_call(x, mod[1], norm_ffn_g[1:2], w_ff1[1].astype(BF16), w_ff2[1].astype(BF16), fg,
                  ts=ts_ffn, final=True)
    return x
```

```python
import functools

import numpy as np
import jax
import jax.numpy as jnp
from jax import lax
from jax.experimental import pallas as pl
from jax.experimental.pallas import tpu as pltpu

F32 = jnp.float32
BF16 = jnp.bfloat16

EPS = 1e-6
D_MODEL = 1024
D_FF = 4 * D_MODEL
A_WIDTH = 512
B_WIDTH = 512
HG_HEADS = 4
HG_DK = 128
HG_DV = 128
IN0_COLS = 3 * A_WIDTH + 4 * B_WIDTH
GM_WIDTH = D_MODEL
GM_GROUPS = 4
GM_CHUNK = 128
GM_GW = GM_WIDTH // GM_GROUPS

SUBLANES = 8
LANES = 128
VMEM_LIMIT_BYTES = 56 * 1024 * 1024

SEQ_TILE = 512
GM_TILE = 1024
GM_SUB = 256
FFN_TILE = 1024
FFN_SUB = 256
ROW_BLOCK = 64
HG_CHUNK = 64
HG_LEVELS = (32, 16, 8, 4, 2, 1)
HG_BLOCK = 16
HG_EXP_GUARD = 80.0
PROJ_PIECE = 256

_SQRT_2_OVER_PI = float(np.sqrt(2.0 / np.pi))


def _sigmoid(v):
    return 1.0 / (1.0 + jnp.exp(-v))


def _dot(a, b):
    return jnp.dot(a, b, preferred_element_type=F32)


def _dot_nt(a, b):
    return lax.dot_general(a, b, (((1,), (1,)), ((), ())), preferred_element_type=F32)


def _dot_tn(a, b):
    return lax.dot_general(a, b, (((0,), (0,)), ((), ())), preferred_element_type=F32)


def _norm_mod_block(x_ref, hb_ref, gs, sh, r):
    xt = x_ref[pl.ds(r, ROW_BLOCK), :]
    inv = lax.rsqrt(jnp.mean(xt * xt, axis=-1, keepdims=True) + EPS)
    hb_ref[pl.ds(r, ROW_BLOCK), :] = ((xt * inv) * gs + sh).astype(BF16)


def _norm_mod_to(x_ref, hb_ref, gs, sh, rows):
    def body(i, carry):
        _norm_mod_block(x_ref, hb_ref, gs, sh, pl.multiple_of(i * ROW_BLOCK, ROW_BLOCK))
        return carry

    lax.fori_loop(0, rows // ROW_BLOCK, body, 0, unroll=True)


def _ada_kernel(c_ref, w_ref, b_ref, o_ref):
    c = c_ref[...]
    s = c * _sigmoid(c)
    o_ref[...] = jnp.dot(s, w_ref[...], preferred_element_type=F32,
                         precision=lax.Precision.HIGHEST) + b_ref[...]


def _ada_call(c, ada_w, ada_b):
    depth, d, e = ada_w.shape
    bsz = c.shape[0]
    tn = 1536
    return pl.pallas_call(
        _ada_kernel,
        grid=(depth, e // tn),
        in_specs=[
            pl.BlockSpec((bsz, d), lambda l, j: (0, 0)),
            pl.BlockSpec((None, d, tn), lambda l, j: (l, 0, j)),
            pl.BlockSpec((None, 1, tn), lambda l, j: (l, 0, j)),
        ],
        out_specs=pl.BlockSpec((None, bsz, tn), lambda l, j: (l, 0, j)),
        out_shape=jax.ShapeDtypeStruct((depth, bsz, e), F32),
        compiler_params=pltpu.CompilerParams(
            dimension_semantics=("arbitrary", "arbitrary"),
            vmem_limit_bytes=VMEM_LIMIT_BYTES),
        name="ada_modulation",
    )(c, ada_w, ada_b.reshape(depth, 1, e))


def _level_masks():
    ti = lax.broadcasted_iota(jnp.int32, (HG_CHUNK, HG_CHUNK), 0)
    si = lax.broadcasted_iota(jnp.int32, (HG_CHUNK, HG_CHUNK), 1)
    masks = []
    for m in HG_LEVELS:
        other_blk = (ti ^ si) & ~(2 * m - 1)
        masks.append((other_blk | ((ti & m) ^ m) | (si & m)) == 0)
    blk_mask = (((ti ^ si) & ~(HG_BLOCK - 1)) | jnp.maximum(si - ti, 0)) == 0
    return masks, ti == si, blk_mask


def _block_start(c):
    rows, w = c.shape
    parts = [jnp.zeros((HG_BLOCK, w), c.dtype)]
    for b in range(1, rows // HG_BLOCK):
        parts.append(jnp.broadcast_to(c[b * HG_BLOCK - 1:b * HG_BLOCK, :], (HG_BLOCK, w)))
    return jnp.concatenate(parts, axis=0)


def _block_boundary(c, m):
    rows, w = c.shape
    if m >= SUBLANES:
        blk = 2 * m
        return jnp.concatenate(
            [jnp.broadcast_to(c[b * blk + m - 1:b * blk + m, :], (blk, w))
             for b in range(rows // blk)], axis=0)
    c3 = c.reshape(rows // SUBLANES, SUBLANES, w)
    if m == 4:
        bnd = jnp.broadcast_to(c3[:, 3:4, :], c3.shape)
    else:
        assert m == 2
        sub = lax.broadcasted_iota(jnp.int32, c3.shape, 1)
        bnd = jnp.where(sub < 4, jnp.broadcast_to(c3[:, 1:2, :], c3.shape),
                        jnp.broadcast_to(c3[:, 5:6, :], c3.shape))
    return bnd.reshape(rows, w)


def _chunk_cumsum(x):
    rows, w = x.shape
    nv = rows // SUBLANES
    x3 = x.reshape(nv, SUBLANES, w)
    sub = lax.broadcasted_iota(jnp.int32, x3.shape, 1)
    sh = 1
    while sh < SUBLANES:
        x3 = x3 + jnp.where(sub >= sh, pltpu.roll(x3, sh, axis=1), 0.0)
        sh *= 2
    tot = jnp.broadcast_to(x3[:, SUBLANES - 1:SUBLANES, :], x3.shape)
    outs = [x3[0]]
    acc = tot[0]
    for j in range(1, nv):
        outs.append(x3[j] + acc)
        if j + 1 < nv:
            acc = acc + tot[j]
    return jnp.concatenate(outs, axis=0)


def _spread(items, slots):
    state = {"calls": 0, "done": 0}

    def run_due():
        state["calls"] += 1
        want = min(len(items), -(-len(items) * state["calls"] // slots))
        while state["done"] < want:
            items[state["done"]]()
            state["done"] += 1
    return run_due


def _interleave(lists):
    keyed = [((i + 0.5) / len(l), n, item) for n, l in enumerate(lists) for i, item in enumerate(l)]
    return [item for _, _, item in sorted(keyed, key=lambda k: k[:2])]


def _mix0_kernel(x_ref, mod_ref, ng_ref, win_ref, cw_ref, cb_ref, hglb_ref, gain_ref, wout_ref,
                 o_ref, hb_ref, p_ref, y_ref, st_ref, zc_ref, cum_ref, ck_ref, f_ref, worst_ref,
                 *, ts):
    @pl.when(pl.program_id(1) == 0)
    def _():
        st_ref[...] = jnp.zeros_like(st_ref)
        zc_ref[...] = jnp.zeros_like(zc_ref)

    half = ts // 2
    rows_a, rows_b = slice(0, half), slice(half, ts)
    chunks_per_half = half // HG_CHUNK
    sh1 = mod_ref[0:1, :]
    gs1 = ng_ref[...] * (1.0 + mod_ref[1:2, :])
    g1 = mod_ref[2:3, :]
    _norm_mod_to(x_ref, hb_ref, gs1, sh1, half)

    a0, a1, a2 = hglb_ref[0:1, :], hglb_ref[1:2, :], hglb_ref[2:3, :]
    amax = jnp.maximum(jnp.maximum(a0, a1), a2)
    e0, e1, e2 = jnp.exp(a0 - amax), jnp.exp(a1 - amax), jnp.exp(a2 - amax)
    lb = e0 / (e0 + e1 + e2)

    cw0, cw1, cw2 = cw_ref[0:1, :], cw_ref[1:2, :], cw_ref[2:3, :]
    cb = cb_ref[...]
    gain = gain_ref[...]
    row8 = lax.broadcasted_iota(jnp.int32, (SUBLANES, A_WIDTH), 0)
    masks, diag, blk_mask = _level_masks()

    in_pieces = [slice(c, c + PROJ_PIECE) for c in range(0, IN0_COLS, PROJ_PIECE)]
    out_pieces = [slice(c, c + PROJ_PIECE) for c in range(0, D_MODEL, PROJ_PIECE)]

    def project(rows, cs):
        p_ref[rows, cs] = _dot(hb_ref[rows, :], win_ref[:, cs])

    def emit_out(rows, cs):
        o_ref[rows, cs] = x_ref[rows, cs] + g1[:, cs] * _dot(y_ref[rows, :], wout_ref[:, cs])

    def gates_chunk(ci, other_work):
        rows = slice(ci * HG_CHUNK, (ci + 1) * HG_CHUNK)

        def col(j):
            return p_ref[rows, j * 512:(j + 1) * 512]

        z = col(1) * col(2)
        prev = zc_ref[...]
        zc_ref[...] = z[HG_CHUNK - SUBLANES:, :]

        def shifted(k):
            zr = pltpu.roll(z, k, axis=0)
            pr = pltpu.roll(prev, k, axis=0)
            head = jnp.where(row8 < k, pr, zr[:SUBLANES])
            return jnp.concatenate([head, zr[SUBLANES:]], axis=0)

        conv = cb + shifted(2) * cw0
        conv = conv + shifted(1) * cw1
        conv = conv + z * cw2
        y_ref[rows, 0:A_WIDTH] = (col(0) * conv).astype(BF16)
        other_work()

        f = lb + (1.0 - lb) * _sigmoid(col(4))
        cum = _chunk_cumsum(jnp.log(f))
        ck = cum - jnp.log(1.0 - f)
        f_ref[rows, :] = f
        cum_ref[rows, :] = cum
        ck_ref[rows, :] = ck
        e = _block_start(cum) - ck
        worst_ref[...] = e if ci == 0 else jnp.maximum(worst_ref[...], e)
        other_work()


    def scores_stage(ci, fast):
        rows = slice(ci * HG_CHUNK, (ci + 1) * HG_CHUNK)
        q = p_ref[rows, 3 * 512:4 * 512]
        v = p_ref[rows, 5 * 512:6 * 512]
        cum = cum_ref[rows, :]
        ck = ck_ref[rows, :]
        last = cum[HG_CHUNK - 1:HG_CHUNK, :]
        q_in = q * jnp.exp(cum)
        k_dec = jnp.exp(last - ck)
        if not fast:
            f = f_ref[rows, :]
            kk = 1.0 - f
            qf = q * f

        heads = []
        for h in range(HG_HEADS):
            hs = slice(h * HG_DK, (h + 1) * HG_DK)
            q_h, cum_h, ck_h = q[:, hs], cum[:, hs], ck[:, hs]
            if fast:
                bnd = _block_start(cum_h)
                qt = (q_h * jnp.exp(cum_h - bnd)).astype(BF16)
                kt = jnp.exp(bnd - ck_h).astype(BF16)
                parts = [(blk_mask, _dot_nt(qt, kt))]
                levels = [(m, mask) for m, mask in zip(HG_LEVELS, masks) if m >= HG_BLOCK]
            else:
                k_b = kk[:, hs].astype(BF16)
                parts = [(diag, _dot_nt(q_h.astype(BF16), k_b))]
                levels = list(zip(HG_LEVELS, masks))
            for m, mask in levels:
                if m == 1:
                    qt, kt = qf[:, hs].astype(BF16), k_b
                else:
                    bnd = _block_boundary(cum_h, m)
                    qt = (q_h * jnp.exp(cum_h - bnd)).astype(BF16)
                    kt = jnp.exp(bnd - ck_h).astype(BF16)
                parts.append((mask, _dot_nt(qt, kt)))
            heads.append(dict(parts=parts, q_in=q_in[:, hs].astype(BF16),
                              k_dec=k_dec[:, hs].astype(BF16), v=v[:, hs].astype(BF16)))
        return dict(rows=rows, heads=heads, dec=jnp.exp(last))

    def state_stage(ctx):
        for h, hd in enumerate(ctx["heads"]):
            hs = slice(h * HG_DK, (h + 1) * HG_DK)
            scores = None
            for mask, s in hd["parts"]:
                scores = jnp.where(mask, s, 0.0 if scores is None else scores)
            st = st_ref[h]
            hd["o"] = _dot_nt(hd["q_in"], st.astype(BF16)) + _dot(scores.astype(BF16), hd["v"])
            st_ref[h] = st * ctx["dec"][:, hs] + _dot_tn(hd["v"], hd["k_dec"])

    def output_stage(ctx):
        rows = ctx["rows"]
        gate = p_ref[rows, 6 * 512:7 * 512]
        gsil = gate * _sigmoid(gate)
        for h, hd in enumerate(ctx["heads"]):
            hs = slice(h * HG_DK, (h + 1) * HG_DK)
            o = hd["o"]
            o = o * lax.rsqrt(jnp.mean(o * o, axis=-1, keepdims=True) + EPS)
            y_ref[rows, A_WIDTH + h * HG_DV:A_WIDTH + (h + 1) * HG_DV] = (
                o * gain[:, hs] * gsil[:, hs]).astype(BF16)

    chunks_a = range(chunks_per_half)
    chunks_b = range(chunks_per_half, 2 * chunks_per_half)

    def is_gate_col(cs):
        return cs.start < 3 * A_WIDTH or 3 * A_WIDTH + B_WIDTH <= cs.start < 3 * A_WIDTH + 2 * B_WIDTH

    gate_pieces = [cs for cs in in_pieces if is_gate_col(cs)]
    rest_pieces = [cs for cs in in_pieces if not is_gate_col(cs)]

    norm_b = _spread([functools.partial(_norm_mod_block, x_ref, hb_ref, gs1, sh1, r)
                      for r in range(half, ts, ROW_BLOCK)], len(gate_pieces))
    for cs in gate_pieces:
        project(rows_a, cs)
        norm_b()

    work = _spread([functools.partial(project, rows_b, cs) for cs in gate_pieces],
                   2 * chunks_per_half)
    for ci in chunks_a:
        gates_chunk(ci, work)

    work = _spread([functools.partial(project, rows, cs)
                    for rows in (rows_a, rows_b) for cs in rest_pieces], 2 * chunks_per_half)
    for ci in chunks_b:
        gates_chunk(ci, work)
    fast_ok = jnp.max(worst_ref[...]) < HG_EXP_GUARD

    def recur_tile(fast):
        n = 2 * chunks_per_half
        ready = []
        ctx = {}

        def other_work():
            if ready:
                ready.pop(0)()

        for t in range(n + 2):
            if t < n:
                ctx[t] = scores_stage(t, fast)
            other_work()
            if 0 <= t - 1 < n:
                state_stage(ctx[t - 1])
            other_work()
            if 0 <= t - 2 < n:
                output_stage(ctx.pop(t - 2))
                if t - 2 == chunks_per_half - 1:
                    ready += [functools.partial(emit_out, rows_a, cs) for cs in out_pieces]
            other_work()
        for item in ready:
            item()

    pl.when(fast_ok)(functools.partial(recur_tile, True))
    pl.when(jnp.logical_not(fast_ok))(functools.partial(recur_tile, False))
    for cs in out_pieces:
        emit_out(rows_b, cs)


def _mix0_call(x, mod, ng, win, cw, cb, hglb, gain, wout, *, ts):
    bsz, seq, d = x.shape
    assert ts % (2 * HG_CHUNK) == 0
    const2 = lambda b, s: (0, 0)
    return pl.pallas_call(
        functools.partial(_mix0_kernel, ts=ts),
        grid=(bsz, seq // ts),
        in_specs=[
            pl.BlockSpec((None, ts, d), lambda b, s: (b, s, 0)),
            pl.BlockSpec((None, 6, d), lambda b, s: (b, 0, 0)),
            pl.BlockSpec((1, d), const2),
            pl.BlockSpec((d, IN0_COLS), const2, pipeline_mode=pl.Buffered(1)),
            pl.BlockSpec((3, A_WIDTH), const2),
            pl.BlockSpec((1, A_WIDTH), const2),
            pl.BlockSpec((3, B_WIDTH), const2),
            pl.BlockSpec((1, B_WIDTH), const2),
            pl.BlockSpec((d, d), const2, pipeline_mode=pl.Buffered(1)),
        ],
        out_specs=pl.BlockSpec((None, ts, d), lambda b, s: (b, s, 0)),
        out_shape=jax.ShapeDtypeStruct((bsz, seq, d), F32),
        scratch_shapes=[
            pltpu.VMEM((ts, d), BF16),
            pltpu.VMEM((ts, IN0_COLS), F32),
            pltpu.VMEM((ts, d), BF16),
            pltpu.VMEM((HG_HEADS, HG_DV, HG_DK), F32),
            pltpu.VMEM((SUBLANES, A_WIDTH), F32),
            pltpu.VMEM((ts, B_WIDTH), F32),
            pltpu.VMEM((ts, B_WIDTH), F32),
            pltpu.VMEM((ts, B_WIDTH), F32),
            pltpu.VMEM((HG_CHUNK, B_WIDTH), F32),
        ],
        compiler_params=pltpu.CompilerParams(
            dimension_semantics=("parallel", "arbitrary"),
            vmem_limit_bytes=VMEM_LIMIT_BYTES),
        name="mixer_conv_hgrn2",
    )(x, mod, ng, win, cw, cb, hglb, gain, wout)


def _mix1_kernel(x_ref, mod_ref, ng_ref, win_ref, bin_ref, lng_ref, lnb_ref, ws_ref,
                 bst_ref, wout_ref, o_ref, hb_ref, z_ref, y_ref, v_ref, *, ts):
    sh1 = mod_ref[0:1, :]
    sc1 = mod_ref[1:2, :]
    g1 = mod_ref[2:3, :]
    gs1 = ng_ref[...] * (1.0 + sc1)
    sub = min(GM_SUB, ts)
    _norm_mod_to(x_ref, hb_ref, gs1, sh1, sub)

    b_in = bin_ref[...]
    ln_g = lng_ref[...]
    ln_b = lnb_ref[...]
    bst = bst_ref[...]
    ti = lax.broadcasted_iota(jnp.int32, (GM_CHUNK, GM_CHUNK), 0)
    si = lax.broadcasted_iota(jnp.int32, (GM_CHUNK, GM_CHUNK), 1)
    w_tri = [jnp.where(si <= ti, ws_ref[g], 0.0).astype(BF16) for g in range(GM_GROUPS)]

    def gelu(t):
        half_t = 0.5 * t
        th = jnp.tanh(t * (_SQRT_2_OVER_PI + (_SQRT_2_OVER_PI * 0.044715) * (t * t)))
        return half_t + half_t * th

    def project(rows, cs):
        z_ref[rows, cs] = _dot(hb_ref[rows, :], win_ref[:, cs])

    def emit_out(rows, cs):
        o_ref[rows, cs] = x_ref[rows, cs] + g1[:, cs] * _dot(y_ref[rows, :], wout_ref[:, cs])

    def gate_chunk(ci, mxu_work):
        rows = slice(ci * GM_CHUNK, (ci + 1) * GM_CHUNK)
        groups = [slice(g * GM_GW, (g + 1) * GM_GW) for g in range(GM_GROUPS)]
        total = None
        for gs in groups:
            vs = slice(GM_WIDTH + gs.start, GM_WIDTH + gs.stop)
            v = gelu(z_ref[rows, vs] + b_in[:, vs])
            v_ref[rows, gs] = v
            part = jnp.sum(v, axis=-1, keepdims=True)
            total = part if total is None else total + part
            mxu_work()
        mu = total * (1.0 / GM_WIDTH)
        total = None
        for gs in groups:
            vc = v_ref[rows, gs] - mu
            v_ref[rows, gs] = vc
            part = jnp.sum(vc * vc, axis=-1, keepdims=True)
            total = part if total is None else total + part
        inv = lax.rsqrt(total * (1.0 / GM_WIDTH) + EPS)
        for g, gs in enumerate(groups):
            vn = ((v_ref[rows, gs] * inv) * ln_g[:, gs] + ln_b[:, gs]).astype(BF16)
            mixed = _dot(w_tri[g], vn) + bst[:, g:g + 1]
            u = gelu(z_ref[rows, gs] + b_in[:, gs])
            y_ref[rows, gs] = (u * mixed).astype(BF16)
            mxu_work()

    n_sub = ts // sub
    chunks_per_sub = sub // GM_CHUNK
    in_pieces = [slice(c, c + PROJ_PIECE) for c in range(0, 2 * GM_WIDTH, PROJ_PIECE)]
    out_pieces = [slice(c, c + PROJ_PIECE) for c in range(0, x_ref.shape[1], PROJ_PIECE)]

    def rows_of(i):
        return slice(i * sub, (i + 1) * sub)

    def side_items(t):
        lists = []
        if t < n_sub:
            lists.append([functools.partial(project, rows_of(t), cs) for cs in in_pieces])
        if t + 1 < n_sub:
            lists.append([functools.partial(_norm_mod_block, x_ref, hb_ref, gs1, sh1, r)
                          for r in range((t + 1) * sub, (t + 2) * sub, ROW_BLOCK)])
        if 0 <= t - 2 < n_sub:
            lists.append([functools.partial(emit_out, rows_of(t - 2), cs) for cs in out_pieces])
        return _interleave(lists)

    for t in range(n_sub + 2):
        items = side_items(t)
        if 0 <= t - 1 < n_sub:
            work = _spread(items, 2 * GM_GROUPS * chunks_per_sub)
            for ci in range((t - 1) * chunks_per_sub, t * chunks_per_sub):
                gate_chunk(ci, work)
        else:
            for item in items:
                item()


def _mix1_call(x, mod, ng, win, b_in, ln_g, ln_b, ws, bst, wout, *, ts):
    bsz, seq, d = x.shape
    const2 = lambda b, s: (0, 0)
    return pl.pallas_call(
        functools.partial(_mix1_kernel, ts=ts),
        grid=(bsz, seq // ts),
        in_specs=[
            pl.BlockSpec((None, ts, d), lambda b, s: (b, s, 0)),
            pl.BlockSpec((None, 6, d), lambda b, s: (b, 0, 0)),
            pl.BlockSpec((1, d), const2),
            pl.BlockSpec((d, 2 * GM_WIDTH), const2, pipeline_mode=pl.Buffered(1)),
            pl.BlockSpec((1, 2 * GM_WIDTH), const2),
            pl.BlockSpec((1, GM_WIDTH), const2),
            pl.BlockSpec((1, GM_WIDTH), const2),
            pl.BlockSpec((GM_GROUPS, GM_CHUNK, GM_CHUNK), lambda b, s: (0, 0, 0)),
            pl.BlockSpec((GM_CHUNK, GM_GROUPS), const2),
            pl.BlockSpec((GM_WIDTH, d), const2, pipeline_mode=pl.Buffered(1)),
        ],
        out_specs=pl.BlockSpec((None, ts, d), lambda b, s: (b, s, 0)),
        out_shape=jax.ShapeDtypeStruct((bsz, seq, d), F32),
        scratch_shapes=[
            pltpu.VMEM((ts, d), BF16),
            pltpu.VMEM((ts, 2 * GM_WIDTH), F32),
            pltpu.VMEM((ts, GM_WIDTH), BF16),
            pltpu.VMEM((ts, GM_WIDTH), F32),
        ],
        compiler_params=pltpu.CompilerParams(
            dimension_semantics=("parallel", "parallel"),
            vmem_limit_bytes=VMEM_LIMIT_BYTES),
        name="mixer_spatial_gating",
    )(x, mod, ng, win, b_in, ln_g, ln_b, ws, bst, wout)


def _ffn_kernel(x_ref, mod_ref, ng_ref, w1_ref, w2_ref, fg_ref, o_ref, hb_ref, *, ts, final):
    sh2 = mod_ref[3:4, :]
    sc2 = mod_ref[4:5, :]
    g2 = mod_ref[5:6, :]
    gs2 = ng_ref[...] * (1.0 + sc2)

    sub = min(FFN_SUB, ts)
    _norm_mod_to(x_ref, hb_ref, gs2, sh2, sub)
    fcols = D_MODEL
    n_f = D_FF // fcols
    for r0 in range(0, ts, sub):
        rows = slice(r0, r0 + sub)
        between = _spread([functools.partial(_norm_mod_block, x_ref, hb_ref, gs2, sh2, r)
                           for r in range(r0 + sub, min(r0 + 2 * sub, ts), ROW_BLOCK)], n_f)
        acc = None
        for j in range(n_f):
            hid = jnp.maximum(_dot(hb_ref[rows, :], w1_ref[:, j * fcols:(j + 1) * fcols]), 0.0)
            between()
            part = _dot((hid * hid).astype(BF16), w2_ref[j * fcols:(j + 1) * fcols, :])
            acc = part if acc is None else acc + part
        out = x_ref[rows, :] + g2 * acc
        if final:
            inv = lax.rsqrt(jnp.mean(out * out, axis=-1, keepdims=True) + EPS)
            out = (out * inv) * fg_ref[...]
        o_ref[rows, :] = out


def _ffn_call(x, mod, ng, w1, w2, fg, *, ts, final):
    bsz, seq, d = x.shape
    const2 = lambda b, s: (0, 0)
    return pl.pallas_call(
        functools.partial(_ffn_kernel, ts=ts, final=final),
        grid=(bsz, seq // ts),
        in_specs=[
            pl.BlockSpec((None, ts, d), lambda b, s: (b, s, 0)),
            pl.BlockSpec((None, 6, d), lambda b, s: (b, 0, 0)),
            pl.BlockSpec((1, d), const2),
            pl.BlockSpec((d, D_FF), const2, pipeline_mode=pl.Buffered(1)),
            pl.BlockSpec((D_FF, d), const2, pipeline_mode=pl.Buffered(1)),
            pl.BlockSpec((1, d), const2),
        ],
        out_specs=pl.BlockSpec((None, ts, d), lambda b, s: (b, s, 0)),
        out_shape=jax.ShapeDtypeStruct((bsz, seq, d), F32),
        scratch_shapes=[pltpu.VMEM((ts, d), BF16)],
        compiler_params=pltpu.CompilerParams(
            dimension_semantics=("parallel", "parallel"),
            vmem_limit_bytes=VMEM_LIMIT_BYTES),
        name="ffn_final" if final else "ffn",
    )(x, mod, ng, w1, w2, fg)


def kernel(x, c, ada_w, ada_b, norm_mix_g, norm_ffn_g, w_in0, conv_w, conv_b, hg_lb, hg_gain,
           w_out0, w_in1, b_in1, gm_ln_g, gm_ln_b, gm_ws, gm_bs, w_out1, w_ff1, w_ff2, final_g):
    bsz, seq, d = x.shape
    ts = min(SEQ_TILE, seq)
    mod = _ada_call(c, ada_w, ada_b).reshape(ada_w.shape[0], bsz, 6, d)
    fg = final_g.reshape(1, d)

    x = _mix0_call(x, mod[0], norm_mix_g[0:1], w_in0[0].astype(BF16), conv_w[0], conv_b[0:1],
                   hg_lb, hg_gain[0:1], w_out0[0].astype(BF16), ts=ts)
    ts_ffn = min(FFN_TILE, seq)
    x = _ffn_call(x, mod[0], norm_ffn_g[0:1], w_ff1[0].astype(BF16), w_ff2[0].astype(BF16), fg,
                  ts=ts_ffn, final=False)
    x = _mix1_call(x, mod[1], norm_mix_g[1:2], w_in1[0].astype(BF16), b_in1[0:1], gm_ln_g[0:1],
                   gm_ln_b[0:1], gm_ws[0], gm_bs[0].T, w_out1[0].astype(BF16),
                   ts=min(GM_TILE, seq))
    x = _ffn_call(x, mod[1], norm_ffn_g[1:2], w_ff1[1].astype(BF16), w_ff2[1].astype(BF16), fg,
                  ts=ts_ffn, final=True)
    return x
```

```python
import functools

import numpy as np
import jax
import jax.numpy as jnp
from jax import lax
from jax.experimental import pallas as pl
from jax.experimental.pallas import tpu as pltpu

F32 = jnp.float32
BF16 = jnp.bfloat16

EPS = 1e-6
D_MODEL = 1024
D_FF = 4 * D_MODEL
A_WIDTH = 512
B_WIDTH = 512
HG_HEADS = 4
HG_DK = 128
HG_DV = 128
IN0_COLS = 3 * A_WIDTH + 4 * B_WIDTH
GM_WIDTH = D_MODEL
GM_GROUPS = 4
GM_CHUNK = 128
GM_GW = GM_WIDTH // GM_GROUPS

SUBLANES = 8
LANES = 128
VMEM_LIMIT_BYTES = 56 * 1024 * 1024

SEQ_TILE = 512
GM_TILE = 1024
GM_SUB = 256
FFN_TILE = 1024
FFN_SUB = 256
ROW_BLOCK = 64
HG_CHUNK = 64
HG_LEVELS = (32, 16, 8, 4, 2, 1)
HG_BLOCK = 16
HG_EXP_GUARD = 80.0
PROJ_PIECE = 256

_SQRT_2_OVER_PI = float(np.sqrt(2.0 / np.pi))


def _sigmoid(v):
    return 1.0 / (1.0 + jnp.exp(-v))


def _dot(a, b):
    return jnp.dot(a, b, preferred_element_type=F32)


def _dot_nt(a, b):
    return lax.dot_general(a, b, (((1,), (1,)), ((), ())), preferred_element_type=F32)


def _dot_tn(a, b):
    return lax.dot_general(a, b, (((0,), (0,)), ((), ())), preferred_element_type=F32)


def _norm_mod_block(x_ref, hb_ref, gs, sh, r):
    xt = x_ref[pl.ds(r, ROW_BLOCK), :]
    inv = lax.rsqrt(jnp.mean(xt * xt, axis=-1, keepdims=True) + EPS)
    hb_ref[pl.ds(r, ROW_BLOCK), :] = ((xt * inv) * gs + sh).astype(BF16)


def _norm_mod_to(x_ref, hb_ref, gs, sh, rows):
    def body(i, carry):
        _norm_mod_block(x_ref, hb_ref, gs, sh, pl.multiple_of(i * ROW_BLOCK, ROW_BLOCK))
        return carry

    lax.fori_loop(0, rows // ROW_BLOCK, body, 0, unroll=True)


def _split_bf16(a):
    hi = a.astype(BF16)
    return hi, (a - hi.astype(F32)).astype(BF16)


def _ada_kernel(c_ref, w_ref, b_ref, o_ref):
    c = c_ref[...]
    bsz = c.shape[0]
    s_hi, s_lo = _split_bf16(c * _sigmoid(c))
    w_hi, w_lo = _split_bf16(w_ref[...])
    both = _dot(jnp.concatenate([s_hi, s_lo], axis=0), w_hi)
    o_ref[...] = both[:bsz] + both[bsz:] + _dot(s_hi, w_lo) + b_ref[...]


def _ada_call(c, ada_w, ada_b):
    depth, d, e = ada_w.shape
    bsz = c.shape[0]
    tn = 1536
    return pl.pallas_call(
        _ada_kernel,
        grid=(depth, e // tn),
        in_specs=[
            pl.BlockSpec((bsz, d), lambda l, j: (0, 0)),
            pl.BlockSpec((None, d, tn), lambda l, j: (l, 0, j)),
            pl.BlockSpec((None, 1, tn), lambda l, j: (l, 0, j)),
        ],
        out_specs=pl.BlockSpec((None, bsz, tn), lambda l, j: (l, 0, j)),
        out_shape=jax.ShapeDtypeStruct((depth, bsz, e), F32),
        compiler_params=pltpu.CompilerParams(
            dimension_semantics=("arbitrary", "arbitrary"),
            vmem_limit_bytes=VMEM_LIMIT_BYTES),
        name="ada_modulation",
    )(c, ada_w, ada_b.reshape(depth, 1, e))


def _level_masks():
    ti = lax.broadcasted_iota(jnp.int32, (HG_CHUNK, HG_CHUNK), 0)
    si = lax.broadcasted_iota(jnp.int32, (HG_CHUNK, HG_CHUNK), 1)
    masks = []
    for m in HG_LEVELS:
        other_blk = (ti ^ si) & ~(2 * m - 1)
        masks.append((other_blk | ((ti & m) ^ m) | (si & m)) == 0)
    blk_mask = (((ti ^ si) & ~(HG_BLOCK - 1)) | jnp.maximum(si - ti, 0)) == 0
    return masks, ti == si, blk_mask


def _block_start(c):
    rows, w = c.shape
    parts = [jnp.zeros((HG_BLOCK, w), c.dtype)]
    for b in range(1, rows // HG_BLOCK):
        parts.append(jnp.broadcast_to(c[b * HG_BLOCK - 1:b * HG_BLOCK, :], (HG_BLOCK, w)))
    return jnp.concatenate(parts, axis=0)


def _block_boundary(c, m):
    rows, w = c.shape
    if m >= SUBLANES:
        blk = 2 * m
        return jnp.concatenate(
            [jnp.broadcast_to(c[b * blk + m - 1:b * blk + m, :], (blk, w))
             for b in range(rows // blk)], axis=0)
    c3 = c.reshape(rows // SUBLANES, SUBLANES, w)
    if m == 4:
        bnd = jnp.broadcast_to(c3[:, 3:4, :], c3.shape)
    else:
        assert m == 2
        sub = lax.broadcasted_iota(jnp.int32, c3.shape, 1)
        bnd = jnp.where(sub < 4, jnp.broadcast_to(c3[:, 1:2, :], c3.shape),
                        jnp.broadcast_to(c3[:, 5:6, :], c3.shape))
    return bnd.reshape(rows, w)


def _chunk_cumsum(x):
    rows, w = x.shape
    nv = rows // SUBLANES
    x3 = x.reshape(nv, SUBLANES, w)
    sub = lax.broadcasted_iota(jnp.int32, x3.shape, 1)
    sh = 1
    while sh < SUBLANES:
        x3 = x3 + jnp.where(sub >= sh, pltpu.roll(x3, sh, axis=1), 0.0)
        sh *= 2
    tot = jnp.broadcast_to(x3[:, SUBLANES - 1:SUBLANES, :], x3.shape)
    outs = [x3[0]]
    acc = tot[0]
    for j in range(1, nv):
        outs.append(x3[j] + acc)
        if j + 1 < nv:
            acc = acc + tot[j]
    return jnp.concatenate(outs, axis=0)


def _spread(items, slots):
    state = {"calls": 0, "done": 0}

    def run_due():
        state["calls"] += 1
        want = min(len(items), -(-len(items) * state["calls"] // slots))
        while state["done"] < want:
            items[state["done"]]()
            state["done"] += 1
    return run_due


def _interleave(lists):
    keyed = [((i + 0.5) / len(l), n, item) for n, l in enumerate(lists) for i, item in enumerate(l)]
    return [item for _, _, item in sorted(keyed, key=lambda k: k[:2])]


def _mix0_kernel(x_ref, mod_ref, ng_ref, win_ref, cw_ref, cb_ref, hglb_ref, gain_ref, wout_ref,
                 o_ref, hb_ref, p_ref, y_ref, st_ref, zc_ref, cum_ref, ck_ref, f_ref, worst_ref,
                 *, ts):
    @pl.when(pl.program_id(1) == 0)
    def _():
        st_ref[...] = jnp.zeros_like(st_ref)
        zc_ref[...] = jnp.zeros_like(zc_ref)

    half = ts // 2
    rows_a, rows_b = slice(0, half), slice(half, ts)
    chunks_per_half = half // HG_CHUNK
    sh1 = mod_ref[0:1, :]
    gs1 = ng_ref[...] * (1.0 + mod_ref[1:2, :])
    g1 = mod_ref[2:3, :]
    _norm_mod_to(x_ref, hb_ref, gs1, sh1, half)

    a0, a1, a2 = hglb_ref[0:1, :], hglb_ref[1:2, :], hglb_ref[2:3, :]
    amax = jnp.maximum(jnp.maximum(a0, a1), a2)
    e0, e1, e2 = jnp.exp(a0 - amax), jnp.exp(a1 - amax), jnp.exp(a2 - amax)
    lb = e0 / (e0 + e1 + e2)

    cw0, cw1, cw2 = cw_ref[0:1, :], cw_ref[1:2, :], cw_ref[2:3, :]
    cb = cb_ref[...]
    gain = gain_ref[...]
    row8 = lax.broadcasted_iota(jnp.int32, (SUBLANES, A_WIDTH), 0)
    masks, diag, blk_mask = _level_masks()

    in_pieces = [slice(c, c + PROJ_PIECE) for c in range(0, IN0_COLS, PROJ_PIECE)]
    out_pieces = [slice(c, c + PROJ_PIECE) for c in range(0, D_MODEL, PROJ_PIECE)]

    def project(rows, cs):
        p_ref[rows, cs] = _dot(hb_ref[rows, :], win_ref[:, cs])

    def emit_out(rows, cs):
        o_ref[rows, cs] = x_ref[rows, cs] + g1[:, cs] * _dot(y_ref[rows, :], wout_ref[:, cs])

    def gates_chunk(ci, other_work):
        rows = slice(ci * HG_CHUNK, (ci + 1) * HG_CHUNK)

        def col(j):
            return p_ref[rows, j * 512:(j + 1) * 512]

        z = col(1) * col(2)
        prev = zc_ref[...]
        zc_ref[...] = z[HG_CHUNK - SUBLANES:, :]

        def shifted(k):
            zr = pltpu.roll(z, k, axis=0)
            pr = pltpu.roll(prev, k, axis=0)
            head = jnp.where(row8 < k, pr, zr[:SUBLANES])
            return jnp.concatenate([head, zr[SUBLANES:]], axis=0)

        conv = cb + shifted(2) * cw0
        conv = conv + shifted(1) * cw1
        conv = conv + z * cw2
        y_ref[rows, 0:A_WIDTH] = (col(0) * conv).astype(BF16)
        other_work()

        f = lb + (1.0 - lb) * _sigmoid(col(4))
        cum = _chunk_cumsum(jnp.log(f))
        ck = cum - jnp.log(1.0 - f)
        f_ref[rows, :] = f
        cum_ref[rows, :] = cum
        ck_ref[rows, :] = ck
        e = _block_start(cum) - ck
        worst_ref[...] = e if ci == 0 else jnp.maximum(worst_ref[...], e)
        other_work()


    def scores_stage(ci, fast):
        rows = slice(ci * HG_CHUNK, (ci + 1) * HG_CHUNK)
        q = p_ref[rows, 3 * 512:4 * 512]
        v = p_ref[rows, 5 * 512:6 * 512]
        cum = cum_ref[rows, :]
        ck = ck_ref[rows, :]
        last = cum[HG_CHUNK - 1:HG_CHUNK, :]
        q_in = q * jnp.exp(cum)
        k_dec = jnp.exp(last - ck)
        if not fast:
            f = f_ref[rows, :]
            kk = 1.0 - f
            qf = q * f

        heads = []
        for h in range(HG_HEADS):
            hs = slice(h * HG_DK, (h + 1) * HG_DK)
            q_h, cum_h, ck_h = q[:, hs], cum[:, hs], ck[:, hs]
            if fast:
                bnd = _block_start(cum_h)
                qt = (q_h * jnp.exp(cum_h - bnd)).astype(BF16)
                kt = jnp.exp(bnd - ck_h).astype(BF16)
                parts = [(blk_mask, _dot_nt(qt, kt))]
                levels = [(m, mask) for m, mask in zip(HG_LEVELS, masks) if m >= HG_BLOCK]
            else:
                k_b = kk[:, hs].astype(BF16)
                parts = [(diag, _dot_nt(q_h.astype(BF16), k_b))]
                levels = list(zip(HG_LEVELS, masks))
            for m, mask in levels:
                if m == 1:
                    qt, kt = qf[:, hs].astype(BF16), k_b
                else:
                    bnd = _block_boundary(cum_h, m)
                    qt = (q_h * jnp.exp(cum_h - bnd)).astype(BF16)
                    kt = jnp.exp(bnd - ck_h).astype(BF16)
                parts.append((mask, _dot_nt(qt, kt)))
            heads.append(dict(parts=parts, q_in=q_in[:, hs].astype(BF16),
                              k_dec=k_dec[:, hs].astype(BF16), v=v[:, hs].astype(BF16)))
        return dict(rows=rows, heads=heads, dec=jnp.exp(last))

    def state_stage(ctx):
        for h, hd in enumerate(ctx["heads"]):
            hs = slice(h * HG_DK, (h + 1) * HG_DK)
            scores = None
            for mask, s in hd["parts"]:
                scores = jnp.where(mask, s, 0.0 if scores is None else scores)
            st = st_ref[h]
            hd["o"] = _dot_nt(hd["q_in"], st.astype(BF16)) + _dot(scores.astype(BF16), hd["v"])
            st_ref[h] = st * ctx["dec"][:, hs] + _dot_tn(hd["v"], hd["k_dec"])

    def output_stage(ctx):
        rows = ctx["rows"]
        gate = p_ref[rows, 6 * 512:7 * 512]
        gsil = gate * _sigmoid(gate)
        for h, hd in enumerate(ctx["heads"]):
            hs = slice(h * HG_DK, (h + 1) * HG_DK)
            o = hd["o"]
            o = o * lax.rsqrt(jnp.mean(o * o, axis=-1, keepdims=True) + EPS)
            y_ref[rows, A_WIDTH + h * HG_DV:A_WIDTH + (h + 1) * HG_DV] = (
                o * gain[:, hs] * gsil[:, hs]).astype(BF16)

    chunks_a = range(chunks_per_half)
    chunks_b = range(chunks_per_half, 2 * chunks_per_half)

    def is_gate_col(cs):
        return cs.start < 3 * A_WIDTH or 3 * A_WIDTH + B_WIDTH <= cs.start < 3 * A_WIDTH + 2 * B_WIDTH

    gate_pieces = [cs for cs in in_pieces if is_gate_col(cs)]
    rest_pieces = [cs for cs in in_pieces if not is_gate_col(cs)]

    norm_b = _spread([functools.partial(_norm_mod_block, x_ref, hb_ref, gs1, sh1, r)
                      for r in range(half, ts, ROW_BLOCK)], len(gate_pieces))
    for cs in gate_pieces:
        project(rows_a, cs)
        norm_b()

    work = _spread([functools.partial(project, rows_b, cs) for cs in gate_pieces],
                   2 * chunks_per_half)
    for ci in chunks_a:
        gates_chunk(ci, work)

    work = _spread([functools.partial(project, rows, cs)
                    for rows in (rows_a, rows_b) for cs in rest_pieces], 2 * chunks_per_half)
    for ci in chunks_b:
        gates_chunk(ci, work)
    fast_ok = jnp.max(worst_ref[...]) < HG_EXP_GUARD

    def recur_tile(fast):
        n = 2 * chunks_per_half
        ready = []
        ctx = {}

        def other_work():
            if ready:
                ready.pop(0)()

        for t in range(n + 2):
            if t < n:
                ctx[t] = scores_stage(t, fast)
            other_work()
            if 0 <= t - 1 < n:
                state_stage(ctx[t - 1])
            other_work()
            if 0 <= t - 2 < n:
                output_stage(ctx.pop(t - 2))
                if t - 2 == chunks_per_half - 1:
                    ready += [functools.partial(emit_out, rows_a, cs) for cs in out_pieces]
            other_work()
        for item in ready:
            item()

    pl.when(fast_ok)(functools.partial(recur_tile, True))
    pl.when(jnp.logical_not(fast_ok))(functools.partial(recur_tile, False))
    for cs in out_pieces:
        emit_out(rows_b, cs)


def _mix0_call(x, mod, ng, win, cw, cb, hglb, gain, wout, *, ts):
    bsz, seq, d = x.shape
    assert ts % (2 * HG_CHUNK) == 0
    const2 = lambda b, s: (0, 0)
    return pl.pallas_call(
        functools.partial(_mix0_kernel, ts=ts),
        grid=(bsz, seq // ts),
        in_specs=[
            pl.BlockSpec((None, ts, d), lambda b, s: (b, s, 0)),
            pl.BlockSpec((None, 6, d), lambda b, s: (b, 0, 0)),
            pl.BlockSpec((1, d), const2),
            pl.BlockSpec((d, IN0_COLS), const2, pipeline_mode=pl.Buffered(1)),
            pl.BlockSpec((3, A_WIDTH), const2),
            pl.BlockSpec((1, A_WIDTH), const2),
            pl.BlockSpec((3, B_WIDTH), const2),
            pl.BlockSpec((1, B_WIDTH), const2),
            pl.BlockSpec((d, d), const2, pipeline_mode=pl.Buffered(1)),
        ],
        out_specs=pl.BlockSpec((None, ts, d), lambda b, s: (b, s, 0)),
        out_shape=jax.ShapeDtypeStruct((bsz, seq, d), F32),
        scratch_shapes=[
            pltpu.VMEM((ts, d), BF16),
            pltpu.VMEM((ts, IN0_COLS), F32),
            pltpu.VMEM((ts, d), BF16),
            pltpu.VMEM((HG_HEADS, HG_DV, HG_DK), F32),
            pltpu.VMEM((SUBLANES, A_WIDTH), F32),
            pltpu.VMEM((ts, B_WIDTH), F32),
            pltpu.VMEM((ts, B_WIDTH), F32),
            pltpu.VMEM((ts, B_WIDTH), F32),
            pltpu.VMEM((HG_CHUNK, B_WIDTH), F32),
        ],
        compiler_params=pltpu.CompilerParams(
            dimension_semantics=("parallel", "arbitrary"),
            vmem_limit_bytes=VMEM_LIMIT_BYTES),
        name="mixer_conv_hgrn2",
    )(x, mod, ng, win, cw, cb, hglb, gain, wout)


def _mix1_kernel(x_ref, mod_ref, ng_ref, win_ref, bin_ref, lng_ref, lnb_ref, ws_ref,
                 bst_ref, wout_ref, o_ref, hb_ref, z_ref, y_ref, v_ref, *, ts):
    sh1 = mod_ref[0:1, :]
    sc1 = mod_ref[1:2, :]
    g1 = mod_ref[2:3, :]
    gs1 = ng_ref[...] * (1.0 + sc1)
    sub = min(GM_SUB, ts)
    _norm_mod_to(x_ref, hb_ref, gs1, sh1, sub)

    b_in = bin_ref[...]
    ln_g = lng_ref[...]
    ln_b = lnb_ref[...]
    bst = bst_ref[...]
    ti = lax.broadcasted_iota(jnp.int32, (GM_CHUNK, GM_CHUNK), 0)
    si = lax.broadcasted_iota(jnp.int32, (GM_CHUNK, GM_CHUNK), 1)
    w_tri = [jnp.where(si <= ti, ws_ref[g], 0.0).astype(BF16) for g in range(GM_GROUPS)]

    def gelu(t):
        half_t = 0.5 * t
        th = jnp.tanh(t * (_SQRT_2_OVER_PI + (_SQRT_2_OVER_PI * 0.044715) * (t * t)))
        return half_t + half_t * th

    def project(rows, cs):
        z_ref[rows, cs] = _dot(hb_ref[rows, :], win_ref[:, cs])

    def emit_out(rows, cs):
        o_ref[rows, cs] = x_ref[rows, cs] + g1[:, cs] * _dot(y_ref[rows, :], wout_ref[:, cs])

    def gate_chunk(ci, mxu_work):
        rows = slice(ci * GM_CHUNK, (ci + 1) * GM_CHUNK)
        groups = [slice(g * GM_GW, (g + 1) * GM_GW) for g in range(GM_GROUPS)]
        total = None
        for gs in groups:
            vs = slice(GM_WIDTH + gs.start, GM_WIDTH + gs.stop)
            v = gelu(z_ref[rows, vs] + b_in[:, vs])
            v_ref[rows, gs] = v
            part = jnp.sum(v, axis=-1, keepdims=True)
            total = part if total is None else total + part
            mxu_work()
        mu = total * (1.0 / GM_WIDTH)
        total = None
        for gs in groups:
            vc = v_ref[rows, gs] - mu
            v_ref[rows, gs] = vc
            part = jnp.sum(vc * vc, axis=-1, keepdims=True)
            total = part if total is None else total + part
        inv = lax.rsqrt(total * (1.0 / GM_WIDTH) + EPS)
        for g, gs in enumerate(groups):
            vn = ((v_ref[rows, gs] * inv) * ln_g[:, gs] + ln_b[:, gs]).astype(BF16)
            mixed = _dot(w_tri[g], vn) + bst[:, g:g + 1]
            u = gelu(z_ref[rows, gs] + b_in[:, gs])
            y_ref[rows, gs] = (u * mixed).astype(BF16)
            mxu_work()

    n_sub = ts // sub
    chunks_per_sub = sub // GM_CHUNK
    in_pieces = [slice(c, c + PROJ_PIECE) for c in range(0, 2 * GM_WIDTH, PROJ_PIECE)]
    out_pieces = [slice(c, c + PROJ_PIECE) for c in range(0, x_ref.shape[1], PROJ_PIECE)]

    def rows_of(i):
        return slice(i * sub, (i + 1) * sub)

    def side_items(t):
        lists = []
        if t < n_sub:
            lists.append([functools.partial(project, rows_of(t), cs) for cs in in_pieces])
        if t + 1 < n_sub:
            lists.append([functools.partial(_norm_mod_block, x_ref, hb_ref, gs1, sh1, r)
                          for r in range((t + 1) * sub, (t + 2) * sub, ROW_BLOCK)])
        if 0 <= t - 2 < n_sub:
            lists.append([functools.partial(emit_out, rows_of(t - 2), cs) for cs in out_pieces])
        return _interleave(lists)

    for t in range(n_sub + 2):
        items = side_items(t)
        if 0 <= t - 1 < n_sub:
            work = _spread(items, 2 * GM_GROUPS * chunks_per_sub)
            for ci in range((t - 1) * chunks_per_sub, t * chunks_per_sub):
                gate_chunk(ci, work)
        else:
            for item in items:
                item()


def _mix1_call(x, mod, ng, win, b_in, ln_g, ln_b, ws, bst, wout, *, ts):
    bsz, seq, d = x.shape
    const2 = lambda b, s: (0, 0)
    return pl.pallas_call(
        functools.partial(_mix1_kernel, ts=ts),
        grid=(bsz, seq // ts),
        in_specs=[
            pl.BlockSpec((None, ts, d), lambda b, s: (b, s, 0)),
            pl.BlockSpec((None, 6, d), lambda b, s: (b, 0, 0)),
            pl.BlockSpec((1, d), const2),
            pl.BlockSpec((d, 2 * GM_WIDTH), const2, pipeline_mode=pl.Buffered(1)),
            pl.BlockSpec((1, 2 * GM_WIDTH), const2),
            pl.BlockSpec((1, GM_WIDTH), const2),
            pl.BlockSpec((1, GM_WIDTH), const2),
            pl.BlockSpec((GM_GROUPS, GM_CHUNK, GM_CHUNK), lambda b, s: (0, 0, 0)),
            pl.BlockSpec((GM_CHUNK, GM_GROUPS), const2),
            pl.BlockSpec((GM_WIDTH, d), const2, pipeline_mode=pl.Buffered(1)),
        ],
        out_specs=pl.BlockSpec((None, ts, d), lambda b, s: (b, s, 0)),
        out_shape=jax.ShapeDtypeStruct((bsz, seq, d), F32),
        scratch_shapes=[
            pltpu.VMEM((ts, d), BF16),
            pltpu.VMEM((ts, 2 * GM_WIDTH), F32),
            pltpu.VMEM((ts, GM_WIDTH), BF16),
            pltpu.VMEM((ts, GM_WIDTH), F32),
        ],
        compiler_params=pltpu.CompilerParams(
            dimension_semantics=("parallel", "parallel"),
            vmem_limit_bytes=VMEM_LIMIT_BYTES),
        name="mixer_spatial_gating",
    )(x, mod, ng, win, b_in, ln_g, ln_b, ws, bst, wout)


def _ffn_kernel(x_ref, mod_ref, ng_ref, w1_ref, w2_ref, fg_ref, o_ref, hb_ref, *, ts, final):
    sh2 = mod_ref[3:4, :]
    sc2 = mod_ref[4:5, :]
    g2 = mod_ref[5:6, :]
    gs2 = ng_ref[...] * (1.0 + sc2)

    sub = min(FFN_SUB, ts)
    _norm_mod_to(x_ref, hb_ref, gs2, sh2, sub)
    fcols = D_MODEL
    n_f = D_FF // fcols
    for r0 in range(0, ts, sub):
        rows = slice(r0, r0 + sub)
        between = _spread([functools.partial(_norm_mod_block, x_ref, hb_ref, gs2, sh2, r)
                           for r in range(r0 + sub, min(r0 + 2 * sub, ts), ROW_BLOCK)], n_f)
        acc = None
        for j in range(n_f):
            hid = jnp.maximum(_dot(hb_ref[rows, :], w1_ref[:, j * fcols:(j + 1) * fcols]), 0.0)
            between()
            part = _dot((hid * hid).astype(BF16), w2_ref[j * fcols:(j + 1) * fcols, :])
            acc = part if acc is None else acc + part
        out = x_ref[rows, :] + g2 * acc
        if final:
            inv = lax.rsqrt(jnp.mean(out * out, axis=-1, keepdims=True) + EPS)
            out = (out * inv) * fg_ref[...]
        o_ref[rows, :] = out


def _ffn_call(x, mod, ng, w1, w2, fg, *, ts, final):
    bsz, seq, d = x.shape
    const2 = lambda b, s: (0, 0)
    return pl.pallas_call(
        functools.partial(_ffn_kernel, ts=ts, final=final),
        grid=(bsz, seq // ts),
        in_specs=[
            pl.BlockSpec((None, ts, d), lambda b, s: (b, s, 0)),
            pl.BlockSpec((None, 6, d), lambda b, s: (b, 0, 0)),
            pl.BlockSpec((1, d), const2),
            pl.BlockSpec((d, D_FF), const2, pipeline_mode=pl.Buffered(1)),
            pl.BlockSpec((D_FF, d), const2, pipeline_mode=pl.Buffered(1)),
            pl.BlockSpec((1, d), const2),
        ],
        out_specs=pl.BlockSpec((None, ts, d), lambda b, s: (b, s, 0)),
        out_shape=jax.ShapeDtypeStruct((bsz, seq, d), F32),
        scratch_shapes=[pltpu.VMEM((ts, d), BF16)],
        compiler_params=pltpu.CompilerParams(
            dimension_semantics=("parallel", "parallel"),
            vmem_limit_bytes=VMEM_LIMIT_BYTES),
        name="ffn_final" if final else "ffn",
    )(x, mod, ng, w1, w2, fg)


def kernel(x, c, ada_w, ada_b, norm_mix_g, norm_ffn_g, w_in0, conv_w, conv_b, hg_lb, hg_gain,
           w_out0, w_in1, b_in1, gm_ln_g, gm_ln_b, gm_ws, gm_bs, w_out1, w_ff1, w_ff2, final_g):
    bsz, seq, d = x.shape
    ts = min(SEQ_TILE, seq)
    mod = _ada_call(c, ada_w, ada_b).reshape(ada_w.shape[0], bsz, 6, d)
    fg = final_g.reshape(1, d)

    x = _mix0_call(x, mod[0], norm_mix_g[0:1], w_in0[0].astype(BF16), conv_w[0], conv_b[0:1],
                   hg_lb, hg_gain[0:1], w_out0[0].astype(BF16), ts=ts)
    ts_ffn = min(FFN_TILE, seq)
    x = _ffn_call(x, mod[0], norm_ffn_g[0:1], w_ff1[0].astype(BF16), w_ff2[0].astype(BF16), fg,
                  ts=ts_ffn, final=False)
    x = _mix1_call(x, mod[1], norm_mix_g[1:2], w_in1[0].astype(BF16), b_in1[0:1], gm_ln_g[0:1],
                   gm_ln_b[0:1], gm_ws[0], gm_bs[0].T, w_out1[0].astype(BF16),
                   ts=min(GM_TILE, seq))
    x = _ffn_call(x, mod[1], norm_ffn_g[1:2], w_ff1[1].astype(BF16), w_ff2[1].astype(BF16), fg,
                  ts=ts_ffn, final=True)
    return x
```

```python
import functools

import numpy as np
import jax
import jax.numpy as jnp
from jax import lax
from jax.experimental import pallas as pl
from jax.experimental.pallas import tpu as pltpu

F32 = jnp.float32
BF16 = jnp.bfloat16

EPS = 1e-6
D_MODEL = 1024
D_FF = 4 * D_MODEL
A_WIDTH = 512
B_WIDTH = 512
HG_HEADS = 4
HG_DK = 128
HG_DV = 128
IN0_COLS = 3 * A_WIDTH + 4 * B_WIDTH
IN0_GROUPS = ("a_b", "a_c", "a_h", "b_q", "b_f", "b_i", "b_g")
GM_WIDTH = D_MODEL
GM_GROUPS = 4
GM_CHUNK = 128
GM_GW = GM_WIDTH // GM_GROUPS

SUBLANES = 8
VMEM_LIMIT_BYTES = 56 * 1024 * 1024

ADA_TN = 1536
SEQ_TILE = 512
GM_TILE = 1024
GM_SUB = 256
FFN_TILE = 1024
FFN_SUB = 256
ROW_BLOCK = 64
HG_CHUNK = 64
HG_LEVELS = (32, 16, 8, 4, 2, 1)
HG_BLOCK = 32
HG_EXP_GUARD = 80.0
PROJ_PIECE = 256

_SQRT_2_OVER_PI = float(np.sqrt(2.0 / np.pi))


def _group(name):
    assert A_WIDTH == B_WIDTH
    j = IN0_GROUPS.index(name)
    return slice(j * A_WIDTH, (j + 1) * A_WIDTH)


def _sigmoid(v):
    return 1.0 / (1.0 + jnp.exp(-v))


def _dot(a, b):
    return jnp.dot(a, b, preferred_element_type=F32)


def _dot_nt(a, b):
    return lax.dot_general(a, b, (((1,), (1,)), ((), ())), preferred_element_type=F32)


def _dot_tn(a, b):
    return lax.dot_general(a, b, (((0,), (0,)), ((), ())), preferred_element_type=F32)


def _norm_mod_block(x_ref, hb_ref, gs, sh, r):
    xt = x_ref[pl.ds(r, ROW_BLOCK), :]
    inv = lax.rsqrt(jnp.mean(xt * xt, axis=-1, keepdims=True) + EPS)
    hb_ref[pl.ds(r, ROW_BLOCK), :] = ((xt * inv) * gs + sh).astype(BF16)


def _norm_mod_to(x_ref, hb_ref, gs, sh, rows):
    def body(i, carry):
        _norm_mod_block(x_ref, hb_ref, gs, sh, pl.multiple_of(i * ROW_BLOCK, ROW_BLOCK))
        return carry

    lax.fori_loop(0, rows // ROW_BLOCK, body, 0, unroll=True)


def _split_bf16(a):
    hi = a.astype(BF16)
    return hi, (a - hi.astype(F32)).astype(BF16)


def _ada_kernel(c_ref, w_ref, b_ref, o_ref):
    c = c_ref[...]
    bsz = c.shape[0]
    s_hi, s_lo = _split_bf16(c * _sigmoid(c))
    w_hi, w_lo = _split_bf16(w_ref[...])
    both = _dot(jnp.concatenate([s_hi, s_lo], axis=0), w_hi)
    o_ref[...] = both[:bsz] + both[bsz:] + _dot(s_hi, w_lo) + b_ref[...]


def _ada_call(c, ada_w, ada_b):
    depth, d, e = ada_w.shape
    bsz = c.shape[0]
    tn = ADA_TN
    return pl.pallas_call(
        _ada_kernel,
        grid=(depth, e // tn),
        in_specs=[
            pl.BlockSpec((bsz, d), lambda l, j: (0, 0)),
            pl.BlockSpec((None, d, tn), lambda l, j: (l, 0, j)),
            pl.BlockSpec((None, 1, tn), lambda l, j: (l, 0, j)),
        ],
        out_specs=pl.BlockSpec((None, bsz, tn), lambda l, j: (l, 0, j)),
        out_shape=jax.ShapeDtypeStruct((depth, bsz, e), F32),
        compiler_params=pltpu.CompilerParams(
            dimension_semantics=("arbitrary", "arbitrary"),
            vmem_limit_bytes=VMEM_LIMIT_BYTES),
        name="ada_modulation",
    )(c, ada_w, ada_b.reshape(depth, 1, e))


def _level_masks():
    ti = lax.broadcasted_iota(jnp.int32, (HG_CHUNK, HG_CHUNK), 0)
    si = lax.broadcasted_iota(jnp.int32, (HG_CHUNK, HG_CHUNK), 1)
    masks = []
    for m in HG_LEVELS:
        other_blk = (ti ^ si) & ~(2 * m - 1)
        masks.append((other_blk | ((ti & m) ^ m) | (si & m)) == 0)
    blk_mask = (((ti ^ si) & ~(HG_BLOCK - 1)) | jnp.maximum(si - ti, 0)) == 0
    return masks, ti == si, blk_mask


def _block_start(c):
    rows, w = c.shape
    parts = [jnp.zeros((HG_BLOCK, w), c.dtype)]
    for b in range(1, rows // HG_BLOCK):
        parts.append(jnp.broadcast_to(c[b * HG_BLOCK - 1:b * HG_BLOCK, :], (HG_BLOCK, w)))
    return jnp.concatenate(parts, axis=0)


def _block_boundary(c, m):
    rows, w = c.shape
    if m >= SUBLANES:
        blk = 2 * m
        return jnp.concatenate(
            [jnp.broadcast_to(c[b * blk + m - 1:b * blk + m, :], (blk, w))
             for b in range(rows // blk)], axis=0)
    c3 = c.reshape(rows // SUBLANES, SUBLANES, w)
    if m == 4:
        bnd = jnp.broadcast_to(c3[:, 3:4, :], c3.shape)
    else:
        assert m == 2
        sub = lax.broadcasted_iota(jnp.int32, c3.shape, 1)
        bnd = jnp.where(sub < 4, jnp.broadcast_to(c3[:, 1:2, :], c3.shape),
                        jnp.broadcast_to(c3[:, 5:6, :], c3.shape))
    return bnd.reshape(rows, w)


def _chunk_cumsum(x):
    rows, w = x.shape
    nv = rows // SUBLANES
    x3 = x.reshape(nv, SUBLANES, w)
    sub = lax.broadcasted_iota(jnp.int32, x3.shape, 1)
    sh = 1
    while sh < SUBLANES:
        x3 = x3 + jnp.where(sub >= sh, pltpu.roll(x3, sh, axis=1), 0.0)
        sh *= 2
    tot = jnp.broadcast_to(x3[:, SUBLANES - 1:SUBLANES, :], x3.shape)
    outs = [x3[0]]
    acc = tot[0]
    for j in range(1, nv):
        outs.append(x3[j] + acc)
        if j + 1 < nv:
            acc = acc + tot[j]
    return jnp.concatenate(outs, axis=0)


def _spread(items, slots):
    state = {"calls": 0, "done": 0}

    def run_due():
        state["calls"] += 1
        want = min(len(items), -(-len(items) * state["calls"] // slots))
        while state["done"] < want:
            items[state["done"]]()
            state["done"] += 1
    return run_due


def _interleave(lists):
    keyed = [((i + 0.5) / len(l), n, item) for n, l in enumerate(lists) for i, item in enumerate(l)]
    return [item for _, _, item in sorted(keyed, key=lambda k: k[:2])]


def _mix0_kernel(x_ref, mod_ref, ng_ref, win_ref, cw_ref, cb_ref, hglb_ref, gain_ref, wout_ref,
                 o_ref, hb_ref, p_ref, y_ref, st_ref, zc_ref, cum_ref, ck_ref, f_ref, worst_ref,
                 *, ts):
    @pl.when(pl.program_id(1) == 0)
    def _():
        st_ref[...] = jnp.zeros_like(st_ref)
        zc_ref[...] = jnp.zeros_like(zc_ref)

    half = ts // 2
    rows_a, rows_b = slice(0, half), slice(half, ts)
    chunks_per_half = half // HG_CHUNK
    sh1 = mod_ref[0:1, :]
    gs1 = ng_ref[...] * (1.0 + mod_ref[1:2, :])
    g1 = mod_ref[2:3, :]
    _norm_mod_to(x_ref, hb_ref, gs1, sh1, half)

    a0, a1, a2 = hglb_ref[0:1, :], hglb_ref[1:2, :], hglb_ref[2:3, :]
    amax = jnp.maximum(jnp.maximum(a0, a1), a2)
    e0, e1, e2 = jnp.exp(a0 - amax), jnp.exp(a1 - amax), jnp.exp(a2 - amax)
    lb = e0 / (e0 + e1 + e2)

    cw0, cw1, cw2 = cw_ref[0:1, :], cw_ref[1:2, :], cw_ref[2:3, :]
    cb = cb_ref[...]
    gain = gain_ref[...]
    row8 = lax.broadcasted_iota(jnp.int32, (SUBLANES, A_WIDTH), 0)
    masks, diag, blk_mask = _level_masks()

    in_pieces = [slice(c, c + PROJ_PIECE) for c in range(0, IN0_COLS, PROJ_PIECE)]
    out_pieces = [slice(c, c + PROJ_PIECE) for c in range(0, D_MODEL, PROJ_PIECE)]

    def project(rows, cs):
        p_ref[rows, cs] = _dot(hb_ref[rows, :], win_ref[:, cs])

    def emit_out(rows, cs):
        o_ref[rows, cs] = x_ref[rows, cs] + g1[:, cs] * _dot(y_ref[rows, :], wout_ref[:, cs])

    def gates_chunk(ci, other_work):
        rows = slice(ci * HG_CHUNK, (ci + 1) * HG_CHUNK)

        def col(name):
            return p_ref[rows, _group(name)]

        z = col("a_c") * col("a_h")
        prev = zc_ref[...]
        zc_ref[...] = z[HG_CHUNK - SUBLANES:, :]

        def shifted(k):
            zr = pltpu.roll(z, k, axis=0)
            pr = pltpu.roll(prev, k, axis=0)
            head = jnp.where(row8 < k, pr, zr[:SUBLANES])
            return jnp.concatenate([head, zr[SUBLANES:]], axis=0)

        conv = cb + shifted(2) * cw0
        conv = conv + shifted(1) * cw1
        conv = conv + z * cw2
        y_ref[rows, 0:A_WIDTH] = (col("a_b") * conv).astype(BF16)
        other_work()

        f = lb + (1.0 - lb) * _sigmoid(col("b_f"))
        cum = _chunk_cumsum(jnp.log(f))
        ck = cum - jnp.log(1.0 - f)
        f_ref[rows, :] = f
        cum_ref[rows, :] = cum
        ck_ref[rows, :] = ck
        e = _block_start(cum) - ck
        worst_ref[...] = e if ci == 0 else jnp.maximum(worst_ref[...], e)
        other_work()


    def scores_stage(ci, fast):
        rows = slice(ci * HG_CHUNK, (ci + 1) * HG_CHUNK)
        q = p_ref[rows, _group("b_q")]
        v = p_ref[rows, _group("b_i")]
        cum = cum_ref[rows, :]
        ck = ck_ref[rows, :]
        last = cum[HG_CHUNK - 1:HG_CHUNK, :]
        q_in = q * jnp.exp(cum)
        k_dec = jnp.exp(last - ck)
        if not fast:
            f = f_ref[rows, :]
            kk = 1.0 - f
            qf = q * f

        heads = []
        for h in range(HG_HEADS):
            hs = slice(h * HG_DK, (h + 1) * HG_DK)
            q_h, cum_h, ck_h = q[:, hs], cum[:, hs], ck[:, hs]
            if fast:
                bnd = _block_start(cum_h)
                qt = (q_h * jnp.exp(cum_h - bnd)).astype(BF16)
                kt = jnp.exp(bnd - ck_h).astype(BF16)
                parts = [(blk_mask, _dot_nt(qt, kt))]
                levels = [(m, mask) for m, mask in zip(HG_LEVELS, masks) if m >= HG_BLOCK]
            else:
                k_b = kk[:, hs].astype(BF16)
                parts = [(diag, _dot_nt(q_h.astype(BF16), k_b))]
                levels = list(zip(HG_LEVELS, masks))
            for m, mask in levels:
                if m == 1:
                    qt, kt = qf[:, hs].astype(BF16), k_b
                else:
                    bnd = _block_boundary(cum_h, m)
                    qt = (q_h * jnp.exp(cum_h - bnd)).astype(BF16)
                    kt = jnp.exp(bnd - ck_h).astype(BF16)
                parts.append((mask, _dot_nt(qt, kt)))
            heads.append(dict(parts=parts, q_in=q_in[:, hs].astype(BF16),
                              k_dec=k_dec[:, hs].astype(BF16), v=v[:, hs].astype(BF16)))
        return dict(rows=rows, heads=heads, dec=jnp.exp(last))

    def state_stage(ctx):
        for h, hd in enumerate(ctx["heads"]):
            hs = slice(h * HG_DK, (h + 1) * HG_DK)
            scores = None
            for mask, s in hd["parts"]:
                scores = jnp.where(mask, s, 0.0 if scores is None else scores)
            st = st_ref[h]
            hd["o"] = _dot(hd["q_in"], st.astype(BF16)) + _dot(scores.astype(BF16), hd["v"])
            dec_rows = jnp.broadcast_to(ctx["dec"][:, hs], (HG_DV, HG_DK)).T
            st_ref[h] = st * dec_rows + _dot_tn(hd["k_dec"], hd["v"])

    def output_stage(ctx):
        rows = ctx["rows"]
        gate = p_ref[rows, _group("b_g")]
        gsil = gate * _sigmoid(gate)
        for h, hd in enumerate(ctx["heads"]):
            hs = slice(h * HG_DK, (h + 1) * HG_DK)
            o = hd["o"]
            o = o * lax.rsqrt(jnp.mean(o * o, axis=-1, keepdims=True) + EPS)
            y_ref[rows, A_WIDTH + h * HG_DV:A_WIDTH + (h + 1) * HG_DV] = (
                o * gain[:, hs] * gsil[:, hs]).astype(BF16)

    chunks_a = range(chunks_per_half)
    chunks_b = range(chunks_per_half, 2 * chunks_per_half)

    def is_gate_col(cs):
        return any(_group(n).start <= cs.start < _group(n).stop for n in ("a_b", "a_c", "a_h", "b_f"))

    gate_pieces = [cs for cs in in_pieces if is_gate_col(cs)]
    rest_pieces = [cs for cs in in_pieces if not is_gate_col(cs)]

    norm_b = _spread([functools.partial(_norm_mod_block, x_ref, hb_ref, gs1, sh1, r)
                      for r in range(half, ts, ROW_BLOCK)], len(gate_pieces))
    for cs in gate_pieces:
        project(rows_a, cs)
        norm_b()

    work = _spread([functools.partial(project, rows_b, cs) for cs in gate_pieces],
                   2 * chunks_per_half)
    for ci in chunks_a:
        gates_chunk(ci, work)

    work = _spread([functools.partial(project, rows, cs)
                    for rows in (rows_a, rows_b) for cs in rest_pieces], 2 * chunks_per_half)
    for ci in chunks_b:
        gates_chunk(ci, work)
    fast_ok = jnp.max(worst_ref[...]) < HG_EXP_GUARD

    def recur_tile(fast):
        n = 2 * chunks_per_half
        ready = []
        ctx = {}

        def other_work():
            if ready:
                ready.pop(0)()

        for t in range(n + 2):
            if t < n:
                ctx[t] = scores_stage(t, fast)
            other_work()
            if 0 <= t - 1 < n:
                state_stage(ctx[t - 1])
            other_work()
            if 0 <= t - 2 < n:
                output_stage(ctx.pop(t - 2))
                if t - 2 == chunks_per_half - 1:
                    ready += [functools.partial(emit_out, rows_a, cs) for cs in out_pieces]
            other_work()
        for item in ready:
            item()

    pl.when(fast_ok)(functools.partial(recur_tile, True))
    pl.when(jnp.logical_not(fast_ok))(functools.partial(recur_tile, False))
    for cs in out_pieces:
        emit_out(rows_b, cs)


def _mix0_call(x, mod, ng, win, cw, cb, hglb, gain, wout, *, ts):
    bsz, seq, d = x.shape
    assert ts % (2 * HG_CHUNK) == 0
    const2 = lambda b, s: (0, 0)
    return pl.pallas_call(
        functools.partial(_mix0_kernel, ts=ts),
        grid=(bsz, seq // ts),
        in_specs=[
            pl.BlockSpec((None, ts, d), lambda b, s: (b, s, 0)),
            pl.BlockSpec((None, 6, d), lambda b, s: (b, 0, 0)),
            pl.BlockSpec((1, d), const2),
            pl.BlockSpec((d, IN0_COLS), const2, pipeline_mode=pl.Buffered(1)),
            pl.BlockSpec((3, A_WIDTH), const2),
            pl.BlockSpec((1, A_WIDTH), const2),
            pl.BlockSpec((3, B_WIDTH), const2),
            pl.BlockSpec((1, B_WIDTH), const2),
            pl.BlockSpec((d, d), const2, pipeline_mode=pl.Buffered(1)),
        ],
        out_specs=pl.BlockSpec((None, ts, d), lambda b, s: (b, s, 0)),
        out_shape=jax.ShapeDtypeStruct((bsz, seq, d), F32),
        scratch_shapes=[
            pltpu.VMEM((ts, d), BF16),
            pltpu.VMEM((ts, IN0_COLS), F32),
            pltpu.VMEM((ts, d), BF16),
            pltpu.VMEM((HG_HEADS, HG_DK, HG_DV), F32),
            pltpu.VMEM((SUBLANES, A_WIDTH), F32),
            pltpu.VMEM((ts, B_WIDTH), F32),
            pltpu.VMEM((ts, B_WIDTH), F32),
            pltpu.VMEM((ts, B_WIDTH), F32),
            pltpu.VMEM((HG_CHUNK, B_WIDTH), F32),
        ],
        compiler_params=pltpu.CompilerParams(
            dimension_semantics=("parallel", "arbitrary"),
            vmem_limit_bytes=VMEM_LIMIT_BYTES),
        name="mixer_conv_hgrn2",
    )(x, mod, ng, win, cw, cb, hglb, gain, wout)


def _mix1_kernel(x_ref, mod_ref, ng_ref, win_ref, bin_ref, lng_ref, lnb_ref, ws_ref,
                 bst_ref, wout_ref, o_ref, hb_ref, z_ref, y_ref, v_ref, *, ts):
    sh1 = mod_ref[0:1, :]
    sc1 = mod_ref[1:2, :]
    g1 = mod_ref[2:3, :]
    gs1 = ng_ref[...] * (1.0 + sc1)
    sub = min(GM_SUB, ts)
    _norm_mod_to(x_ref, hb_ref, gs1, sh1, sub)

    b_in = bin_ref[...]
    ln_g = lng_ref[...]
    ln_b = lnb_ref[...]
    bst = bst_ref[...]
    ti = lax.broadcasted_iota(jnp.int32, (GM_CHUNK, GM_CHUNK), 0)
    si = lax.broadcasted_iota(jnp.int32, (GM_CHUNK, GM_CHUNK), 1)
    w_tri = [jnp.where(si <= ti, ws_ref[g], 0.0).astype(BF16) for g in range(GM_GROUPS)]

    def gelu(t):
        half_t = 0.5 * t
        th = jnp.tanh(t * (_SQRT_2_OVER_PI + (_SQRT_2_OVER_PI * 0.044715) * (t * t)))
        return half_t + half_t * th

    def project(rows, cs):
        z_ref[rows, cs] = _dot(hb_ref[rows, :], win_ref[:, cs])

    def emit_out(rows, cs):
        o_ref[rows, cs] = x_ref[rows, cs] + g1[:, cs] * _dot(y_ref[rows, :], wout_ref[:, cs])

    def gate_chunk(ci, mxu_work):
        rows = slice(ci * GM_CHUNK, (ci + 1) * GM_CHUNK)
        groups = [slice(g * GM_GW, (g + 1) * GM_GW) for g in range(GM_GROUPS)]
        total = None
        for gs in groups:
            vs = slice(GM_WIDTH + gs.start, GM_WIDTH + gs.stop)
            v = gelu(z_ref[rows, vs] + b_in[:, vs])
            v_ref[rows, gs] = v
            part = jnp.sum(v, axis=-1, keepdims=True)
            total = part if total is None else total + part
            mxu_work()
        mu = total * (1.0 / GM_WIDTH)
        total = None
        for gs in groups:
            vc = v_ref[rows, gs] - mu
            v_ref[rows, gs] = vc
            part = jnp.sum(vc * vc, axis=-1, keepdims=True)
            total = part if total is None else total + part
        inv = lax.rsqrt(total * (1.0 / GM_WIDTH) + EPS)
        for g, gs in enumerate(groups):
            vn = ((v_ref[rows, gs] * inv) * ln_g[:, gs] + ln_b[:, gs]).astype(BF16)
            mixed = _dot(w_tri[g], vn) + bst[:, g:g + 1]
            u = gelu(z_ref[rows, gs] + b_in[:, gs])
            y_ref[rows, gs] = (u * mixed).astype(BF16)
            mxu_work()

    n_sub = ts // sub
    chunks_per_sub = sub // GM_CHUNK
    in_pieces = [slice(c, c + PROJ_PIECE) for c in range(0, 2 * GM_WIDTH, PROJ_PIECE)]
    out_pieces = [slice(c, c + PROJ_PIECE) for c in range(0, x_ref.shape[1], PROJ_PIECE)]

    def rows_of(i):
        return slice(i * sub, (i + 1) * sub)

    def side_items(t):
        lists = []
        if t < n_sub:
            lists.append([functools.partial(project, rows_of(t), cs) for cs in in_pieces])
        if t + 1 < n_sub:
            lists.append([functools.partial(_norm_mod_block, x_ref, hb_ref, gs1, sh1, r)
                          for r in range((t + 1) * sub, (t + 2) * sub, ROW_BLOCK)])
        if 0 <= t - 2 < n_sub:
            lists.append([functools.partial(emit_out, rows_of(t - 2), cs) for cs in out_pieces])
        return _interleave(lists)

    for t in range(n_sub + 2):
        items = side_items(t)
        if 0 <= t - 1 < n_sub:
            work = _spread(items, 2 * GM_GROUPS * chunks_per_sub)
            for ci in range((t - 1) * chunks_per_sub, t * chunks_per_sub):
                gate_chunk(ci, work)
        else:
            for item in items:
                item()


def _mix1_call(x, mod, ng, win, b_in, ln_g, ln_b, ws, bst, wout, *, ts):
    bsz, seq, d = x.shape
    const2 = lambda b, s: (0, 0)
    return pl.pallas_call(
        functools.partial(_mix1_kernel, ts=ts),
        grid=(bsz, seq // ts),
        in_specs=[
            pl.BlockSpec((None, ts, d), lambda b, s: (b, s, 0)),
            pl.BlockSpec((None, 6, d), lambda b, s: (b, 0, 0)),
            pl.BlockSpec((1, d), const2),
            pl.BlockSpec((d, 2 * GM_WIDTH), const2, pipeline_mode=pl.Buffered(1)),
            pl.BlockSpec((1, 2 * GM_WIDTH), const2),
            pl.BlockSpec((1, GM_WIDTH), const2),
            pl.BlockSpec((1, GM_WIDTH), const2),
            pl.BlockSpec((GM_GROUPS, GM_CHUNK, GM_CHUNK), lambda b, s: (0, 0, 0)),
            pl.BlockSpec((GM_CHUNK, GM_GROUPS), const2),
            pl.BlockSpec((GM_WIDTH, d), const2, pipeline_mode=pl.Buffered(1)),
        ],
        out_specs=pl.BlockSpec((None, ts, d), lambda b, s: (b, s, 0)),
        out_shape=jax.ShapeDtypeStruct((bsz, seq, d), F32),
        scratch_shapes=[
            pltpu.VMEM((ts, d), BF16),
            pltpu.VMEM((ts, 2 * GM_WIDTH), F32),
            pltpu.VMEM((ts, GM_WIDTH), BF16),
            pltpu.VMEM((ts, GM_WIDTH), F32),
        ],
        compiler_params=pltpu.CompilerParams(
            dimension_semantics=("parallel", "parallel"),
            vmem_limit_bytes=VMEM_LIMIT_BYTES),
        name="mixer_spatial_gating",
    )(x, mod, ng, win, b_in, ln_g, ln_b, ws, bst, wout)


def _ffn_kernel(x_ref, mod_ref, ng_ref, w1_ref, w2_ref, fg_ref, o_ref, hb_ref, *, ts, final):
    sh2 = mod_ref[3:4, :]
    sc2 = mod_ref[4:5, :]
    g2 = mod_ref[5:6, :]
    gs2 = ng_ref[...] * (1.0 + sc2)

    sub = min(FFN_SUB, ts)
    _norm_mod_to(x_ref, hb_ref, gs2, sh2, sub)
    fcols = D_MODEL
    n_f = D_FF // fcols
    for r0 in range(0, ts, sub):
        rows = slice(r0, r0 + sub)
        between = _spread([functools.partial(_norm_mod_block, x_ref, hb_ref, gs2, sh2, r)
                           for r in range(r0 + sub, min(r0 + 2 * sub, ts), ROW_BLOCK)], n_f)
        acc = None
        for j in range(n_f):
            hid = jnp.maximum(_dot(hb_ref[rows, :], w1_ref[:, j * fcols:(j + 1) * fcols]), 0.0)
            between()
            part = _dot((hid * hid).astype(BF16), w2_ref[j * fcols:(j + 1) * fcols, :])
            acc = part if acc is None else acc + part
        out = x_ref[rows, :] + g2 * acc
        if final:
            inv = lax.rsqrt(jnp.mean(out * out, axis=-1, keepdims=True) + EPS)
            out = (out * inv) * fg_ref[...]
        o_ref[rows, :] = out


def _ffn_call(x, mod, ng, w1, w2, fg, *, ts, final):
    bsz, seq, d = x.shape
    const2 = lambda b, s: (0, 0)
    return pl.pallas_call(
        functools.partial(_ffn_kernel, ts=ts, final=final),
        grid=(bsz, seq // ts),
        in_specs=[
            pl.BlockSpec((None, ts, d), lambda b, s: (b, s, 0)),
            pl.BlockSpec((None, 6, d), lambda b, s: (b, 0, 0)),
            pl.BlockSpec((1, d), const2),
            pl.BlockSpec((d, D_FF), const2, pipeline_mode=pl.Buffered(1)),
            pl.BlockSpec((D_FF, d), const2, pipeline_mode=pl.Buffered(1)),
            pl.BlockSpec((1, d), const2),
        ],
        out_specs=pl.BlockSpec((None, ts, d), lambda b, s: (b, s, 0)),
        out_shape=jax.ShapeDtypeStruct((bsz, seq, d), F32),
        scratch_shapes=[pltpu.VMEM((ts, d), BF16)],
        compiler_params=pltpu.CompilerParams(
            dimension_semantics=("parallel", "parallel"),
            vmem_limit_bytes=VMEM_LIMIT_BYTES),
        name="ffn_final" if final else "ffn",
    )(x, mod, ng, w1, w2, fg)


def kernel(x, c, ada_w, ada_b, norm_mix_g, norm_ffn_g, w_in0, conv_w, conv_b, hg_lb, hg_gain,
           w_out0, w_in1, b_in1, gm_ln_g, gm_ln_b, gm_ws, gm_bs, w_out1, w_ff1, w_ff2, final_g):
    bsz, seq, d = x.shape
    ts = min(SEQ_TILE, seq)
    mod = _ada_call(c, ada_w, ada_b).reshape(ada_w.shape[0], bsz, 6, d)
    fg = final_g.reshape(1, d)

    x = _mix0_call(x, mod[0], norm_mix_g[0:1], w_in0[0].astype(BF16), conv_w[0], conv_b[0:1],
                   hg_lb, hg_gain[0:1], w_out0[0].astype(BF16), ts=ts)
    ts_ffn = min(FFN_TILE, seq)
    x = _ffn_call(x, mod[0], norm_ffn_g[0:1], w_ff1[0].astype(BF16), w_ff2[0].astype(BF16), fg,
                  ts=ts_ffn, final=False)
    x = _mix1_call(x, mod[1], norm_mix_g[1:2], w_in1[0].astype(BF16), b_in1[0:1], gm_ln_g[0:1],
                   gm_ln_b[0:1], gm_ws[0], gm_bs[0].T, w_out1[0].astype(BF16),
                   ts=min(GM_TILE, seq))
    x = _ffn_call(x, mod[1], norm_ffn_g[1:2], w_ff1[1].astype(BF16), w_ff2[1].astype(BF16), fg,
                  ts=ts_ffn, final=True)
    return x
```

```python
import functools

import numpy as np
import jax
import jax.numpy as jnp
from jax import lax
from jax.experimental import pallas as pl
from jax.experimental.pallas import tpu as pltpu

F32 = jnp.float32
BF16 = jnp.bfloat16

EPS = 1e-6
D_MODEL = 1024
D_FF = 4 * D_MODEL
A_WIDTH = 512
B_WIDTH = 512
HG_HEADS = 4
HG_DK = 128
HG_DV = 128
IN0_COLS = 3 * A_WIDTH + 4 * B_WIDTH
IN0_GROUPS = ("a_b", "a_c", "a_h", "b_q", "b_f", "b_i", "b_g")
GM_WIDTH = D_MODEL
GM_GROUPS = 4
GM_CHUNK = 128
GM_GW = GM_WIDTH // GM_GROUPS

SUBLANES = 8
VMEM_LIMIT_BYTES = 56 * 1024 * 1024

ADA_TN = 1536
SEQ_TILE = 512
GM_TILE = 1024
GM_SUB = 256
FFN_TILE = 1024
FFN_SUB = 256
ROW_BLOCK = 64
HG_CHUNK = 64
HG_LEVELS = (32, 16, 8, 4, 2, 1)
HG_EXP_GUARD = 80.0
PROJ_PIECE = 256

_SQRT_2_OVER_PI = float(np.sqrt(2.0 / np.pi))


def _group(name):
    assert A_WIDTH == B_WIDTH
    j = IN0_GROUPS.index(name)
    return slice(j * A_WIDTH, (j + 1) * A_WIDTH)


def _sigmoid(v):
    return 1.0 / (1.0 + jnp.exp(-v))


def _dot(a, b):
    return jnp.dot(a, b, preferred_element_type=F32)


def _dot_nt(a, b):
    return lax.dot_general(a, b, (((1,), (1,)), ((), ())), preferred_element_type=F32)


def _dot_tn(a, b):
    return lax.dot_general(a, b, (((0,), (0,)), ((), ())), preferred_element_type=F32)


def _norm_mod_block(x_ref, hb_ref, gs, sh, r):
    xt = x_ref[pl.ds(r, ROW_BLOCK), :]
    inv = lax.rsqrt(jnp.mean(xt * xt, axis=-1, keepdims=True) + EPS)
    hb_ref[pl.ds(r, ROW_BLOCK), :] = ((xt * inv) * gs + sh).astype(BF16)


def _norm_mod_to(x_ref, hb_ref, gs, sh, rows):
    def body(i, carry):
        _norm_mod_block(x_ref, hb_ref, gs, sh, pl.multiple_of(i * ROW_BLOCK, ROW_BLOCK))
        return carry

    lax.fori_loop(0, rows // ROW_BLOCK, body, 0, unroll=True)


def _split_bf16(a):
    hi = a.astype(BF16)
    return hi, (a - hi.astype(F32)).astype(BF16)


def _ada_kernel(c_ref, w_ref, b_ref, o_ref):
    c = c_ref[...]
    bsz = c.shape[0]
    s_hi, s_lo = _split_bf16(c * _sigmoid(c))
    w_hi, w_lo = _split_bf16(w_ref[...])
    both = _dot(jnp.concatenate([s_hi, s_lo], axis=0), w_hi)
    o_ref[...] = both[:bsz] + both[bsz:] + _dot(s_hi, w_lo) + b_ref[...]


def _ada_call(c, ada_w, ada_b):
    depth, d, e = ada_w.shape
    bsz = c.shape[0]
    tn = ADA_TN
    return pl.pallas_call(
        _ada_kernel,
        grid=(depth, e // tn),
        in_specs=[
            pl.BlockSpec((bsz, d), lambda l, j: (0, 0)),
            pl.BlockSpec((None, d, tn), lambda l, j: (l, 0, j)),
            pl.BlockSpec((None, 1, tn), lambda l, j: (l, 0, j)),
        ],
        out_specs=pl.BlockSpec((None, bsz, tn), lambda l, j: (l, 0, j)),
        out_shape=jax.ShapeDtypeStruct((depth, bsz, e), F32),
        compiler_params=pltpu.CompilerParams(
            dimension_semantics=("arbitrary", "arbitrary"),
            vmem_limit_bytes=VMEM_LIMIT_BYTES),
        name="ada_modulation",
    )(c, ada_w, ada_b.reshape(depth, 1, e))


def _level_masks():
    ti = lax.broadcasted_iota(jnp.int32, (HG_CHUNK, HG_CHUNK), 0)
    si = lax.broadcasted_iota(jnp.int32, (HG_CHUNK, HG_CHUNK), 1)
    masks = []
    for m in HG_LEVELS:
        other_blk = (ti ^ si) & ~(2 * m - 1)
        masks.append((other_blk | ((ti & m) ^ m) | (si & m)) == 0)
    return masks, ti == si, si <= ti


def _mid_row(c):
    return c[HG_CHUNK // 2 - 1:HG_CHUNK // 2, :]


def _block_boundary(c, m):
    rows, w = c.shape
    if m >= SUBLANES:
        blk = 2 * m
        return jnp.concatenate(
            [jnp.broadcast_to(c[b * blk + m - 1:b * blk + m, :], (blk, w))
             for b in range(rows // blk)], axis=0)
    c3 = c.reshape(rows // SUBLANES, SUBLANES, w)
    if m == 4:
        bnd = jnp.broadcast_to(c3[:, 3:4, :], c3.shape)
    else:
        assert m == 2
        sub = lax.broadcasted_iota(jnp.int32, c3.shape, 1)
        bnd = jnp.where(sub < 4, jnp.broadcast_to(c3[:, 1:2, :], c3.shape),
                        jnp.broadcast_to(c3[:, 5:6, :], c3.shape))
    return bnd.reshape(rows, w)


def _chunk_cumsum(x):
    rows, w = x.shape
    nv = rows // SUBLANES
    x3 = x.reshape(nv, SUBLANES, w)
    sub = lax.broadcasted_iota(jnp.int32, x3.shape, 1)
    sh = 1
    while sh < SUBLANES:
        x3 = x3 + jnp.where(sub >= sh, pltpu.roll(x3, sh, axis=1), 0.0)
        sh *= 2
    tot = jnp.broadcast_to(x3[:, SUBLANES - 1:SUBLANES, :], x3.shape)
    outs = [x3[0]]
    acc = tot[0]
    for j in range(1, nv):
        outs.append(x3[j] + acc)
        if j + 1 < nv:
            acc = acc + tot[j]
    return jnp.concatenate(outs, axis=0)


def _spread(items, slots):
    state = {"calls": 0, "done": 0}

    def run_due():
        state["calls"] += 1
        want = min(len(items), -(-len(items) * state["calls"] // slots))
        while state["done"] < want:
            items[state["done"]]()
            state["done"] += 1
    return run_due


def _interleave(lists):
    keyed = [((i + 0.5) / len(l), n, item) for n, l in enumerate(lists) for i, item in enumerate(l)]
    return [item for _, _, item in sorted(keyed, key=lambda k: k[:2])]


def _mix0_kernel(x_ref, mod_ref, ng_ref, win_ref, cw_ref, cb_ref, hglb_ref, gain_ref, wout_ref,
                 o_ref, hb_ref, p_ref, y_ref, st_ref, zc_ref, cum_ref, ck_ref, f_ref, worst_ref,
                 *, ts):
    @pl.when(pl.program_id(1) == 0)
    def _():
        st_ref[...] = jnp.zeros_like(st_ref)
        zc_ref[...] = jnp.zeros_like(zc_ref)

    half = ts // 2
    rows_a, rows_b = slice(0, half), slice(half, ts)
    chunks_per_half = half // HG_CHUNK
    sh1 = mod_ref[0:1, :]
    gs1 = ng_ref[...] * (1.0 + mod_ref[1:2, :])
    g1 = mod_ref[2:3, :]
    _norm_mod_to(x_ref, hb_ref, gs1, sh1, half)

    a0, a1, a2 = hglb_ref[0:1, :], hglb_ref[1:2, :], hglb_ref[2:3, :]
    amax = jnp.maximum(jnp.maximum(a0, a1), a2)
    e0, e1, e2 = jnp.exp(a0 - amax), jnp.exp(a1 - amax), jnp.exp(a2 - amax)
    lb = e0 / (e0 + e1 + e2)

    cw0, cw1, cw2 = cw_ref[0:1, :], cw_ref[1:2, :], cw_ref[2:3, :]
    cb = cb_ref[...]
    gain = gain_ref[...]
    row8 = lax.broadcasted_iota(jnp.int32, (SUBLANES, A_WIDTH), 0)
    masks, diag, causal = _level_masks()

    in_pieces = [slice(c, c + PROJ_PIECE) for c in range(0, IN0_COLS, PROJ_PIECE)]
    out_pieces = [slice(c, c + PROJ_PIECE) for c in range(0, D_MODEL, PROJ_PIECE)]

    def project(rows, cs):
        p_ref[rows, cs] = _dot(hb_ref[rows, :], win_ref[:, cs])

    def emit_out(rows, cs):
        o_ref[rows, cs] = x_ref[rows, cs] + g1[:, cs] * _dot(y_ref[rows, :], wout_ref[:, cs])

    def gates_chunk(ci, other_work):
        rows = slice(ci * HG_CHUNK, (ci + 1) * HG_CHUNK)

        def col(name):
            return p_ref[rows, _group(name)]

        z = col("a_c") * col("a_h")
        prev = zc_ref[...]
        zc_ref[...] = z[HG_CHUNK - SUBLANES:, :]

        def shifted(k):
            zr = pltpu.roll(z, k, axis=0)
            pr = pltpu.roll(prev, k, axis=0)
            head = jnp.where(row8 < k, pr, zr[:SUBLANES])
            return jnp.concatenate([head, zr[SUBLANES:]], axis=0)

        conv = cb + shifted(2) * cw0
        conv = conv + shifted(1) * cw1
        conv = conv + z * cw2
        y_ref[rows, 0:A_WIDTH] = (col("a_b") * conv).astype(BF16)
        other_work()

        f = lb + (1.0 - lb) * _sigmoid(col("b_f"))
        cum = _chunk_cumsum(jnp.log(f))
        ck = cum - jnp.log(1.0 - f)
        f_ref[rows, :] = f
        cum_ref[rows, :] = cum
        ck_ref[rows, :] = ck
        mid = _mid_row(cum)
        e = jnp.maximum(mid - ck, -mid)
        worst_ref[...] = e if ci == 0 else jnp.maximum(worst_ref[...], e)
        other_work()


    def scores_stage(ci, fast):
        rows = slice(ci * HG_CHUNK, (ci + 1) * HG_CHUNK)
        q = p_ref[rows, _group("b_q")]
        v = p_ref[rows, _group("b_i")]
        cum = cum_ref[rows, :]
        ck = ck_ref[rows, :]
        last = cum[HG_CHUNK - 1:HG_CHUNK, :]
        q_in = q * jnp.exp(cum)
        k_dec = jnp.exp(last - ck)
        if not fast:
            f = f_ref[rows, :]
            kk = 1.0 - f
            qf = q * f

        heads = []
        for h in range(HG_HEADS):
            hs = slice(h * HG_DK, (h + 1) * HG_DK)
            q_h, cum_h, ck_h = q[:, hs], cum[:, hs], ck[:, hs]
            if fast:
                mid = _mid_row(cum_h)
                qt = (q_h * jnp.exp(cum_h - mid)).astype(BF16)
                kt = jnp.exp(mid - ck_h).astype(BF16)
                parts = [(causal, _dot_nt(qt, kt))]
                levels = []
            else:
                k_b = kk[:, hs].astype(BF16)
                parts = [(diag, _dot_nt(q_h.astype(BF16), k_b))]
                levels = list(zip(HG_LEVELS, masks))
            for m, mask in levels:
                if m == 1:
                    qt, kt = qf[:, hs].astype(BF16), k_b
                else:
                    bnd = _block_boundary(cum_h, m)
                    qt = (q_h * jnp.exp(cum_h - bnd)).astype(BF16)
                    kt = jnp.exp(bnd - ck_h).astype(BF16)
                parts.append((mask, _dot_nt(qt, kt)))
            heads.append(dict(parts=parts, q_in=q_in[:, hs].astype(BF16),
                              k_dec=k_dec[:, hs].astype(BF16), v=v[:, hs].astype(BF16)))
        return dict(rows=rows, heads=heads, dec=jnp.exp(last))

    def state_stage(ctx):
        for h, hd in enumerate(ctx["heads"]):
            hs = slice(h * HG_DK, (h + 1) * HG_DK)
            scores = None
            for mask, s in hd["parts"]:
                scores = jnp.where(mask, s, 0.0 if scores is None else scores)
            st = st_ref[h]
            hd["o"] = _dot(hd["q_in"], st.astype(BF16)) + _dot(scores.astype(BF16), hd["v"])
            dec_rows = jnp.broadcast_to(ctx["dec"][:, hs], (HG_DV, HG_DK)).T
            st_ref[h] = st * dec_rows + _dot_tn(hd["k_dec"], hd["v"])

    def output_stage(ctx):
        rows = ctx["rows"]
        gate = p_ref[rows, _group("b_g")]
        gsil = gate * _sigmoid(gate)
        for h, hd in enumerate(ctx["heads"]):
            hs = slice(h * HG_DK, (h + 1) * HG_DK)
            o = hd["o"]
            o = o * lax.rsqrt(jnp.mean(o * o, axis=-1, keepdims=True) + EPS)
            y_ref[rows, A_WIDTH + h * HG_DV:A_WIDTH + (h + 1) * HG_DV] = (
                o * gain[:, hs] * gsil[:, hs]).astype(BF16)

    chunks_a = range(chunks_per_half)
    chunks_b = range(chunks_per_half, 2 * chunks_per_half)

    def is_gate_col(cs):
        return any(_group(n).start <= cs.start < _group(n).stop for n in ("a_b", "a_c", "a_h", "b_f"))

    gate_pieces = [cs for cs in in_pieces if is_gate_col(cs)]
    rest_pieces = [cs for cs in in_pieces if not is_gate_col(cs)]

    norm_b = _spread([functools.partial(_norm_mod_block, x_ref, hb_ref, gs1, sh1, r)
                      for r in range(half, ts, ROW_BLOCK)], len(gate_pieces))
    for cs in gate_pieces:
        project(rows_a, cs)
        norm_b()

    work = _spread([functools.partial(project, rows_b, cs) for cs in gate_pieces],
                   2 * chunks_per_half)
    for ci in chunks_a:
        gates_chunk(ci, work)

    work = _spread([functools.partial(project, rows, cs)
                    for rows in (rows_a, rows_b) for cs in rest_pieces], 2 * chunks_per_half)
    for ci in chunks_b:
        gates_chunk(ci, work)
    fast_ok = jnp.max(worst_ref[...]) < HG_EXP_GUARD

    def recur_tile(fast):
        n = 2 * chunks_per_half
        ready = []
        ctx = {}

        def other_work():
            if ready:
                ready.pop(0)()

        for t in range(n + 2):
            if t < n:
                ctx[t] = scores_stage(t, fast)
            other_work()
            if 0 <= t - 1 < n:
                state_stage(ctx[t - 1])
            other_work()
            if 0 <= t - 2 < n:
                output_stage(ctx.pop(t - 2))
                if t - 2 == chunks_per_half - 1:
                    ready += [functools.partial(emit_out, rows_a, cs) for cs in out_pieces]
            other_work()
        for item in ready:
            item()

    pl.when(fast_ok)(functools.partial(recur_tile, True))
    pl.when(jnp.logical_not(fast_ok))(functools.partial(recur_tile, False))
    for cs in out_pieces:
        emit_out(rows_b, cs)


def _mix0_call(x, mod, ng, win, cw, cb, hglb, gain, wout, *, ts):
    bsz, seq, d = x.shape
    assert ts % (2 * HG_CHUNK) == 0
    const2 = lambda b, s: (0, 0)
    return pl.pallas_call(
        functools.partial(_mix0_kernel, ts=ts),
        grid=(bsz, seq // ts),
        in_specs=[
            pl.BlockSpec((None, ts, d), lambda b, s: (b, s, 0)),
            pl.BlockSpec((None, 6, d), lambda b, s: (b, 0, 0)),
            pl.BlockSpec((1, d), const2),
            pl.BlockSpec((d, IN0_COLS), const2, pipeline_mode=pl.Buffered(1)),
            pl.BlockSpec((3, A_WIDTH), const2),
            pl.BlockSpec((1, A_WIDTH), const2),
            pl.BlockSpec((3, B_WIDTH), const2),
            pl.BlockSpec((1, B_WIDTH), const2),
            pl.BlockSpec((d, d), const2, pipeline_mode=pl.Buffered(1)),
        ],
        out_specs=pl.BlockSpec((None, ts, d), lambda b, s: (b, s, 0)),
        out_shape=jax.ShapeDtypeStruct((bsz, seq, d), F32),
        scratch_shapes=[
            pltpu.VMEM((ts, d), BF16),
            pltpu.VMEM((ts, IN0_COLS), F32),
            pltpu.VMEM((ts, d), BF16),
            pltpu.VMEM((HG_HEADS, HG_DK, HG_DV), F32),
            pltpu.VMEM((SUBLANES, A_WIDTH), F32),
            pltpu.VMEM((ts, B_WIDTH), F32),
            pltpu.VMEM((ts, B_WIDTH), F32),
            pltpu.VMEM((ts, B_WIDTH), F32),
            pltpu.VMEM((HG_CHUNK, B_WIDTH), F32),
        ],
        compiler_params=pltpu.CompilerParams(
            dimension_semantics=("parallel", "arbitrary"),
            vmem_limit_bytes=VMEM_LIMIT_BYTES),
        name="mixer_conv_hgrn2",
    )(x, mod, ng, win, cw, cb, hglb, gain, wout)


def _mix1_kernel(x_ref, mod_ref, ng_ref, win_ref, bin_ref, lng_ref, lnb_ref, ws_ref,
                 bst_ref, wout_ref, o_ref, hb_ref, z_ref, y_ref, v_ref, *, ts):
    sh1 = mod_ref[0:1, :]
    sc1 = mod_ref[1:2, :]
    g1 = mod_ref[2:3, :]
    gs1 = ng_ref[...] * (1.0 + sc1)
    sub = min(GM_SUB, ts)
    _norm_mod_to(x_ref, hb_ref, gs1, sh1, sub)

    b_in = bin_ref[...]
    ln_g = lng_ref[...]
    ln_b = lnb_ref[...]
    bst = bst_ref[...]
    ti = lax.broadcasted_iota(jnp.int32, (GM_CHUNK, GM_CHUNK), 0)
    si = lax.broadcasted_iota(jnp.int32, (GM_CHUNK, GM_CHUNK), 1)
    w_tri = [jnp.where(si <= ti, ws_ref[g], 0.0).astype(BF16) for g in range(GM_GROUPS)]

    def gelu(t):
        half_t = 0.5 * t
        th = jnp.tanh(t * (_SQRT_2_OVER_PI + (_SQRT_2_OVER_PI * 0.044715) * (t * t)))
        return half_t + half_t * th

    def project(rows, cs):
        z_ref[rows, cs] = _dot(hb_ref[rows, :], win_ref[:, cs])

    def emit_out(rows, cs):
        o_ref[rows, cs] = x_ref[rows, cs] + g1[:, cs] * _dot(y_ref[rows, :], wout_ref[:, cs])

    def gate_chunk(ci, mxu_work):
        rows = slice(ci * GM_CHUNK, (ci + 1) * GM_CHUNK)
        groups = [slice(g * GM_GW, (g + 1) * GM_GW) for g in range(GM_GROUPS)]
        total = None
        for gs in groups:
            vs = slice(GM_WIDTH + gs.start, GM_WIDTH + gs.stop)
            v = gelu(z_ref[rows, vs] + b_in[:, vs])
            v_ref[rows, gs] = v
            part = jnp.sum(v, axis=-1, keepdims=True)
            total = part if total is None else total + part
            mxu_work()
        mu = total * (1.0 / GM_WIDTH)
        total = None
        for gs in groups:
            vc = v_ref[rows, gs] - mu
            v_ref[rows, gs] = vc
            part = jnp.sum(vc * vc, axis=-1, keepdims=True)
            total = part if total is None else total + part
        inv = lax.rsqrt(total * (1.0 / GM_WIDTH) + EPS)
        for g, gs in enumerate(groups):
            vn = ((v_ref[rows, gs] * inv) * ln_g[:, gs] + ln_b[:, gs]).astype(BF16)
            mixed = _dot(w_tri[g], vn) + bst[:, g:g + 1]
            u = gelu(z_ref[rows, gs] + b_in[:, gs])
            y_ref[rows, gs] = (u * mixed).astype(BF16)
            mxu_work()

    n_sub = ts // sub
    chunks_per_sub = sub // GM_CHUNK
    in_pieces = [slice(c, c + PROJ_PIECE) for c in range(0, 2 * GM_WIDTH, PROJ_PIECE)]
    out_pieces = [slice(c, c + PROJ_PIECE) for c in range(0, x_ref.shape[1], PROJ_PIECE)]

    def rows_of(i):
        return slice(i * sub, (i + 1) * sub)

    def side_items(t):
        lists = []
        if t < n_sub:
            lists.append([functools.partial(project, rows_of(t), cs) for cs in in_pieces])
        if t + 1 < n_sub:
            lists.append([functools.partial(_norm_mod_block, x_ref, hb_ref, gs1, sh1, r)
                          for r in range((t + 1) * sub, (t + 2) * sub, ROW_BLOCK)])
        if 0 <= t - 2 < n_sub:
            lists.append([functools.partial(emit_out, rows_of(t - 2), cs) for cs in out_pieces])
        return _interleave(lists)

    for t in range(n_sub + 2):
        items = side_items(t)
        if 0 <= t - 1 < n_sub:
            work = _spread(items, 2 * GM_GROUPS * chunks_per_sub)
            for ci in range((t - 1) * chunks_per_sub, t * chunks_per_sub):
                gate_chunk(ci, work)
        else:
            for item in items:
                item()


def _mix1_call(x, mod, ng, win, b_in, ln_g, ln_b, ws, bst, wout, *, ts):
    bsz, seq, d = x.shape
    const2 = lambda b, s: (0, 0)
    return pl.pallas_call(
        functools.partial(_mix1_kernel, ts=ts),
        grid=(bsz, seq // ts),
        in_specs=[
            pl.BlockSpec((None, ts, d), lambda b, s: (b, s, 0)),
            pl.BlockSpec((None, 6, d), lambda b, s: (b, 0, 0)),
            pl.BlockSpec((1, d), const2),
            pl.BlockSpec((d, 2 * GM_WIDTH), const2, pipeline_mode=pl.Buffered(1)),
            pl.BlockSpec((1, 2 * GM_WIDTH), const2),
            pl.BlockSpec((1, GM_WIDTH), const2),
            pl.BlockSpec((1, GM_WIDTH), const2),
            pl.BlockSpec((GM_GROUPS, GM_CHUNK, GM_CHUNK), lambda b, s: (0, 0, 0)),
            pl.BlockSpec((GM_CHUNK, GM_GROUPS), const2),
            pl.BlockSpec((GM_WIDTH, d), const2, pipeline_mode=pl.Buffered(1)),
        ],
        out_specs=pl.BlockSpec((None, ts, d), lambda b, s: (b, s, 0)),
        out_shape=jax.ShapeDtypeStruct((bsz, seq, d), F32),
        scratch_shapes=[
            pltpu.VMEM((ts, d), BF16),
            pltpu.VMEM((ts, 2 * GM_WIDTH), F32),
            pltpu.VMEM((ts, GM_WIDTH), BF16),
            pltpu.VMEM((ts, GM_WIDTH), F32),
        ],
        compiler_params=pltpu.CompilerParams(
            dimension_semantics=("parallel", "parallel"),
            vmem_limit_bytes=VMEM_LIMIT_BYTES),
        name="mixer_spatial_gating",
    )(x, mod, ng, win, b_in, ln_g, ln_b, ws, bst, wout)


def _ffn_kernel(x_ref, mod_ref, ng_ref, w1_ref, w2_ref, fg_ref, o_ref, hb_ref, *, ts, final):
    sh2 = mod_ref[3:4, :]
    sc2 = mod_ref[4:5, :]
    g2 = mod_ref[5:6, :]
    gs2 = ng_ref[...] * (1.0 + sc2)

    sub = min(FFN_SUB, ts)
    _norm_mod_to(x_ref, hb_ref, gs2, sh2, sub)
    fcols = D_MODEL
    n_f = D_FF // fcols
    for r0 in range(0, ts, sub):
        rows = slice(r0, r0 + sub)
        between = _spread([functools.partial(_norm_mod_block, x_ref, hb_ref, gs2, sh2, r)
                           for r in range(r0 + sub, min(r0 + 2 * sub, ts), ROW_BLOCK)], n_f)
        acc = None
        for j in range(n_f):
            hid = jnp.maximum(_dot(hb_ref[rows, :], w1_ref[:, j * fcols:(j + 1) * fcols]), 0.0)
            between()
            part = _dot((hid * hid).astype(BF16), w2_ref[j * fcols:(j + 1) * fcols, :])
            acc = part if acc is None else acc + part
        out = x_ref[rows, :] + g2 * acc
        if final:
            inv = lax.rsqrt(jnp.mean(out * out, axis=-1, keepdims=True) + EPS)
            out = (out * inv) * fg_ref[...]
        o_ref[rows, :] = out


def _ffn_call(x, mod, ng, w1, w2, fg, *, ts, final):
    bsz, seq, d = x.shape
    const2 = lambda b, s: (0, 0)
    return pl.pallas_call(
        functools.partial(_ffn_kernel, ts=ts, final=final),
        grid=(bsz, seq // ts),
        in_specs=[
            pl.BlockSpec((None, ts, d), lambda b, s: (b, s, 0)),
            pl.BlockSpec((None, 6, d), lambda b, s: (b, 0, 0)),
            pl.BlockSpec((1, d), const2),
            pl.BlockSpec((d, D_FF), const2, pipeline_mode=pl.Buffered(1)),
            pl.BlockSpec((D_FF, d), const2, pipeline_mode=pl.Buffered(1)),
            pl.BlockSpec((1, d), const2),
        ],
        out_specs=pl.BlockSpec((None, ts, d), lambda b, s: (b, s, 0)),
        out_shape=jax.ShapeDtypeStruct((bsz, seq, d), F32),
        scratch_shapes=[pltpu.VMEM((ts, d), BF16)],
        compiler_params=pltpu.CompilerParams(
            dimension_semantics=("parallel", "parallel"),
            vmem_limit_bytes=VMEM_LIMIT_BYTES),
        name="ffn_final" if final else "ffn",
    )(x, mod, ng, w1, w2, fg)


def kernel(x, c, ada_w, ada_b, norm_mix_g, norm_ffn_g, w_in0, conv_w, conv_b, hg_lb, hg_gain,
           w_out0, w_in1, b_in1, gm_ln_g, gm_ln_b, gm_ws, gm_bs, w_out1, w_ff1, w_ff2, final_g):
    bsz, seq, d = x.shape
    ts = min(SEQ_TILE, seq)
    mod = _ada_call(c, ada_w, ada_b).reshape(ada_w.shape[0], bsz, 6, d)
    fg = final_g.reshape(1, d)

    x = _mix0_call(x, mod[0], norm_mix_g[0:1], w_in0[0].astype(BF16), conv_w[0], conv_b[0:1],
                   hg_lb, hg_gain[0:1], w_out0[0].astype(BF16), ts=ts)
    ts_ffn = min(FFN_TILE, seq)
    x = _ffn_call(x, mod[0], norm_ffn_g[0:1], w_ff1[0].astype(BF16), w_ff2[0].astype(BF16), fg,
                  ts=ts_ffn, final=False)
    x = _mix1_call(x, mod[1], norm_mix_g[1:2], w_in1[0].astype(BF16), b_in1[0:1], gm_ln_g[0:1],
                   gm_ln_b[0:1], gm_ws[0], gm_bs[0].T, w_out1[0].astype(BF16),
                   ts=min(GM_TILE, seq))
    x = _ffn_call(x, mod[1], norm_ffn_g[1:2], w_ff1[1].astype(BF16), w_ff2[1].astype(BF16), fg,
                  ts=ts_ffn, final=True)
    return x
```

```python
import functools

import numpy as np
import jax
import jax.numpy as jnp
from jax import lax
from jax.experimental import pallas as pl
from jax.experimental.pallas import tpu as pltpu

F32 = jnp.float32
BF16 = jnp.bfloat16

EPS = 1e-6
D_MODEL = 1024
D_FF = 4 * D_MODEL
A_WIDTH = 512
B_WIDTH = 512
HG_HEADS = 4
HG_DK = 128
HG_DV = 128
IN0_COLS = 3 * A_WIDTH + 4 * B_WIDTH
IN0_GROUPS = ("a_b", "a_c", "a_h", "b_q", "b_f", "b_i", "b_g")
GM_WIDTH = D_MODEL
GM_GROUPS = 4
GM_CHUNK = 128
GM_GW = GM_WIDTH // GM_GROUPS

SUBLANES = 8
VMEM_LIMIT_BYTES = 56 * 1024 * 1024

ADA_TN = 1536
SEQ_TILE = 512
GM_TILE = 1024
GM_SUB = 256
FFN_TILE = 1024
FFN_SUB = 256
ROW_BLOCK = 64
HG_CHUNK = 64
HG_LEVELS = (32, 16, 8, 4, 2, 1)
HG_EXP_GUARD = 80.0
PROJ_PIECE = 256

_SQRT_2_OVER_PI = float(np.sqrt(2.0 / np.pi))


def _group(name):
    assert A_WIDTH == B_WIDTH
    j = IN0_GROUPS.index(name)
    return slice(j * A_WIDTH, (j + 1) * A_WIDTH)


def _sigmoid(v):
    return 1.0 / (1.0 + jnp.exp(-v))


def _dot(a, b):
    return jnp.dot(a, b, preferred_element_type=F32)


def _dot_nt(a, b):
    return lax.dot_general(a, b, (((1,), (1,)), ((), ())), preferred_element_type=F32)


def _dot_tn(a, b):
    return lax.dot_general(a, b, (((0,), (0,)), ((), ())), preferred_element_type=F32)


def _norm_mod_block(x_ref, hb_ref, gs, sh, r):
    xt = x_ref[pl.ds(r, ROW_BLOCK), :]
    inv = lax.rsqrt(jnp.mean(xt * xt, axis=-1, keepdims=True) + EPS)
    hb_ref[pl.ds(r, ROW_BLOCK), :] = ((xt * inv) * gs + sh).astype(BF16)


def _norm_mod_to(x_ref, hb_ref, gs, sh, rows):
    def body(i, carry):
        _norm_mod_block(x_ref, hb_ref, gs, sh, pl.multiple_of(i * ROW_BLOCK, ROW_BLOCK))
        return carry

    lax.fori_loop(0, rows // ROW_BLOCK, body, 0, unroll=True)


def _split_bf16(a):
    hi = a.astype(BF16)
    return hi, (a - hi.astype(F32)).astype(BF16)


def _ada_kernel(c_ref, w_ref, b_ref, o_ref):
    c = c_ref[...]
    bsz = c.shape[0]
    s_hi, s_lo = _split_bf16(c * _sigmoid(c))
    w_hi, w_lo = _split_bf16(w_ref[...])
    both = _dot(jnp.concatenate([s_hi, s_lo], axis=0), w_hi)
    o_ref[...] = both[:bsz] + both[bsz:] + _dot(s_hi, w_lo) + b_ref[...]


def _ada_call(c, ada_w, ada_b):
    depth, d, e = ada_w.shape
    bsz = c.shape[0]
    tn = ADA_TN
    return pl.pallas_call(
        _ada_kernel,
        grid=(depth, e // tn),
        in_specs=[
            pl.BlockSpec((bsz, d), lambda l, j: (0, 0)),
            pl.BlockSpec((None, d, tn), lambda l, j: (l, 0, j)),
            pl.BlockSpec((None, 1, tn), lambda l, j: (l, 0, j)),
        ],
        out_specs=pl.BlockSpec((None, bsz, tn), lambda l, j: (l, 0, j)),
        out_shape=jax.ShapeDtypeStruct((depth, bsz, e), F32),
        compiler_params=pltpu.CompilerParams(
            dimension_semantics=("arbitrary", "arbitrary"),
            vmem_limit_bytes=VMEM_LIMIT_BYTES),
        name="ada_modulation",
    )(c, ada_w, ada_b.reshape(depth, 1, e))


def _level_masks():
    ti = lax.broadcasted_iota(jnp.int32, (HG_CHUNK, HG_CHUNK), 0)
    si = lax.broadcasted_iota(jnp.int32, (HG_CHUNK, HG_CHUNK), 1)
    masks = []
    for m in HG_LEVELS:
        other_blk = (ti ^ si) & ~(2 * m - 1)
        masks.append((other_blk | ((ti & m) ^ m) | (si & m)) == 0)
    return masks, ti == si, si <= ti


def _mid_row(c):
    return c[HG_CHUNK // 2 - 1:HG_CHUNK // 2, :]


def _block_boundary(c, m):
    rows, w = c.shape
    if m >= SUBLANES:
        blk = 2 * m
        return jnp.concatenate(
            [jnp.broadcast_to(c[b * blk + m - 1:b * blk + m, :], (blk, w))
             for b in range(rows // blk)], axis=0)
    c3 = c.reshape(rows // SUBLANES, SUBLANES, w)
    if m == 4:
        bnd = jnp.broadcast_to(c3[:, 3:4, :], c3.shape)
    else:
        assert m == 2
        sub = lax.broadcasted_iota(jnp.int32, c3.shape, 1)
        bnd = jnp.where(sub < 4, jnp.broadcast_to(c3[:, 1:2, :], c3.shape),
                        jnp.broadcast_to(c3[:, 5:6, :], c3.shape))
    return bnd.reshape(rows, w)


def _chunk_cumsum(x):
    rows, w = x.shape
    nv = rows // SUBLANES
    x3 = x.reshape(nv, SUBLANES, w)
    sub = lax.broadcasted_iota(jnp.int32, x3.shape, 1)
    sh = 1
    while sh < SUBLANES:
        x3 = x3 + jnp.where(sub >= sh, pltpu.roll(x3, sh, axis=1), 0.0)
        sh *= 2
    tot = jnp.broadcast_to(x3[:, SUBLANES - 1:SUBLANES, :], x3.shape)
    outs = [x3[0]]
    acc = tot[0]
    for j in range(1, nv):
        outs.append(x3[j] + acc)
        if j + 1 < nv:
            acc = acc + tot[j]
    return jnp.concatenate(outs, axis=0)


def _spread(items, slots):
    state = {"calls": 0, "done": 0}

    def run_due():
        state["calls"] += 1
        want = min(len(items), -(-len(items) * state["calls"] // slots))
        while state["done"] < want:
            items[state["done"]]()
            state["done"] += 1
    return run_due


def _interleave(lists):
    keyed = [((i + 0.5) / len(l), n, item) for n, l in enumerate(lists) for i, item in enumerate(l)]
    return [item for _, _, item in sorted(keyed, key=lambda k: k[:2])]


def _mix0_kernel(x_ref, mod_ref, ng_ref, win_ref, cw_ref, cb_ref, hglb_ref, gain_ref, wout_ref,
                 o_ref, hb_ref, p_ref, y_ref, st_ref, zc_ref, cum_ref, ck_ref, f_ref, worst_ref,
                 *, ts):
    @pl.when(pl.program_id(1) == 0)
    def _():
        st_ref[...] = jnp.zeros_like(st_ref)
        zc_ref[...] = jnp.zeros_like(zc_ref)

    half = ts // 2
    rows_a, rows_b = slice(0, half), slice(half, ts)
    chunks_per_half = half // HG_CHUNK
    sh1 = mod_ref[0:1, :]
    gs1 = ng_ref[...] * (1.0 + mod_ref[1:2, :])
    g1 = mod_ref[2:3, :]
    _norm_mod_to(x_ref, hb_ref, gs1, sh1, half)

    a0, a1, a2 = hglb_ref[0:1, :], hglb_ref[1:2, :], hglb_ref[2:3, :]
    amax = jnp.maximum(jnp.maximum(a0, a1), a2)
    e0, e1, e2 = jnp.exp(a0 - amax), jnp.exp(a1 - amax), jnp.exp(a2 - amax)
    lb = e0 / (e0 + e1 + e2)

    cw0, cw1, cw2 = cw_ref[0:1, :], cw_ref[1:2, :], cw_ref[2:3, :]
    cb = cb_ref[...]
    gain = gain_ref[...]
    row8 = lax.broadcasted_iota(jnp.int32, (SUBLANES, A_WIDTH), 0)
    masks, diag, causal = _level_masks()

    in_pieces = [slice(c, c + PROJ_PIECE) for c in range(0, IN0_COLS, PROJ_PIECE)]
    out_pieces = [slice(c, c + PROJ_PIECE) for c in range(0, D_MODEL, PROJ_PIECE)]

    def project(rows, cs):
        p_ref[rows, cs] = _dot(hb_ref[rows, :], win_ref[:, cs])

    def emit_out(rows, cs):
        o_ref[rows, cs] = x_ref[rows, cs] + g1[:, cs] * _dot(y_ref[rows, :], wout_ref[:, cs])

    def gates_chunk(ci, other_work):
        rows = slice(ci * HG_CHUNK, (ci + 1) * HG_CHUNK)

        def col(name):
            return p_ref[rows, _group(name)]

        z = col("a_c") * col("a_h")
        prev = zc_ref[...]
        zc_ref[...] = z[HG_CHUNK - SUBLANES:, :]

        def shifted(k):
            zr = pltpu.roll(z, k, axis=0)
            pr = pltpu.roll(prev, k, axis=0)
            head = jnp.where(row8 < k, pr, zr[:SUBLANES])
            return jnp.concatenate([head, zr[SUBLANES:]], axis=0)

        conv = cb + shifted(2) * cw0
        conv = conv + shifted(1) * cw1
        conv = conv + z * cw2
        y_ref[rows, 0:A_WIDTH] = (col("a_b") * conv).astype(BF16)
        other_work()

        f = lb + (1.0 - lb) * _sigmoid(col("b_f"))
        cum = _chunk_cumsum(jnp.log(f))
        ck = cum - jnp.log(1.0 - f)
        f_ref[rows, :] = f
        cum_ref[rows, :] = cum
        ck_ref[rows, :] = ck
        mid = _mid_row(cum)
        e = jnp.maximum(mid - ck, -mid)
        worst_ref[...] = e if ci == 0 else jnp.maximum(worst_ref[...], e)
        other_work()


    def scores_stage(ci, fast):
        rows = slice(ci * HG_CHUNK, (ci + 1) * HG_CHUNK)
        q = p_ref[rows, _group("b_q")]
        v = p_ref[rows, _group("b_i")]
        cum = cum_ref[rows, :]
        ck = ck_ref[rows, :]
        last = cum[HG_CHUNK - 1:HG_CHUNK, :]
        q_in = q * jnp.exp(cum)
        k_dec = jnp.exp(last - ck)
        if not fast:
            f = f_ref[rows, :]
            kk = 1.0 - f
            qf = q * f

        heads = []
        for h in range(HG_HEADS):
            hs = slice(h * HG_DK, (h + 1) * HG_DK)
            q_h, cum_h, ck_h = q[:, hs], cum[:, hs], ck[:, hs]
            if fast:
                mid = _mid_row(cum_h)
                qt = (q_h * jnp.exp(cum_h - mid)).astype(BF16)
                kt = jnp.exp(mid - ck_h).astype(BF16)
                parts = [(causal, _dot_nt(qt, kt))]
                levels = []
            else:
                k_b = kk[:, hs].astype(BF16)
                parts = [(diag, _dot_nt(q_h.astype(BF16), k_b))]
                levels = list(zip(HG_LEVELS, masks))
            for m, mask in levels:
                if m == 1:
                    qt, kt = qf[:, hs].astype(BF16), k_b
                else:
                    bnd = _block_boundary(cum_h, m)
                    qt = (q_h * jnp.exp(cum_h - bnd)).astype(BF16)
                    kt = jnp.exp(bnd - ck_h).astype(BF16)
                parts.append((mask, _dot_nt(qt, kt)))
            heads.append(dict(parts=parts, q_in=q_in[:, hs].astype(BF16),
                              k_dec=k_dec[:, hs].astype(BF16), v=v[:, hs].astype(BF16)))
        return dict(rows=rows, heads=heads, dec=jnp.exp(last))

    def state_stage(ctx):
        for h, hd in enumerate(ctx["heads"]):
            hs = slice(h * HG_DK, (h + 1) * HG_DK)
            scores = None
            for mask, s in hd["parts"]:
                scores = jnp.where(mask, s, 0.0 if scores is None else scores)
            st = st_ref[h]
            hd["o"] = _dot(hd["q_in"], st.astype(BF16)) + _dot(scores.astype(BF16), hd["v"])
            dec_rows = jnp.broadcast_to(ctx["dec"][:, hs], (HG_DV, HG_DK)).T
            st_ref[h] = st * dec_rows + _dot_tn(hd["k_dec"], hd["v"])

    def output_stage(ctx):
        rows = ctx["rows"]
        gate = p_ref[rows, _group("b_g")]
        gsil = gate * _sigmoid(gate)
        for h, hd in enumerate(ctx["heads"]):
            hs = slice(h * HG_DK, (h + 1) * HG_DK)
            o = hd["o"]
            o = o * lax.rsqrt(jnp.mean(o * o, axis=-1, keepdims=True) + EPS)
            y_ref[rows, A_WIDTH + h * HG_DV:A_WIDTH + (h + 1) * HG_DV] = (
                o * gain[:, hs] * gsil[:, hs]).astype(BF16)

    chunks_a = range(chunks_per_half)
    chunks_b = range(chunks_per_half, 2 * chunks_per_half)

    def is_gate_col(cs):
        return any(_group(n).start <= cs.start < _group(n).stop for n in ("a_b", "a_c", "a_h", "b_f"))

    gate_pieces = [cs for cs in in_pieces if is_gate_col(cs)]
    rest_pieces = [cs for cs in in_pieces if not is_gate_col(cs)]

    norm_b = _spread([functools.partial(_norm_mod_block, x_ref, hb_ref, gs1, sh1, r)
                      for r in range(half, ts, ROW_BLOCK)], len(gate_pieces))
    for cs in gate_pieces:
        project(rows_a, cs)
        norm_b()

    work = _spread([functools.partial(project, rows_b, cs) for cs in gate_pieces],
                   2 * chunks_per_half)
    for ci in chunks_a:
        gates_chunk(ci, work)

    work = _spread([functools.partial(project, rows, cs)
                    for rows in (rows_a, rows_b) for cs in rest_pieces], 2 * chunks_per_half)
    for ci in chunks_b:
        gates_chunk(ci, work)
    fast_ok = jnp.max(worst_ref[...]) < HG_EXP_GUARD

    def recur_tile(fast):
        n = 2 * chunks_per_half
        ready = []
        ctx = {}

        def other_work():
            if ready:
                ready.pop(0)()

        for t in range(n + 2):
            if t < n:
                ctx[t] = scores_stage(t, fast)
            other_work()
            if 0 <= t - 1 < n:
                state_stage(ctx[t - 1])
            other_work()
            if 0 <= t - 2 < n:
                output_stage(ctx.pop(t - 2))
                if t - 2 == chunks_per_half - 1:
                    ready += [functools.partial(emit_out, rows_a, cs) for cs in out_pieces]
            other_work()
        for item in ready:
            item()

    pl.when(fast_ok)(functools.partial(recur_tile, True))
    pl.when(jnp.logical_not(fast_ok))(functools.partial(recur_tile, False))
    for cs in out_pieces:
        emit_out(rows_b, cs)


def _mix0_call(x, mod, ng, win, cw, cb, hglb, gain, wout, *, ts):
    bsz, seq, d = x.shape
    assert ts % (2 * HG_CHUNK) == 0
    const2 = lambda b, s: (0, 0)
    return pl.pallas_call(
        functools.partial(_mix0_kernel, ts=ts),
        grid=(bsz, seq // ts),
        in_specs=[
            pl.BlockSpec((None, ts, d), lambda b, s: (b, s, 0)),
            pl.BlockSpec((None, 6, d), lambda b, s: (b, 0, 0)),
            pl.BlockSpec((1, d), const2),
            pl.BlockSpec((d, IN0_COLS), const2, pipeline_mode=pl.Buffered(1)),
            pl.BlockSpec((3, A_WIDTH), const2),
            pl.BlockSpec((1, A_WIDTH), const2),
            pl.BlockSpec((3, B_WIDTH), const2),
            pl.BlockSpec((1, B_WIDTH), const2),
            pl.BlockSpec((d, d), const2, pipeline_mode=pl.Buffered(1)),
        ],
        out_specs=pl.BlockSpec((None, ts, d), lambda b, s: (b, s, 0)),
        out_shape=jax.ShapeDtypeStruct((bsz, seq, d), F32),
        scratch_shapes=[
            pltpu.VMEM((ts, d), BF16),
            pltpu.VMEM((ts, IN0_COLS), F32),
            pltpu.VMEM((ts, d), BF16),
            pltpu.VMEM((HG_HEADS, HG_DK, HG_DV), F32),
            pltpu.VMEM((SUBLANES, A_WIDTH), F32),
            pltpu.VMEM((ts, B_WIDTH), F32),
            pltpu.VMEM((ts, B_WIDTH), F32),
            pltpu.VMEM((ts, B_WIDTH), F32),
            pltpu.VMEM((HG_CHUNK, B_WIDTH), F32),
        ],
        compiler_params=pltpu.CompilerParams(
            dimension_semantics=("parallel", "arbitrary"),
            vmem_limit_bytes=VMEM_LIMIT_BYTES),
        name="mixer_conv_hgrn2",
    )(x, mod, ng, win, cw, cb, hglb, gain, wout)


def _mix1_kernel(x_ref, mod_ref, ng_ref, win_ref, bin_ref, lng_ref, lnb_ref, ws_ref,
                 bst_ref, wout_ref, o_ref, hb_ref, z_ref, y_ref, v_ref, *, ts):
    sh1 = mod_ref[0:1, :]
    sc1 = mod_ref[1:2, :]
    g1 = mod_ref[2:3, :]
    gs1 = ng_ref[...] * (1.0 + sc1)
    sub = min(GM_SUB, ts)
    _norm_mod_to(x_ref, hb_ref, gs1, sh1, sub)

    b_in = bin_ref[...]
    ln_g = lng_ref[...]
    ln_b = lnb_ref[...]
    bst = bst_ref[...]
    ti = lax.broadcasted_iota(jnp.int32, (GM_CHUNK, GM_CHUNK), 0)
    si = lax.broadcasted_iota(jnp.int32, (GM_CHUNK, GM_CHUNK), 1)
    w_tri = [jnp.where(si <= ti, ws_ref[g], 0.0).astype(BF16) for g in range(GM_GROUPS)]

    def gelu(t):
        half_t = 0.5 * t
        th = jnp.tanh(t * (_SQRT_2_OVER_PI + (_SQRT_2_OVER_PI * 0.044715) * (t * t)))
        return half_t + half_t * th

    def project(rows, cs):
        z_ref[rows, cs] = _dot(hb_ref[rows, :], win_ref[:, cs])

    def emit_out(rows, cs):
        o_ref[rows, cs] = x_ref[rows, cs] + g1[:, cs] * _dot(y_ref[rows, :], wout_ref[:, cs])

    def gate_chunk(ci, mxu_work):
        rows = slice(ci * GM_CHUNK, (ci + 1) * GM_CHUNK)
        groups = [slice(g * GM_GW, (g + 1) * GM_GW) for g in range(GM_GROUPS)]
        total = None
        for gs in groups:
            vs = slice(GM_WIDTH + gs.start, GM_WIDTH + gs.stop)
            v = gelu(z_ref[rows, vs] + b_in[:, vs])
            v_ref[rows, gs] = v
            part = jnp.sum(v, axis=-1, keepdims=True)
            total = part if total is None else total + part
            mxu_work()
        mu = total * (1.0 / GM_WIDTH)
        total = None
        for gs in groups:
            vc = v_ref[rows, gs] - mu
            v_ref[rows, gs] = vc
            part = jnp.sum(vc * vc, axis=-1, keepdims=True)
            total = part if total is None else total + part
        inv = lax.rsqrt(total * (1.0 / GM_WIDTH) + EPS)
        for g, gs in enumerate(groups):
            vn = ((v_ref[rows, gs] * inv) * ln_g[:, gs] + ln_b[:, gs]).astype(BF16)
            mixed = _dot(w_tri[g], vn) + bst[:, g:g + 1]
            u = gelu(z_ref[rows, gs] + b_in[:, gs])
            y_ref[rows, gs] = (u * mixed).astype(BF16)
            mxu_work()

    n_sub = ts // sub
    chunks_per_sub = sub // GM_CHUNK
    in_pieces = [slice(c, c + PROJ_PIECE) for c in range(0, 2 * GM_WIDTH, PROJ_PIECE)]
    out_pieces = [slice(c, c + PROJ_PIECE) for c in range(0, x_ref.shape[1], PROJ_PIECE)]

    def rows_of(i):
        return slice(i * sub, (i + 1) * sub)

    def side_items(t):
        lists = []
        if t < n_sub:
            lists.append([functools.partial(project, rows_of(t), cs) for cs in in_pieces])
        if t + 1 < n_sub:
            lists.append([functools.partial(_norm_mod_block, x_ref, hb_ref, gs1, sh1, r)
                          for r in range((t + 1) * sub, (t + 2) * sub, ROW_BLOCK)])
        if 0 <= t - 2 < n_sub:
            lists.append([functools.partial(emit_out, rows_of(t - 2), cs) for cs in out_pieces])
        return _interleave(lists)

    for t in range(n_sub + 2):
        items = side_items(t)
        if 0 <= t - 1 < n_sub:
            work = _spread(items, 2 * GM_GROUPS * chunks_per_sub)
            for ci in range((t - 1) * chunks_per_sub, t * chunks_per_sub):
                gate_chunk(ci, work)
        else:
            for item in items:
                item()


def _mix1_call(x, mod, ng, win, b_in, ln_g, ln_b, ws, bst, wout, *, ts):
    bsz, seq, d = x.shape
    const2 = lambda b, s: (0, 0)
    return pl.pallas_call(
        functools.partial(_mix1_kernel, ts=ts),
        grid=(bsz, seq // ts),
        in_specs=[
            pl.BlockSpec((None, ts, d), lambda b, s: (b, s, 0)),
            pl.BlockSpec((None, 6, d), lambda b, s: (b, 0, 0)),
            pl.BlockSpec((1, d), const2),
            pl.BlockSpec((d, 2 * GM_WIDTH), const2, pipeline_mode=pl.Buffered(1)),
            pl.BlockSpec((1, 2 * GM_WIDTH), const2),
            pl.BlockSpec((1, GM_WIDTH), const2),
            pl.BlockSpec((1, GM_WIDTH), const2),
            pl.BlockSpec((GM_GROUPS, GM_CHUNK, GM_CHUNK), lambda b, s: (0, 0, 0)),
            pl.BlockSpec((GM_CHUNK, GM_GROUPS), const2),
            pl.BlockSpec((GM_WIDTH, d), const2, pipeline_mode=pl.Buffered(1)),
        ],
        out_specs=pl.BlockSpec((None, ts, d), lambda b, s: (b, s, 0)),
        out_shape=jax.ShapeDtypeStruct((bsz, seq, d), F32),
        scratch_shapes=[
            pltpu.VMEM((ts, d), BF16),
            pltpu.VMEM((ts, 2 * GM_WIDTH), F32),
            pltpu.VMEM((ts, GM_WIDTH), BF16),
            pltpu.VMEM((ts, GM_WIDTH), F32),
        ],
        compiler_params=pltpu.CompilerParams(
            dimension_semantics=("parallel", "parallel"),
            vmem_limit_bytes=VMEM_LIMIT_BYTES),
        name="mixer_spatial_gating",
    )(x, mod, ng, win, b_in, ln_g, ln_b, ws, bst, wout)


def _ffn_kernel(x_ref, mod_ref, ng_ref, w1_ref, w2_ref, fg_ref, o_ref, hb_ref, *, ts, final):
    sh2 = mod_ref[3:4, :]
    sc2 = mod_ref[4:5, :]
    g2 = mod_ref[5:6, :]
    gs2 = ng_ref[...] * (1.0 + sc2)

    sub = min(FFN_SUB, ts)
    _norm_mod_to(x_ref, hb_ref, gs2, sh2, sub)
    fcols = D_MODEL
    n_f = D_FF // fcols
    for r0 in range(0, ts, sub):
        rows = slice(r0, r0 + sub)
        between = _spread([functools.partial(_norm_mod_block, x_ref, hb_ref, gs2, sh2, r)
                           for r in range(r0 + sub, min(r0 + 2 * sub, ts), ROW_BLOCK)], n_f)
        acc = None
        for j in range(n_f):
            hid = jnp.maximum(_dot(hb_ref[rows, :], w1_ref[:, j * fcols:(j + 1) * fcols]), 0.0)
            between()
            part = _dot((hid * hid).astype(BF16), w2_ref[j * fcols:(j + 1) * fcols, :])
            acc = part if acc is None else acc + part
        out = x_ref[rows, :] + g2 * acc
        if final:
            inv = lax.rsqrt(jnp.mean(out * out, axis=-1, keepdims=True) + EPS)
            out = (out * inv) * fg_ref[...]
        o_ref[rows, :] = out


def _ffn_call(x, mod, ng, w1, w2, fg, *, layer, ts, final):
    bsz, seq, d = x.shape
    const2 = lambda b, s: (0, 0)
    this_layer = lambda b, s: (layer, 0, 0)
    return pl.pallas_call(
        functools.partial(_ffn_kernel, ts=ts, final=final),
        grid=(bsz, seq // ts),
        in_specs=[
            pl.BlockSpec((None, ts, d), lambda b, s: (b, s, 0)),
            pl.BlockSpec((None, None, 6, d), lambda b, s: (layer, b, 0, 0)),
            pl.BlockSpec((1, d), const2),
            pl.BlockSpec((None, d, D_FF), this_layer, pipeline_mode=pl.Buffered(1)),
            pl.BlockSpec((None, D_FF, d), this_layer, pipeline_mode=pl.Buffered(1)),
            pl.BlockSpec((1, d), const2),
        ],
        out_specs=pl.BlockSpec((None, ts, d), lambda b, s: (b, s, 0)),
        out_shape=jax.ShapeDtypeStruct((bsz, seq, d), F32),
        scratch_shapes=[pltpu.VMEM((ts, d), BF16)],
        compiler_params=pltpu.CompilerParams(
            dimension_semantics=("parallel", "parallel"),
            vmem_limit_bytes=VMEM_LIMIT_BYTES),
        name="ffn_final" if final else "ffn",
    )(x, mod, ng, w1, w2, fg)


def kernel(x, c, ada_w, ada_b, norm_mix_g, norm_ffn_g, w_in0, conv_w, conv_b, hg_lb, hg_gain,
           w_out0, w_in1, b_in1, gm_ln_g, gm_ln_b, gm_ws, gm_bs, w_out1, w_ff1, w_ff2, final_g):
    bsz, seq, d = x.shape
    ts = min(SEQ_TILE, seq)
    mod = _ada_call(c, ada_w, ada_b).reshape(ada_w.shape[0], bsz, 6, d)
    fg = final_g.reshape(1, d)

    x = _mix0_call(x, mod[0], norm_mix_g[0:1], w_in0[0].astype(BF16), conv_w[0], conv_b[0:1],
                   hg_lb, hg_gain[0:1], w_out0[0].astype(BF16), ts=ts)
    ts_ffn = min(FFN_TILE, seq)
    w_ff1_b, w_ff2_b = w_ff1.astype(BF16), w_ff2.astype(BF16)
    x = _ffn_call(x, mod, norm_ffn_g[0:1], w_ff1_b, w_ff2_b, fg, layer=0, ts=ts_ffn, final=False)
    x = _mix1_call(x, mod[1], norm_mix_g[1:2], w_in1[0].astype(BF16), b_in1[0:1], gm_ln_g[0:1],
                   gm_ln_b[0:1], gm_ws[0], gm_bs[0].T, w_out1[0].astype(BF16),
                   ts=min(GM_TILE, seq))
    x = _ffn_call(x, mod, norm_ffn_g[1:2], w_ff1_b, w_ff2_b, fg, layer=1, ts=ts_ffn, final=True)
    return x
```

```python
import functools

import numpy as np
import jax
import jax.numpy as jnp
from jax import lax
from jax.experimental import pallas as pl
from jax.experimental.pallas import tpu as pltpu

F32 = jnp.float32
BF16 = jnp.bfloat16

EPS = 1e-6
D_MODEL = 1024
D_FF = 4 * D_MODEL
A_WIDTH = 512
B_WIDTH = 512
HG_HEADS = 4
HG_DK = 128
HG_DV = 128
IN0_COLS = 3 * A_WIDTH + 4 * B_WIDTH
IN0_GROUPS = ("a_b", "a_c", "a_h", "b_q", "b_f", "b_i", "b_g")
GM_WIDTH = D_MODEL
GM_GROUPS = 4
GM_CHUNK = 128
GM_GW = GM_WIDTH // GM_GROUPS

SUBLANES = 8
VMEM_LIMIT_BYTES = 56 * 1024 * 1024

ADA_TN = 1536
SEQ_TILE = 1024
GM_TILE = 1024
GM_SUB = 256
FFN_TILE = 1024
FFN_SUB = 256
ROW_BLOCK = 64
HG_CHUNK = 64
HG_LEVELS = (32, 16, 8, 4, 2, 1)
HG_EXP_GUARD = 80.0
PROJ_PIECE = 256

_SQRT_2_OVER_PI = float(np.sqrt(2.0 / np.pi))


def _group(name):
    assert A_WIDTH == B_WIDTH
    j = IN0_GROUPS.index(name)
    return slice(j * A_WIDTH, (j + 1) * A_WIDTH)


def _sigmoid(v):
    return 1.0 / (1.0 + jnp.exp(-v))


def _dot(a, b):
    return jnp.dot(a, b, preferred_element_type=F32)


def _dot_nt(a, b):
    return lax.dot_general(a, b, (((1,), (1,)), ((), ())), preferred_element_type=F32)


def _dot_tn(a, b):
    return lax.dot_general(a, b, (((0,), (0,)), ((), ())), preferred_element_type=F32)


def _norm_mod_block(x_ref, hb_ref, gs, sh, r):
    xt = x_ref[pl.ds(r, ROW_BLOCK), :]
    inv = lax.rsqrt(jnp.mean(xt * xt, axis=-1, keepdims=True) + EPS)
    hb_ref[pl.ds(r, ROW_BLOCK), :] = ((xt * inv) * gs + sh).astype(BF16)


def _norm_mod_to(x_ref, hb_ref, gs, sh, rows):
    def body(i, carry):
        _norm_mod_block(x_ref, hb_ref, gs, sh, pl.multiple_of(i * ROW_BLOCK, ROW_BLOCK))
        return carry

    lax.fori_loop(0, rows // ROW_BLOCK, body, 0, unroll=True)


def _split_bf16(a):
    hi = a.astype(BF16)
    return hi, (a - hi.astype(F32)).astype(BF16)


def _ada_kernel(c_ref, w_ref, b_ref, o_ref):
    c = c_ref[...]
    bsz = c.shape[0]
    s_hi, s_lo = _split_bf16(c * _sigmoid(c))
    w_hi, w_lo = _split_bf16(w_ref[...])
    both = _dot(jnp.concatenate([s_hi, s_lo], axis=0), w_hi)
    o_ref[...] = both[:bsz] + both[bsz:] + _dot(s_hi, w_lo) + b_ref[...]


def _ada_call(c, ada_w, ada_b):
    depth, d, e = ada_w.shape
    bsz = c.shape[0]
    tn = ADA_TN
    return pl.pallas_call(
        _ada_kernel,
        grid=(depth, e // tn),
        in_specs=[
            pl.BlockSpec((bsz, d), lambda l, j: (0, 0)),
            pl.BlockSpec((None, d, tn), lambda l, j: (l, 0, j)),
            pl.BlockSpec((None, 1, tn), lambda l, j: (l, 0, j)),
        ],
        out_specs=pl.BlockSpec((None, bsz, tn), lambda l, j: (l, 0, j)),
        out_shape=jax.ShapeDtypeStruct((depth, bsz, e), F32),
        compiler_params=pltpu.CompilerParams(
            dimension_semantics=("arbitrary", "arbitrary"),
            vmem_limit_bytes=VMEM_LIMIT_BYTES),
        name="ada_modulation",
    )(c, ada_w, ada_b.reshape(depth, 1, e))


def _level_masks():
    ti = lax.broadcasted_iota(jnp.int32, (HG_CHUNK, HG_CHUNK), 0)
    si = lax.broadcasted_iota(jnp.int32, (HG_CHUNK, HG_CHUNK), 1)
    masks = []
    for m in HG_LEVELS:
        other_blk = (ti ^ si) & ~(2 * m - 1)
        masks.append((other_blk | ((ti & m) ^ m) | (si & m)) == 0)
    return masks, ti == si, si <= ti


def _mid_row(c):
    return c[HG_CHUNK // 2 - 1:HG_CHUNK // 2, :]


def _block_boundary(c, m):
    rows, w = c.shape
    if m >= SUBLANES:
        blk = 2 * m
        return jnp.concatenate(
            [jnp.broadcast_to(c[b * blk + m - 1:b * blk + m, :], (blk, w))
             for b in range(rows // blk)], axis=0)
    c3 = c.reshape(rows // SUBLANES, SUBLANES, w)
    if m == 4:
        bnd = jnp.broadcast_to(c3[:, 3:4, :], c3.shape)
    else:
        assert m == 2
        sub = lax.broadcasted_iota(jnp.int32, c3.shape, 1)
        bnd = jnp.where(sub < 4, jnp.broadcast_to(c3[:, 1:2, :], c3.shape),
                        jnp.broadcast_to(c3[:, 5:6, :], c3.shape))
    return bnd.reshape(rows, w)


def _chunk_cumsum(x):
    rows, w = x.shape
    nv = rows // SUBLANES
    x3 = x.reshape(nv, SUBLANES, w)
    sub = lax.broadcasted_iota(jnp.int32, x3.shape, 1)
    sh = 1
    while sh < SUBLANES:
        x3 = x3 + jnp.where(sub >= sh, pltpu.roll(x3, sh, axis=1), 0.0)
        sh *= 2
    tot = jnp.broadcast_to(x3[:, SUBLANES - 1:SUBLANES, :], x3.shape)
    outs = [x3[0]]
    acc = tot[0]
    for j in range(1, nv):
        outs.append(x3[j] + acc)
        if j + 1 < nv:
            acc = acc + tot[j]
    return jnp.concatenate(outs, axis=0)


def _spread(items, slots):
    state = {"calls": 0, "done": 0}

    def run_due():
        state["calls"] += 1
        want = min(len(items), -(-len(items) * state["calls"] // slots))
        while state["done"] < want:
            items[state["done"]]()
            state["done"] += 1
    return run_due


def _interleave(lists):
    keyed = [((i + 0.5) / len(l), n, item) for n, l in enumerate(lists) for i, item in enumerate(l)]
    return [item for _, _, item in sorted(keyed, key=lambda k: k[:2])]


def _mix0_kernel(x_ref, mod_ref, ng_ref, win_ref, cw_ref, cb_ref, hglb_ref, gain_ref, wout_ref,
                 o_ref, hb_ref, p_ref, y_ref, st_ref, zc_ref, cum_ref, ck_ref, f_ref, worst_ref,
                 *, ts):
    @pl.when(pl.program_id(1) == 0)
    def _():
        st_ref[...] = jnp.zeros_like(st_ref)
        zc_ref[...] = jnp.zeros_like(zc_ref)

    half = ts // 2
    rows_a, rows_b = slice(0, half), slice(half, ts)
    chunks_per_half = half // HG_CHUNK
    sh1 = mod_ref[0:1, :]
    gs1 = ng_ref[...] * (1.0 + mod_ref[1:2, :])
    g1 = mod_ref[2:3, :]
    _norm_mod_to(x_ref, hb_ref, gs1, sh1, half)

    a0, a1, a2 = hglb_ref[0:1, :], hglb_ref[1:2, :], hglb_ref[2:3, :]
    amax = jnp.maximum(jnp.maximum(a0, a1), a2)
    e0, e1, e2 = jnp.exp(a0 - amax), jnp.exp(a1 - amax), jnp.exp(a2 - amax)
    lb = e0 / (e0 + e1 + e2)

    cw0, cw1, cw2 = cw_ref[0:1, :], cw_ref[1:2, :], cw_ref[2:3, :]
    cb = cb_ref[...]
    gain = gain_ref[...]
    row8 = lax.broadcasted_iota(jnp.int32, (SUBLANES, A_WIDTH), 0)
    masks, diag, causal = _level_masks()

    in_pieces = [slice(c, c + PROJ_PIECE) for c in range(0, IN0_COLS, PROJ_PIECE)]
    out_pieces = [slice(c, c + PROJ_PIECE) for c in range(0, D_MODEL, PROJ_PIECE)]

    def project(rows, cs):
        p_ref[rows, cs] = _dot(hb_ref[rows, :], win_ref[:, cs])

    def emit_out(rows, cs):
        o_ref[rows, cs] = x_ref[rows, cs] + g1[:, cs] * _dot(y_ref[rows, :], wout_ref[:, cs])

    def gates_chunk(ci, other_work):
        rows = slice(ci * HG_CHUNK, (ci + 1) * HG_CHUNK)

        def col(name):
            return p_ref[rows, _group(name)]

        z = col("a_c") * col("a_h")
        prev = zc_ref[...]
        zc_ref[...] = z[HG_CHUNK - SUBLANES:, :]

        def shifted(k):
            zr = pltpu.roll(z, k, axis=0)
            pr = pltpu.roll(prev, k, axis=0)
            head = jnp.where(row8 < k, pr, zr[:SUBLANES])
            return jnp.concatenate([head, zr[SUBLANES:]], axis=0)

        conv = cb + shifted(2) * cw0
        conv = conv + shifted(1) * cw1
        conv = conv + z * cw2
        y_ref[rows, 0:A_WIDTH] = (col("a_b") * conv).astype(BF16)
        other_work()

        f = lb + (1.0 - lb) * _sigmoid(col("b_f"))
        cum = _chunk_cumsum(jnp.log(f))
        ck = cum - jnp.log(1.0 - f)
        f_ref[rows, :] = f
        cum_ref[rows, :] = cum
        ck_ref[rows, :] = ck
        mid = _mid_row(cum)
        e = jnp.maximum(mid - ck, -mid)
        worst_ref[...] = e if ci == 0 else jnp.maximum(worst_ref[...], e)
        other_work()


    def scores_stage(ci, fast):
        rows = slice(ci * HG_CHUNK, (ci + 1) * HG_CHUNK)
        q = p_ref[rows, _group("b_q")]
        v = p_ref[rows, _group("b_i")]
        cum = cum_ref[rows, :]
        ck = ck_ref[rows, :]
        last = cum[HG_CHUNK - 1:HG_CHUNK, :]
        q_in = q * jnp.exp(cum)
        k_dec = jnp.exp(last - ck)
        if not fast:
            f = f_ref[rows, :]
            kk = 1.0 - f
            qf = q * f

        heads = []
        for h in range(HG_HEADS):
            hs = slice(h * HG_DK, (h + 1) * HG_DK)
            q_h, cum_h, ck_h = q[:, hs], cum[:, hs], ck[:, hs]
            if fast:
                mid = _mid_row(cum_h)
                qt = (q_h * jnp.exp(cum_h - mid)).astype(BF16)
                kt = jnp.exp(mid - ck_h).astype(BF16)
                parts = [(causal, _dot_nt(qt, kt))]
                levels = []
            else:
                k_b = kk[:, hs].astype(BF16)
                parts = [(diag, _dot_nt(q_h.astype(BF16), k_b))]
                levels = list(zip(HG_LEVELS, masks))
            for m, mask in levels:
                if m == 1:
                    qt, kt = qf[:, hs].astype(BF16), k_b
                else:
                    bnd = _block_boundary(cum_h, m)
                    qt = (q_h * jnp.exp(cum_h - bnd)).astype(BF16)
                    kt = jnp.exp(bnd - ck_h).astype(BF16)
                parts.append((mask, _dot_nt(qt, kt)))
            heads.append(dict(parts=parts, q_in=q_in[:, hs].astype(BF16),
                              k_dec=k_dec[:, hs].astype(BF16), v=v[:, hs].astype(BF16)))
        return dict(rows=rows, heads=heads, dec=jnp.exp(last))

    def state_stage(ctx):
        for h, hd in enumerate(ctx["heads"]):
            hs = slice(h * HG_DK, (h + 1) * HG_DK)
            scores = None
            for mask, s in hd["parts"]:
                scores = jnp.where(mask, s, 0.0 if scores is None else scores)
            st = st_ref[h]
            hd["o"] = _dot(hd["q_in"], st.astype(BF16)) + _dot(scores.astype(BF16), hd["v"])
            dec_rows = jnp.broadcast_to(ctx["dec"][:, hs], (HG_DV, HG_DK)).T
            st_ref[h] = st * dec_rows + _dot_tn(hd["k_dec"], hd["v"])

    def output_stage(ctx):
        rows = ctx["rows"]
        gate = p_ref[rows, _group("b_g")]
        gsil = gate * _sigmoid(gate)
        for h, hd in enumerate(ctx["heads"]):
            hs = slice(h * HG_DK, (h + 1) * HG_DK)
            o = hd["o"]
            o = o * lax.rsqrt(jnp.mean(o * o, axis=-1, keepdims=True) + EPS)
            y_ref[rows, A_WIDTH + h * HG_DV:A_WIDTH + (h + 1) * HG_DV] = (
                o * gain[:, hs] * gsil[:, hs]).astype(BF16)

    chunks_a = range(chunks_per_half)
    chunks_b = range(chunks_per_half, 2 * chunks_per_half)

    def is_gate_col(cs):
        return any(_group(n).start <= cs.start < _group(n).stop for n in ("a_b", "a_c", "a_h", "b_f"))

    gate_pieces = [cs for cs in in_pieces if is_gate_col(cs)]
    rest_pieces = [cs for cs in in_pieces if not is_gate_col(cs)]

    norm_b = _spread([functools.partial(_norm_mod_block, x_ref, hb_ref, gs1, sh1, r)
                      for r in range(half, ts, ROW_BLOCK)], len(gate_pieces))
    for cs in gate_pieces:
        project(rows_a, cs)
        norm_b()

    work = _spread([functools.partial(project, rows_b, cs) for cs in gate_pieces],
                   2 * chunks_per_half)
    for ci in chunks_a:
        gates_chunk(ci, work)

    work = _spread([functools.partial(project, rows, cs)
                    for rows in (rows_a, rows_b) for cs in rest_pieces], 2 * chunks_per_half)
    for ci in chunks_b:
        gates_chunk(ci, work)
    fast_ok = jnp.max(worst_ref[...]) < HG_EXP_GUARD

    def recur_tile(fast):
        n = 2 * chunks_per_half
        ready = []
        ctx = {}

        def other_work():
            if ready:
                ready.pop(0)()

        for t in range(n + 2):
            if t < n:
                ctx[t] = scores_stage(t, fast)
            other_work()
            if 0 <= t - 1 < n:
                state_stage(ctx[t - 1])
            other_work()
            if 0 <= t - 2 < n:
                output_stage(ctx.pop(t - 2))
                if t - 2 == chunks_per_half - 1:
                    ready += [functools.partial(emit_out, rows_a, cs) for cs in out_pieces]
            other_work()
        for item in ready:
            item()

    pl.when(fast_ok)(functools.partial(recur_tile, True))
    pl.when(jnp.logical_not(fast_ok))(functools.partial(recur_tile, False))
    for cs in out_pieces:
        emit_out(rows_b, cs)


def _mix0_call(x, mod, ng, win, cw, cb, hglb, gain, wout, *, ts):
    bsz, seq, d = x.shape
    assert ts % (2 * HG_CHUNK) == 0
    const2 = lambda b, s: (0, 0)
    return pl.pallas_call(
        functools.partial(_mix0_kernel, ts=ts),
        grid=(bsz, seq // ts),
        in_specs=[
            pl.BlockSpec((None, ts, d), lambda b, s: (b, s, 0)),
            pl.BlockSpec((None, 6, d), lambda b, s: (b, 0, 0)),
            pl.BlockSpec((1, d), const2),
            pl.BlockSpec((d, IN0_COLS), const2, pipeline_mode=pl.Buffered(1)),
            pl.BlockSpec((3, A_WIDTH), const2),
            pl.BlockSpec((1, A_WIDTH), const2),
            pl.BlockSpec((3, B_WIDTH), const2),
            pl.BlockSpec((1, B_WIDTH), const2),
            pl.BlockSpec((d, d), const2, pipeline_mode=pl.Buffered(1)),
        ],
        out_specs=pl.BlockSpec((None, ts, d), lambda b, s: (b, s, 0)),
        out_shape=jax.ShapeDtypeStruct((bsz, seq, d), F32),
        scratch_shapes=[
            pltpu.VMEM((ts, d), BF16),
            pltpu.VMEM((ts, IN0_COLS), F32),
            pltpu.VMEM((ts, d), BF16),
            pltpu.VMEM((HG_HEADS, HG_DK, HG_DV), F32),
            pltpu.VMEM((SUBLANES, A_WIDTH), F32),
            pltpu.VMEM((ts, B_WIDTH), F32),
            pltpu.VMEM((ts, B_WIDTH), F32),
            pltpu.VMEM((ts, B_WIDTH), F32),
            pltpu.VMEM((HG_CHUNK, B_WIDTH), F32),
        ],
        compiler_params=pltpu.CompilerParams(
            dimension_semantics=("parallel", "arbitrary"),
            vmem_limit_bytes=VMEM_LIMIT_BYTES),
        name="mixer_conv_hgrn2",
    )(x, mod, ng, win, cw, cb, hglb, gain, wout)


def _mix1_kernel(x_ref, mod_ref, ng_ref, win_ref, bin_ref, lng_ref, lnb_ref, ws_ref,
                 bst_ref, wout_ref, o_ref, hb_ref, z_ref, y_ref, v_ref, *, ts):
    sh1 = mod_ref[0:1, :]
    sc1 = mod_ref[1:2, :]
    g1 = mod_ref[2:3, :]
    gs1 = ng_ref[...] * (1.0 + sc1)
    sub = min(GM_SUB, ts)
    _norm_mod_to(x_ref, hb_ref, gs1, sh1, sub)

    b_in = bin_ref[...]
    ln_g = lng_ref[...]
    ln_b = lnb_ref[...]
    bst = bst_ref[...]
    ti = lax.broadcasted_iota(jnp.int32, (GM_CHUNK, GM_CHUNK), 0)
    si = lax.broadcasted_iota(jnp.int32, (GM_CHUNK, GM_CHUNK), 1)
    w_tri = [jnp.where(si <= ti, ws_ref[g], 0.0).astype(BF16) for g in range(GM_GROUPS)]

    def gelu(t):
        half_t = 0.5 * t
        th = jnp.tanh(t * (_SQRT_2_OVER_PI + (_SQRT_2_OVER_PI * 0.044715) * (t * t)))
        return half_t + half_t * th

    def project(rows, cs):
        z_ref[rows, cs] = _dot(hb_ref[rows, :], win_ref[:, cs])

    def emit_out(rows, cs):
        o_ref[rows, cs] = x_ref[rows, cs] + g1[:, cs] * _dot(y_ref[rows, :], wout_ref[:, cs])

    def gate_chunk(ci, mxu_work):
        rows = slice(ci * GM_CHUNK, (ci + 1) * GM_CHUNK)
        groups = [slice(g * GM_GW, (g + 1) * GM_GW) for g in range(GM_GROUPS)]
        total = None
        for gs in groups:
            vs = slice(GM_WIDTH + gs.start, GM_WIDTH + gs.stop)
            v = gelu(z_ref[rows, vs] + b_in[:, vs])
            v_ref[rows, gs] = v
            part = jnp.sum(v, axis=-1, keepdims=True)
            total = part if total is None else total + part
            mxu_work()
        mu = total * (1.0 / GM_WIDTH)
        total = None
        for gs in groups:
            vc = v_ref[rows, gs] - mu
            v_ref[rows, gs] = vc
            part = jnp.sum(vc * vc, axis=-1, keepdims=True)
            total = part if total is None else total + part
        inv = lax.rsqrt(total * (1.0 / GM_WIDTH) + EPS)
        for g, gs in enumerate(groups):
            vn = ((v_ref[rows, gs] * inv) * ln_g[:, gs] + ln_b[:, gs]).astype(BF16)
            mixed = _dot(w_tri[g], vn) + bst[:, g:g + 1]
            u = gelu(z_ref[rows, gs] + b_in[:, gs])
            y_ref[rows, gs] = (u * mixed).astype(BF16)
            mxu_work()

    n_sub = ts // sub
    chunks_per_sub = sub // GM_CHUNK
    in_pieces = [slice(c, c + PROJ_PIECE) for c in range(0, 2 * GM_WIDTH, PROJ_PIECE)]
    out_pieces = [slice(c, c + PROJ_PIECE) for c in range(0, x_ref.shape[1], PROJ_PIECE)]

    def rows_of(i):
        return slice(i * sub, (i + 1) * sub)

    def side_items(t):
        lists = []
        if t < n_sub:
            lists.append([functools.partial(project, rows_of(t), cs) for cs in in_pieces])
        if t + 1 < n_sub:
            lists.append([functools.partial(_norm_mod_block, x_ref, hb_ref, gs1, sh1, r)
                          for r in range((t + 1) * sub, (t + 2) * sub, ROW_BLOCK)])
        if 0 <= t - 2 < n_sub:
            lists.append([functools.partial(emit_out, rows_of(t - 2), cs) for cs in out_pieces])
        return _interleave(lists)

    for t in range(n_sub + 2):
        items = side_items(t)
        if 0 <= t - 1 < n_sub:
            work = _spread(items, 2 * GM_GROUPS * chunks_per_sub)
            for ci in range((t - 1) * chunks_per_sub, t * chunks_per_sub):
                gate_chunk(ci, work)
        else:
            for item in items:
                item()


def _mix1_call(x, mod, ng, win, b_in, ln_g, ln_b, ws, bst, wout, *, ts):
    bsz, seq, d = x.shape
    const2 = lambda b, s: (0, 0)
    return pl.pallas_call(
        functools.partial(_mix1_kernel, ts=ts),
        grid=(bsz, seq // ts),
        in_specs=[
            pl.BlockSpec((None, ts, d), lambda b, s: (b, s, 0)),
            pl.BlockSpec((None, 6, d), lambda b, s: (b, 0, 0)),
            pl.BlockSpec((1, d), const2),
            pl.BlockSpec((d, 2 * GM_WIDTH), const2, pipeline_mode=pl.Buffered(1)),
            pl.BlockSpec((1, 2 * GM_WIDTH), const2),
            pl.BlockSpec((1, GM_WIDTH), const2),
            pl.BlockSpec((1, GM_WIDTH), const2),
            pl.BlockSpec((GM_GROUPS, GM_CHUNK, GM_CHUNK), lambda b, s: (0, 0, 0)),
            pl.BlockSpec((GM_CHUNK, GM_GROUPS), const2),
            pl.BlockSpec((GM_WIDTH, d), const2, pipeline_mode=pl.Buffered(1)),
        ],
        out_specs=pl.BlockSpec((None, ts, d), lambda b, s: (b, s, 0)),
        out_shape=jax.ShapeDtypeStruct((bsz, seq, d), F32),
        scratch_shapes=[
            pltpu.VMEM((ts, d), BF16),
            pltpu.VMEM((ts, 2 * GM_WIDTH), F32),
            pltpu.VMEM((ts, GM_WIDTH), BF16),
            pltpu.VMEM((ts, GM_WIDTH), F32),
        ],
        compiler_params=pltpu.CompilerParams(
            dimension_semantics=("parallel", "parallel"),
            vmem_limit_bytes=VMEM_LIMIT_BYTES),
        name="mixer_spatial_gating",
    )(x, mod, ng, win, b_in, ln_g, ln_b, ws, bst, wout)


def _ffn_kernel(x_ref, mod_ref, ng_ref, w1_ref, w2_ref, fg_ref, o_ref, hb_ref, *, ts, final):
    sh2 = mod_ref[3:4, :]
    sc2 = mod_ref[4:5, :]
    g2 = mod_ref[5:6, :]
    gs2 = ng_ref[...] * (1.0 + sc2)

    sub = min(FFN_SUB, ts)
    _norm_mod_to(x_ref, hb_ref, gs2, sh2, sub)
    fcols = D_MODEL
    n_f = D_FF // fcols
    for r0 in range(0, ts, sub):
        rows = slice(r0, r0 + sub)
        between = _spread([functools.partial(_norm_mod_block, x_ref, hb_ref, gs2, sh2, r)
                           for r in range(r0 + sub, min(r0 + 2 * sub, ts), ROW_BLOCK)], n_f)
        acc = None
        for j in range(n_f):
            hid = jnp.maximum(_dot(hb_ref[rows, :], w1_ref[:, j * fcols:(j + 1) * fcols]), 0.0)
            between()
            part = _dot((hid * hid).astype(BF16), w2_ref[j * fcols:(j + 1) * fcols, :])
            acc = part if acc is None else acc + part
        out = x_ref[rows, :] + g2 * acc
        if final:
            inv = lax.rsqrt(jnp.mean(out * out, axis=-1, keepdims=True) + EPS)
            out = (out * inv) * fg_ref[...]
        o_ref[rows, :] = out


def _ffn_call(x, mod, ng, w1, w2, fg, *, ts, final):
    bsz, seq, d = x.shape
    const2 = lambda b, s: (0, 0)
    return pl.pallas_call(
        functools.partial(_ffn_kernel, ts=ts, final=final),
        grid=(bsz, seq // ts),
        in_specs=[
            pl.BlockSpec((None, ts, d), lambda b, s: (b, s, 0)),
            pl.BlockSpec((None, 6, d), lambda b, s: (b, 0, 0)),
            pl.BlockSpec((1, d), const2),
            pl.BlockSpec((d, D_FF), const2, pipeline_mode=pl.Buffered(1)),
            pl.BlockSpec((D_FF, d), const2, pipeline_mode=pl.Buffered(1)),
            pl.BlockSpec((1, d), const2),
        ],
        out_specs=pl.BlockSpec((None, ts, d), lambda b, s: (b, s, 0)),
        out_shape=jax.ShapeDtypeStruct((bsz, seq, d), F32),
        scratch_shapes=[pltpu.VMEM((ts, d), BF16)],
        compiler_params=pltpu.CompilerParams(
            dimension_semantics=("parallel", "parallel"),
            vmem_limit_bytes=VMEM_LIMIT_BYTES),
        name="ffn_final" if final else "ffn",
    )(x, mod, ng, w1, w2, fg)


def kernel(x, c, ada_w, ada_b, norm_mix_g, norm_ffn_g, w_in0, conv_w, conv_b, hg_lb, hg_gain,
           w_out0, w_in1, b_in1, gm_ln_g, gm_ln_b, gm_ws, gm_bs, w_out1, w_ff1, w_ff2, final_g):
    bsz, seq, d = x.shape
    ts = min(SEQ_TILE, seq)
    mod = _ada_call(c, ada_w, ada_b).reshape(ada_w.shape[0], bsz, 6, d)
    fg = final_g.reshape(1, d)

    x = _mix0_call(x, mod[0], norm_mix_g[0:1], w_in0[0].astype(BF16), conv_w[0], conv_b[0:1],
                   hg_lb, hg_gain[0:1], w_out0[0].astype(BF16), ts=ts)
    ts_ffn = min(FFN_TILE, seq)
    x = _ffn_call(x, mod[0], norm_ffn_g[0:1], w_ff1[0].astype(BF16), w_ff2[0].astype(BF16), fg,
                  ts=ts_ffn, final=False)
    x = _mix1_call(x, mod[1], norm_mix_g[1:2], w_in1[0].astype(BF16), b_in1[0:1], gm_ln_g[0:1],
                   gm_ln_b[0:1], gm_ws[0], gm_bs[0].T, w_out1[0].astype(BF16),
                   ts=min(GM_TILE, seq))
    x = _ffn_call(x, mod[1], norm_ffn_g[1:2], w_ff1[1].astype(BF16), w_ff2[1].astype(BF16), fg,
                  ts=ts_ffn, final=True)
    return x
```

```python
import functools

import numpy as np
import jax
import jax.numpy as jnp
from jax import lax
from jax.experimental import pallas as pl
from jax.experimental.pallas import tpu as pltpu

F32 = jnp.float32
BF16 = jnp.bfloat16

EPS = 1e-6
D_MODEL = 1024
D_FF = 4 * D_MODEL
A_WIDTH = 512
B_WIDTH = 512
HG_HEADS = 4
HG_DK = 128
HG_DV = 128
IN0_COLS = 3 * A_WIDTH + 4 * B_WIDTH
IN0_GROUPS = ("a_b", "a_c", "a_h", "b_q", "b_f", "b_i", "b_g")
GM_WIDTH = D_MODEL
GM_GROUPS = 4
GM_CHUNK = 128
GM_GW = GM_WIDTH // GM_GROUPS

SUBLANES = 8
VMEM_LIMIT_BYTES = 56 * 1024 * 1024

ADA_TN = 1536
SEQ_TILE = 1024
GM_TILE = 1024
GM_SUB = 256
FFN_TILE = 1024
FFN_SUB = 256
ROW_BLOCK = 64
HG_CHUNK = 64
HG_LEVELS = (32, 16, 8, 4, 2, 1)
HG_EXP_GUARD = 80.0
PROJ_PIECE = 256

_SQRT_2_OVER_PI = float(np.sqrt(2.0 / np.pi))


def _group(name):
    assert A_WIDTH == B_WIDTH
    j = IN0_GROUPS.index(name)
    return slice(j * A_WIDTH, (j + 1) * A_WIDTH)


def _sigmoid(v):
    return 1.0 / (1.0 + jnp.exp(-v))


def _dot(a, b):
    return jnp.dot(a, b, preferred_element_type=F32)


def _dot_nt(a, b):
    return lax.dot_general(a, b, (((1,), (1,)), ((), ())), preferred_element_type=F32)


def _dot_tn(a, b):
    return lax.dot_general(a, b, (((0,), (0,)), ((), ())), preferred_element_type=F32)


def _norm_mod_block(x_ref, hb_ref, gs, sh, r):
    xt = x_ref[pl.ds(r, ROW_BLOCK), :]
    inv = lax.rsqrt(jnp.mean(xt * xt, axis=-1, keepdims=True) + EPS)
    hb_ref[pl.ds(r, ROW_BLOCK), :] = ((xt * inv) * gs + sh).astype(BF16)


def _norm_mod_to(x_ref, hb_ref, gs, sh, rows):
    def body(i, carry):
        _norm_mod_block(x_ref, hb_ref, gs, sh, pl.multiple_of(i * ROW_BLOCK, ROW_BLOCK))
        return carry

    lax.fori_loop(0, rows // ROW_BLOCK, body, 0, unroll=True)


def _split_bf16(a):
    hi = a.astype(BF16)
    return hi, (a - hi.astype(F32)).astype(BF16)


def _ada_kernel(c_ref, w_ref, b_ref, o_ref):
    c = c_ref[...]
    bsz = c.shape[0]
    s_hi, s_lo = _split_bf16(c * _sigmoid(c))
    w_hi, w_lo = _split_bf16(w_ref[...])
    both = _dot(jnp.concatenate([s_hi, s_lo], axis=0), w_hi)
    o_ref[...] = both[:bsz] + both[bsz:] + _dot(s_hi, w_lo) + b_ref[...]


def _ada_call(c, ada_w, ada_b):
    depth, d, e = ada_w.shape
    bsz = c.shape[0]
    tn = ADA_TN
    return pl.pallas_call(
        _ada_kernel,
        grid=(depth, e // tn),
        in_specs=[
            pl.BlockSpec((bsz, d), lambda l, j: (0, 0)),
            pl.BlockSpec((None, d, tn), lambda l, j: (l, 0, j)),
            pl.BlockSpec((None, 1, tn), lambda l, j: (l, 0, j)),
        ],
        out_specs=pl.BlockSpec((None, bsz, tn), lambda l, j: (l, 0, j)),
        out_shape=jax.ShapeDtypeStruct((depth, bsz, e), F32),
        compiler_params=pltpu.CompilerParams(
            dimension_semantics=("arbitrary", "arbitrary"),
            vmem_limit_bytes=VMEM_LIMIT_BYTES),
        name="ada_modulation",
    )(c, ada_w, ada_b.reshape(depth, 1, e))


def _level_masks():
    ti = lax.broadcasted_iota(jnp.int32, (HG_CHUNK, HG_CHUNK), 0)
    si = lax.broadcasted_iota(jnp.int32, (HG_CHUNK, HG_CHUNK), 1)
    masks = []
    for m in HG_LEVELS:
        other_blk = (ti ^ si) & ~(2 * m - 1)
        masks.append((other_blk | ((ti & m) ^ m) | (si & m)) == 0)
    return masks, ti == si, si <= ti


def _mid_row(c):
    return c[HG_CHUNK // 2 - 1:HG_CHUNK // 2, :]


def _block_boundary(c, m):
    rows, w = c.shape
    if m >= SUBLANES:
        blk = 2 * m
        return jnp.concatenate(
            [jnp.broadcast_to(c[b * blk + m - 1:b * blk + m, :], (blk, w))
             for b in range(rows // blk)], axis=0)
    c3 = c.reshape(rows // SUBLANES, SUBLANES, w)
    if m == 4:
        bnd = jnp.broadcast_to(c3[:, 3:4, :], c3.shape)
    else:
        assert m == 2
        sub = lax.broadcasted_iota(jnp.int32, c3.shape, 1)
        bnd = jnp.where(sub < 4, jnp.broadcast_to(c3[:, 1:2, :], c3.shape),
                        jnp.broadcast_to(c3[:, 5:6, :], c3.shape))
    return bnd.reshape(rows, w)


def _chunk_cumsum(x):
    rows, w = x.shape
    nv = rows // SUBLANES
    x3 = x.reshape(nv, SUBLANES, w)
    sub = lax.broadcasted_iota(jnp.int32, x3.shape, 1)
    sh = 1
    while sh < SUBLANES:
        x3 = x3 + jnp.where(sub >= sh, pltpu.roll(x3, sh, axis=1), 0.0)
        sh *= 2
    tot = jnp.broadcast_to(x3[:, SUBLANES - 1:SUBLANES, :], x3.shape)
    outs = [x3[0]]
    acc = tot[0]
    for j in range(1, nv):
        outs.append(x3[j] + acc)
        if j + 1 < nv:
            acc = acc + tot[j]
    return jnp.concatenate(outs, axis=0)


def _spread(items, slots):
    state = {"calls": 0, "done": 0}

    def run_due():
        state["calls"] += 1
        want = min(len(items), -(-len(items) * state["calls"] // slots))
        while state["done"] < want:
            items[state["done"]]()
            state["done"] += 1
    return run_due


def _interleave(lists):
    keyed = [((i + 0.5) / len(l), n, item) for n, l in enumerate(lists) for i, item in enumerate(l)]
    return [item for _, _, item in sorted(keyed, key=lambda k: k[:2])]


def _mix0_kernel(x_ref, mod_ref, ng_ref, win_ref, cw_ref, cb_ref, hglb_ref, gain_ref, wout_ref,
                 o_ref, hb_ref, p_ref, y_ref, st_ref, zc_ref, cum_ref, ck_ref, f_ref, worst_ref,
                 *, ts):
    @pl.when(pl.program_id(1) == 0)
    def _():
        st_ref[...] = jnp.zeros_like(st_ref)
        zc_ref[...] = jnp.zeros_like(zc_ref)

    half = ts // 2
    rows_a, rows_b = slice(0, half), slice(half, ts)
    chunks_per_half = half // HG_CHUNK
    sh1 = mod_ref[0:1, :]
    gs1 = ng_ref[...] * (1.0 + mod_ref[1:2, :])
    g1 = mod_ref[2:3, :]
    _norm_mod_to(x_ref, hb_ref, gs1, sh1, half)

    a0, a1, a2 = hglb_ref[0:1, :], hglb_ref[1:2, :], hglb_ref[2:3, :]
    amax = jnp.maximum(jnp.maximum(a0, a1), a2)
    e0, e1, e2 = jnp.exp(a0 - amax), jnp.exp(a1 - amax), jnp.exp(a2 - amax)
    lb = e0 / (e0 + e1 + e2)

    cw0, cw1, cw2 = cw_ref[0:1, :], cw_ref[1:2, :], cw_ref[2:3, :]
    cb = cb_ref[...]
    gain = gain_ref[...]
    row8 = lax.broadcasted_iota(jnp.int32, (SUBLANES, A_WIDTH), 0)
    masks, diag, causal = _level_masks()

    in_pieces = [slice(c, c + PROJ_PIECE) for c in range(0, IN0_COLS, PROJ_PIECE)]
    out_pieces = [slice(c, c + PROJ_PIECE) for c in range(0, D_MODEL, PROJ_PIECE)]

    def project(rows, cs):
        p_ref[rows, cs] = _dot(hb_ref[rows, :], win_ref[:, cs])

    def emit_out(rows, cs):
        o_ref[rows, cs] = x_ref[rows, cs] + g1[:, cs] * _dot(y_ref[rows, :], wout_ref[:, cs])

    def gates_chunk(ci, other_work):
        rows = slice(ci * HG_CHUNK, (ci + 1) * HG_CHUNK)

        def col(name):
            return p_ref[rows, _group(name)]

        z = col("a_c") * col("a_h")
        prev = zc_ref[...]
        zc_ref[...] = z[HG_CHUNK - SUBLANES:, :]

        def shifted(k):
            zr = pltpu.roll(z, k, axis=0)
            pr = pltpu.roll(prev, k, axis=0)
            head = jnp.where(row8 < k, pr, zr[:SUBLANES])
            return jnp.concatenate([head, zr[SUBLANES:]], axis=0)

        conv = cb + shifted(2) * cw0
        conv = conv + shifted(1) * cw1
        conv = conv + z * cw2
        y_ref[rows, 0:A_WIDTH] = (col("a_b") * conv).astype(BF16)
        other_work()

        f = lb + (1.0 - lb) * _sigmoid(col("b_f"))
        cum = _chunk_cumsum(jnp.log(f))
        ck = cum - jnp.log(1.0 - f)
        f_ref[rows, :] = f
        cum_ref[rows, :] = cum
        ck_ref[rows, :] = ck
        mid = _mid_row(cum)
        e = jnp.maximum(mid - ck, -mid)
        worst_ref[...] = e if ci == 0 else jnp.maximum(worst_ref[...], e)
        other_work()


    def scores_stage(ci, fast):
        rows = slice(ci * HG_CHUNK, (ci + 1) * HG_CHUNK)
        q = p_ref[rows, _group("b_q")]
        v = p_ref[rows, _group("b_i")]
        cum = cum_ref[rows, :]
        ck = ck_ref[rows, :]
        last = cum[HG_CHUNK - 1:HG_CHUNK, :]
        q_in = q * jnp.exp(cum)
        k_dec = jnp.exp(last - ck)
        if not fast:
            f = f_ref[rows, :]
            kk = 1.0 - f
            qf = q * f

        heads = []
        for h in range(HG_HEADS):
            hs = slice(h * HG_DK, (h + 1) * HG_DK)
            q_h, cum_h, ck_h = q[:, hs], cum[:, hs], ck[:, hs]
            if fast:
                mid = _mid_row(cum_h)
                qt = (q_h * jnp.exp(cum_h - mid)).astype(BF16)
                kt = jnp.exp(mid - ck_h).astype(BF16)
                parts = [(causal, _dot_nt(qt, kt))]
                levels = []
            else:
                k_b = kk[:, hs].astype(BF16)
                parts = [(diag, _dot_nt(q_h.astype(BF16), k_b))]
                levels = list(zip(HG_LEVELS, masks))
            for m, mask in levels:
                if m == 1:
                    qt, kt = qf[:, hs].astype(BF16), k_b
                else:
                    bnd = _block_boundary(cum_h, m)
                    qt = (q_h * jnp.exp(cum_h - bnd)).astype(BF16)
                    kt = jnp.exp(bnd - ck_h).astype(BF16)
                parts.append((mask, _dot_nt(qt, kt)))
            heads.append(dict(parts=parts, q_in=q_in[:, hs].astype(BF16),
                              k_dec=k_dec[:, hs].astype(BF16), v=v[:, hs].astype(BF16)))
        return dict(rows=rows, heads=heads, dec=jnp.exp(last))

    def state_stage(ctx):
        for h, hd in enumerate(ctx["heads"]):
            hs = slice(h * HG_DK, (h + 1) * HG_DK)
            scores = None
            for mask, s in hd["parts"]:
                scores = jnp.where(mask, s, 0.0 if scores is None else scores)
            st = st_ref[h]
            hd["o"] = _dot(hd["q_in"], st.astype(BF16)) + _dot(scores.astype(BF16), hd["v"])
            dec_rows = jnp.broadcast_to(ctx["dec"][:, hs], (HG_DV, HG_DK)).T
            st_ref[h] = st * dec_rows + _dot_tn(hd["k_dec"], hd["v"])

    def output_stage(ctx):
        rows = ctx["rows"]
        gate = p_ref[rows, _group("b_g")]
        gsil = gate * _sigmoid(gate)
        for h, hd in enumerate(ctx["heads"]):
            hs = slice(h * HG_DK, (h + 1) * HG_DK)
            o = hd["o"]
            o = o * lax.rsqrt(jnp.mean(o * o, axis=-1, keepdims=True) + EPS)
            y_ref[rows, A_WIDTH + h * HG_DV:A_WIDTH + (h + 1) * HG_DV] = (
                o * gain[:, hs] * gsil[:, hs]).astype(BF16)

    chunks_a = range(chunks_per_half)
    chunks_b = range(chunks_per_half, 2 * chunks_per_half)

    def is_gate_col(cs):
        return any(_group(n).start <= cs.start < _group(n).stop for n in ("a_b", "a_c", "a_h", "b_f"))

    gate_pieces = [cs for cs in in_pieces if is_gate_col(cs)]
    rest_pieces = [cs for cs in in_pieces if not is_gate_col(cs)]

    norm_b = _spread([functools.partial(_norm_mod_block, x_ref, hb_ref, gs1, sh1, r)
                      for r in range(half, ts, ROW_BLOCK)], len(gate_pieces))
    for cs in gate_pieces:
        project(rows_a, cs)
        norm_b()

    work = _spread([functools.partial(project, rows_b, cs) for cs in gate_pieces],
                   2 * chunks_per_half)
    for ci in chunks_a:
        gates_chunk(ci, work)

    work = _spread([functools.partial(project, rows, cs)
                    for rows in (rows_a, rows_b) for cs in rest_pieces], 2 * chunks_per_half)
    for ci in chunks_b:
        gates_chunk(ci, work)
    fast_ok = jnp.max(worst_ref[...]) < HG_EXP_GUARD

    def recur_tile(fast):
        n = 2 * chunks_per_half
        ready = []
        ctx = {}

        def other_work():
            if ready:
                ready.pop(0)()

        for t in range(n + 2):
            if t < n:
                ctx[t] = scores_stage(t, fast)
            other_work()
            if 0 <= t - 1 < n:
                state_stage(ctx[t - 1])
            other_work()
            if 0 <= t - 2 < n:
                output_stage(ctx.pop(t - 2))
                if t - 2 == chunks_per_half - 1:
                    ready += [functools.partial(emit_out, rows_a, cs) for cs in out_pieces]
            other_work()
        for item in ready:
            item()

    pl.when(fast_ok)(functools.partial(recur_tile, True))
    pl.when(jnp.logical_not(fast_ok))(functools.partial(recur_tile, False))
    for cs in out_pieces:
        emit_out(rows_b, cs)


def _mix0_call(x, mod, ng, win, cw, cb, hglb, gain, wout, *, ts):
    bsz, seq, d = x.shape
    assert ts % (2 * HG_CHUNK) == 0
    assert hglb.shape == (3, B_WIDTH)
    const2 = lambda b, s: (0, 0)
    return pl.pallas_call(
        functools.partial(_mix0_kernel, ts=ts),
        grid=(bsz, seq // ts),
        in_specs=[
            pl.BlockSpec((None, ts, d), lambda b, s: (b, s, 0)),
            pl.BlockSpec((None, 6, d), lambda b, s: (b, 0, 0)),
            pl.BlockSpec((1, d), const2),
            pl.BlockSpec((d, IN0_COLS), const2, pipeline_mode=pl.Buffered(1)),
            pl.BlockSpec((3, A_WIDTH), const2),
            pl.BlockSpec((1, A_WIDTH), const2),
            pl.BlockSpec((3, B_WIDTH), const2),
            pl.BlockSpec((1, B_WIDTH), const2),
            pl.BlockSpec((d, d), const2, pipeline_mode=pl.Buffered(1)),
        ],
        out_specs=pl.BlockSpec((None, ts, d), lambda b, s: (b, s, 0)),
        out_shape=jax.ShapeDtypeStruct((bsz, seq, d), F32),
        scratch_shapes=[
            pltpu.VMEM((ts, d), BF16),
            pltpu.VMEM((ts, IN0_COLS), F32),
            pltpu.VMEM((ts, d), BF16),
            pltpu.VMEM((HG_HEADS, HG_DK, HG_DV), F32),
            pltpu.VMEM((SUBLANES, A_WIDTH), F32),
            pltpu.VMEM((ts, B_WIDTH), F32),
            pltpu.VMEM((ts, B_WIDTH), F32),
            pltpu.VMEM((ts, B_WIDTH), F32),
            pltpu.VMEM((HG_CHUNK, B_WIDTH), F32),
        ],
        compiler_params=pltpu.CompilerParams(
            dimension_semantics=("parallel", "arbitrary"),
            vmem_limit_bytes=VMEM_LIMIT_BYTES),
        name="mixer_conv_hgrn2",
    )(x, mod, ng, win, cw, cb, hglb, gain, wout)


def _mix1_kernel(x_ref, mod_ref, ng_ref, win_ref, bin_ref, lng_ref, lnb_ref, ws_ref,
                 bst_ref, wout_ref, o_ref, hb_ref, z_ref, y_ref, v_ref, *, ts):
    sh1 = mod_ref[0:1, :]
    sc1 = mod_ref[1:2, :]
    g1 = mod_ref[2:3, :]
    gs1 = ng_ref[...] * (1.0 + sc1)
    sub = min(GM_SUB, ts)
    _norm_mod_to(x_ref, hb_ref, gs1, sh1, sub)

    b_in = bin_ref[...]
    ln_g = lng_ref[...]
    ln_b = lnb_ref[...]
    bst = bst_ref[...]
    ti = lax.broadcasted_iota(jnp.int32, (GM_CHUNK, GM_CHUNK), 0)
    si = lax.broadcasted_iota(jnp.int32, (GM_CHUNK, GM_CHUNK), 1)
    w_tri = [jnp.where(si <= ti, ws_ref[g], 0.0).astype(BF16) for g in range(GM_GROUPS)]

    def gelu(t):
        half_t = 0.5 * t
        th = jnp.tanh(t * (_SQRT_2_OVER_PI + (_SQRT_2_OVER_PI * 0.044715) * (t * t)))
        return half_t + half_t * th

    def project(rows, cs):
        z_ref[rows, cs] = _dot(hb_ref[rows, :], win_ref[:, cs])

    def emit_out(rows, cs):
        o_ref[rows, cs] = x_ref[rows, cs] + g1[:, cs] * _dot(y_ref[rows, :], wout_ref[:, cs])

    def gate_chunk(ci, mxu_work):
        rows = slice(ci * GM_CHUNK, (ci + 1) * GM_CHUNK)
        groups = [slice(g * GM_GW, (g + 1) * GM_GW) for g in range(GM_GROUPS)]
        total = None
        for gs in groups:
            vs = slice(GM_WIDTH + gs.start, GM_WIDTH + gs.stop)
            v = gelu(z_ref[rows, vs] + b_in[:, vs])
            v_ref[rows, gs] = v
            part = jnp.sum(v, axis=-1, keepdims=True)
            total = part if total is None else total + part
            mxu_work()
        mu = total * (1.0 / GM_WIDTH)
        total = None
        for gs in groups:
            vc = v_ref[rows, gs] - mu
            v_ref[rows, gs] = vc
            part = jnp.sum(vc * vc, axis=-1, keepdims=True)
            total = part if total is None else total + part
        inv = lax.rsqrt(total * (1.0 / GM_WIDTH) + EPS)
        for g, gs in enumerate(groups):
            vn = ((v_ref[rows, gs] * inv) * ln_g[:, gs] + ln_b[:, gs]).astype(BF16)
            mixed = _dot(w_tri[g], vn) + bst[:, g:g + 1]
            u = gelu(z_ref[rows, gs] + b_in[:, gs])
            y_ref[rows, gs] = (u * mixed).astype(BF16)
            mxu_work()

    n_sub = ts // sub
    chunks_per_sub = sub // GM_CHUNK
    in_pieces = [slice(c, c + PROJ_PIECE) for c in range(0, 2 * GM_WIDTH, PROJ_PIECE)]
    out_pieces = [slice(c, c + PROJ_PIECE) for c in range(0, x_ref.shape[1], PROJ_PIECE)]

    def rows_of(i):
        return slice(i * sub, (i + 1) * sub)

    def side_items(t):
        lists = []
        if t < n_sub:
            lists.append([functools.partial(project, rows_of(t), cs) for cs in in_pieces])
        if t + 1 < n_sub:
            lists.append([functools.partial(_norm_mod_block, x_ref, hb_ref, gs1, sh1, r)
                          for r in range((t + 1) * sub, (t + 2) * sub, ROW_BLOCK)])
        if 0 <= t - 2 < n_sub:
            lists.append([functools.partial(emit_out, rows_of(t - 2), cs) for cs in out_pieces])
        return _interleave(lists)

    for t in range(n_sub + 2):
        items = side_items(t)
        if 0 <= t - 1 < n_sub:
            work = _spread(items, 2 * GM_GROUPS * chunks_per_sub)
            for ci in range((t - 1) * chunks_per_sub, t * chunks_per_sub):
                gate_chunk(ci, work)
        else:
            for item in items:
                item()


def _mix1_call(x, mod, ng, win, b_in, ln_g, ln_b, ws, bst, wout, *, ts):
    bsz, seq, d = x.shape
    const2 = lambda b, s: (0, 0)
    return pl.pallas_call(
        functools.partial(_mix1_kernel, ts=ts),
        grid=(bsz, seq // ts),
        in_specs=[
            pl.BlockSpec((None, ts, d), lambda b, s: (b, s, 0)),
            pl.BlockSpec((None, 6, d), lambda b, s: (b, 0, 0)),
            pl.BlockSpec((1, d), const2),
            pl.BlockSpec((d, 2 * GM_WIDTH), const2, pipeline_mode=pl.Buffered(1)),
            pl.BlockSpec((1, 2 * GM_WIDTH), const2),
            pl.BlockSpec((1, GM_WIDTH), const2),
            pl.BlockSpec((1, GM_WIDTH), const2),
            pl.BlockSpec((GM_GROUPS, GM_CHUNK, GM_CHUNK), lambda b, s: (0, 0, 0)),
            pl.BlockSpec((GM_CHUNK, GM_GROUPS), const2),
            pl.BlockSpec((GM_WIDTH, d), const2, pipeline_mode=pl.Buffered(1)),
        ],
        out_specs=pl.BlockSpec((None, ts, d), lambda b, s: (b, s, 0)),
        out_shape=jax.ShapeDtypeStruct((bsz, seq, d), F32),
        scratch_shapes=[
            pltpu.VMEM((ts, d), BF16),
            pltpu.VMEM((ts, 2 * GM_WIDTH), F32),
            pltpu.VMEM((ts, GM_WIDTH), BF16),
            pltpu.VMEM((ts, GM_WIDTH), F32),
        ],
        compiler_params=pltpu.CompilerParams(
            dimension_semantics=("parallel", "parallel"),
            vmem_limit_bytes=VMEM_LIMIT_BYTES),
        name="mixer_spatial_gating",
    )(x, mod, ng, win, b_in, ln_g, ln_b, ws, bst, wout)


def _ffn_kernel(x_ref, mod_ref, ng_ref, w1_ref, w2_ref, fg_ref, o_ref, hb_ref, *, ts, final):
    sh2 = mod_ref[3:4, :]
    sc2 = mod_ref[4:5, :]
    g2 = mod_ref[5:6, :]
    gs2 = ng_ref[...] * (1.0 + sc2)

    sub = min(FFN_SUB, ts)
    _norm_mod_to(x_ref, hb_ref, gs2, sh2, sub)
    fcols = D_MODEL
    n_f = D_FF // fcols
    for r0 in range(0, ts, sub):
        rows = slice(r0, r0 + sub)
        between = _spread([functools.partial(_norm_mod_block, x_ref, hb_ref, gs2, sh2, r)
                           for r in range(r0 + sub, min(r0 + 2 * sub, ts), ROW_BLOCK)], n_f)
        acc = None
        for j in range(n_f):
            hid = jnp.maximum(_dot(hb_ref[rows, :], w1_ref[:, j * fcols:(j + 1) * fcols]), 0.0)
            between()
            part = _dot((hid * hid).astype(BF16), w2_ref[j * fcols:(j + 1) * fcols, :])
            acc = part if acc is None else acc + part
        out = x_ref[rows, :] + g2 * acc
        if final:
            inv = lax.rsqrt(jnp.mean(out * out, axis=-1, keepdims=True) + EPS)
            out = (out * inv) * fg_ref[...]
        o_ref[rows, :] = out


def _ffn_call(x, mod, ng, w1, w2, fg, *, ts, final):
    bsz, seq, d = x.shape
    const2 = lambda b, s: (0, 0)
    return pl.pallas_call(
        functools.partial(_ffn_kernel, ts=ts, final=final),
        grid=(bsz, seq // ts),
        in_specs=[
            pl.BlockSpec((None, ts, d), lambda b, s: (b, s, 0)),
            pl.BlockSpec((None, 6, d), lambda b, s: (b, 0, 0)),
            pl.BlockSpec((1, d), const2),
            pl.BlockSpec((d, D_FF), const2, pipeline_mode=pl.Buffered(1)),
            pl.BlockSpec((D_FF, d), const2, pipeline_mode=pl.Buffered(1)),
            pl.BlockSpec((1, d), const2),
        ],
        out_specs=pl.BlockSpec((None, ts, d), lambda b, s: (b, s, 0)),
        out_shape=jax.ShapeDtypeStruct((bsz, seq, d), F32),
        scratch_shapes=[pltpu.VMEM((ts, d), BF16)],
        compiler_params=pltpu.CompilerParams(
            dimension_semantics=("parallel", "parallel"),
            vmem_limit_bytes=VMEM_LIMIT_BYTES),
        name="ffn_final" if final else "ffn",
    )(x, mod, ng, w1, w2, fg)


def kernel(x, c, ada_w, ada_b, norm_mix_g, norm_ffn_g, w_in0, conv_w, conv_b, hg_lb, hg_gain,
           w_out0, w_in1, b_in1, gm_ln_g, gm_ln_b, gm_ws, gm_bs, w_out1, w_ff1, w_ff2, final_g):
    bsz, seq, d = x.shape
    ts = min(SEQ_TILE, seq)
    mod = _ada_call(c, ada_w, ada_b).reshape(ada_w.shape[0], bsz, 6, d)
    fg = final_g.reshape(1, d)

    x = _mix0_call(x, mod[0], norm_mix_g[0:1], w_in0[0].astype(BF16), conv_w[0], conv_b[0:1],
                   hg_lb, hg_gain[0:1], w_out0[0].astype(BF16), ts=ts)
    ts_ffn = min(FFN_TILE, seq)
    x = _ffn_call(x, mod[0], norm_ffn_g[0:1], w_ff1[0].astype(BF16), w_ff2[0].astype(BF16), fg,
                  ts=ts_ffn, final=False)
    x = _mix1_call(x, mod[1], norm_mix_g[1:2], w_in1[0].astype(BF16), b_in1[0:1], gm_ln_g[0:1],
                   gm_ln_b[0:1], gm_ws[0], gm_bs[0].T, w_out1[0].astype(BF16),
                   ts=min(GM_TILE, seq))
    x = _ffn_call(x, mod[1], norm_ffn_g[1:2], w_ff1[1].astype(BF16), w_ff2[1].astype(BF16), fg,
                  ts=ts_ffn, final=True)
    return x
```

```python
import functools

import numpy as np
import jax
import jax.numpy as jnp
from jax import lax
from jax.experimental import pallas as pl
from jax.experimental.pallas import tpu as pltpu

F32 = jnp.float32
BF16 = jnp.bfloat16

EPS = 1e-6
D_MODEL = 1024
D_FF = 4 * D_MODEL
A_WIDTH = 512
B_WIDTH = 512
HG_HEADS = 4
HG_DK = 128
HG_DV = 128
IN0_COLS = 3 * A_WIDTH + 4 * B_WIDTH
IN0_GROUPS = ("a_b", "a_c", "a_h", "b_q", "b_f", "b_i", "b_g")
GM_WIDTH = D_MODEL
GM_GROUPS = 4
GM_CHUNK = 128
GM_GW = GM_WIDTH // GM_GROUPS

SUBLANES = 8
VMEM_LIMIT_BYTES = 56 * 1024 * 1024

ADA_TN = 1536
SEQ_TILE = 1024
GM_TILE = 1024
GM_SUB = 256
FFN_TILE = 1024
FFN_SUB = 256
ROW_BLOCK = 64
HG_CHUNK = 64
HG_LEVELS = (32, 16, 8, 4, 2, 1)
HG_EXP_GUARD = 80.0
PROJ_PIECE = 256

_SQRT_2_OVER_PI = float(np.sqrt(2.0 / np.pi))


def _group(name):
    assert A_WIDTH == B_WIDTH
    j = IN0_GROUPS.index(name)
    return slice(j * A_WIDTH, (j + 1) * A_WIDTH)


def _sigmoid(v):
    return 1.0 / (1.0 + jnp.exp(-v))


def _dot(a, b):
    return jnp.dot(a, b, preferred_element_type=F32)


def _dot_nt(a, b):
    return lax.dot_general(a, b, (((1,), (1,)), ((), ())), preferred_element_type=F32)


def _dot_tn(a, b):
    return lax.dot_general(a, b, (((0,), (0,)), ((), ())), preferred_element_type=F32)


def _norm_mod_block(x_ref, hb_ref, gs, sh, r):
    xt = x_ref[pl.ds(r, ROW_BLOCK), :]
    inv = lax.rsqrt(jnp.mean(xt * xt, axis=-1, keepdims=True) + EPS)
    hb_ref[pl.ds(r, ROW_BLOCK), :] = ((xt * inv) * gs + sh).astype(BF16)


def _norm_mod_to(x_ref, hb_ref, gs, sh, rows):
    def body(i, carry):
        _norm_mod_block(x_ref, hb_ref, gs, sh, pl.multiple_of(i * ROW_BLOCK, ROW_BLOCK))
        return carry

    lax.fori_loop(0, rows // ROW_BLOCK, body, 0, unroll=True)


def _split_bf16(a):
    hi = a.astype(BF16)
    return hi, (a - hi.astype(F32)).astype(BF16)


def _ada_kernel(c_ref, w_ref, b_ref, o_ref):
    c = c_ref[...]
    bsz = c.shape[0]
    s_hi, s_lo = _split_bf16(c * _sigmoid(c))
    w_hi, w_lo = _split_bf16(w_ref[...])
    both = _dot(jnp.concatenate([s_hi, s_lo], axis=0), w_hi)
    o_ref[...] = both[:bsz] + both[bsz:] + _dot(s_hi, w_lo) + b_ref[...]


def _ada_call(c, ada_w, ada_b):
    depth, d, e = ada_w.shape
    bsz = c.shape[0]
    tn = ADA_TN
    return pl.pallas_call(
        _ada_kernel,
        grid=(depth, e // tn),
        in_specs=[
            pl.BlockSpec((bsz, d), lambda l, j: (0, 0)),
            pl.BlockSpec((None, d, tn), lambda l, j: (l, 0, j)),
            pl.BlockSpec((None, 1, tn), lambda l, j: (l, 0, j)),
        ],
        out_specs=pl.BlockSpec((None, bsz, tn), lambda l, j: (l, 0, j)),
        out_shape=jax.ShapeDtypeStruct((depth, bsz, e), F32),
        compiler_params=pltpu.CompilerParams(
            dimension_semantics=("arbitrary", "arbitrary"),
            vmem_limit_bytes=VMEM_LIMIT_BYTES),
        name="ada_modulation",
    )(c, ada_w, ada_b.reshape(depth, 1, e))


def _level_masks():
    ti = lax.broadcasted_iota(jnp.int32, (HG_CHUNK, HG_CHUNK), 0)
    si = lax.broadcasted_iota(jnp.int32, (HG_CHUNK, HG_CHUNK), 1)
    masks = []
    for m in HG_LEVELS:
        other_blk = (ti ^ si) & ~(2 * m - 1)
        masks.append((other_blk | ((ti & m) ^ m) | (si & m)) == 0)
    return masks, ti == si, si <= ti


def _mid_row(c):
    return c[HG_CHUNK // 2 - 1:HG_CHUNK // 2, :]


def _block_boundary(c, m):
    rows, w = c.shape
    if m >= SUBLANES:
        blk = 2 * m
        return jnp.concatenate(
            [jnp.broadcast_to(c[b * blk + m - 1:b * blk + m, :], (blk, w))
             for b in range(rows // blk)], axis=0)
    c3 = c.reshape(rows // SUBLANES, SUBLANES, w)
    if m == 4:
        bnd = jnp.broadcast_to(c3[:, 3:4, :], c3.shape)
    else:
        assert m == 2
        sub = lax.broadcasted_iota(jnp.int32, c3.shape, 1)
        bnd = jnp.where(sub < 4, jnp.broadcast_to(c3[:, 1:2, :], c3.shape),
                        jnp.broadcast_to(c3[:, 5:6, :], c3.shape))
    return bnd.reshape(rows, w)


def _chunk_cumsum(x):
    rows, w = x.shape
    nv = rows // SUBLANES
    x3 = x.reshape(nv, SUBLANES, w)
    sub = lax.broadcasted_iota(jnp.int32, x3.shape, 1)
    sh = 1
    while sh < SUBLANES:
        x3 = x3 + jnp.where(sub >= sh, pltpu.roll(x3, sh, axis=1), 0.0)
        sh *= 2
    tot = jnp.broadcast_to(x3[:, SUBLANES - 1:SUBLANES, :], x3.shape)
    outs = [x3[0]]
    acc = tot[0]
    for j in range(1, nv):
        outs.append(x3[j] + acc)
        if j + 1 < nv:
            acc = acc + tot[j]
    return jnp.concatenate(outs, axis=0)


def _spread(items, slots):
    state = {"calls": 0, "done": 0}

    def run_due():
        state["calls"] += 1
        want = min(len(items), -(-len(items) * state["calls"] // slots))
        while state["done"] < want:
            items[state["done"]]()
            state["done"] += 1
    return run_due


def _interleave(lists):
    keyed = [((i + 0.5) / len(l), n, item) for n, l in enumerate(lists) for i, item in enumerate(l)]
    return [item for _, _, item in sorted(keyed, key=lambda k: k[:2])]


def _mix0_kernel(x_ref, mod_ref, ng_ref, win_ref, cw_ref, cb_ref, hglb_ref, gain_ref, wout_ref,
                 o_ref, hb_ref, p_ref, y_ref, st_ref, zc_ref, cum_ref, ck_ref, f_ref, worst_ref,
                 *, ts):
    @pl.when(pl.program_id(1) == 0)
    def _():
        st_ref[...] = jnp.zeros_like(st_ref)
        zc_ref[...] = jnp.zeros_like(zc_ref)

    half = ts // 2
    rows_a, rows_b = slice(0, half), slice(half, ts)
    chunks_per_half = half // HG_CHUNK
    sh1 = mod_ref[0:1, :]
    gs1 = ng_ref[...] * (1.0 + mod_ref[1:2, :])
    g1 = mod_ref[2:3, :]
    _norm_mod_to(x_ref, hb_ref, gs1, sh1, half)

    a0, a1, a2 = hglb_ref[0:1, :], hglb_ref[1:2, :], hglb_ref[2:3, :]
    amax = jnp.maximum(jnp.maximum(a0, a1), a2)
    e0, e1, e2 = jnp.exp(a0 - amax), jnp.exp(a1 - amax), jnp.exp(a2 - amax)
    lb = e0 / (e0 + e1 + e2)

    cw0, cw1, cw2 = cw_ref[0:1, :], cw_ref[1:2, :], cw_ref[2:3, :]
    cb = cb_ref[...]
    gain = gain_ref[...]
    row8 = lax.broadcasted_iota(jnp.int32, (SUBLANES, A_WIDTH), 0)
    masks, diag, causal = _level_masks()

    in_pieces = [slice(c, c + PROJ_PIECE) for c in range(0, IN0_COLS, PROJ_PIECE)]
    out_pieces = [slice(c, c + PROJ_PIECE) for c in range(0, D_MODEL, PROJ_PIECE)]

    def project(rows, cs):
        p_ref[rows, cs] = _dot(hb_ref[rows, :], win_ref[:, cs])

    def emit_out(rows, cs):
        o_ref[rows, cs] = x_ref[rows, cs] + g1[:, cs] * _dot(y_ref[rows, :], wout_ref[:, cs])

    def gates_chunk(ci, other_work):
        rows = slice(ci * HG_CHUNK, (ci + 1) * HG_CHUNK)

        def col(name):
            return p_ref[rows, _group(name)]

        z = col("a_c") * col("a_h")
        prev = zc_ref[...]
        zc_ref[...] = z[HG_CHUNK - SUBLANES:, :]

        def shifted(k):
            zr = pltpu.roll(z, k, axis=0)
            pr = pltpu.roll(prev, k, axis=0)
            head = jnp.where(row8 < k, pr, zr[:SUBLANES])
            return jnp.concatenate([head, zr[SUBLANES:]], axis=0)

        conv = cb + shifted(2) * cw0
        conv = conv + shifted(1) * cw1
        conv = conv + z * cw2
        y_ref[rows, 0:A_WIDTH] = (col("a_b") * conv).astype(BF16)
        other_work()

        f = lb + (1.0 - lb) * _sigmoid(col("b_f"))
        cum = _chunk_cumsum(jnp.log(f))
        ck = cum - jnp.log(1.0 - f)
        f_ref[rows, :] = f
        cum_ref[rows, :] = cum
        ck_ref[rows, :] = ck
        mid = _mid_row(cum)
        e = jnp.maximum(mid - ck, -mid)
        worst_ref[...] = e if ci == 0 else jnp.maximum(worst_ref[...], e)
        other_work()


    def scores_stage(ci, fast):
        if isinstance(ci, int):
            rows = slice(ci * HG_CHUNK, (ci + 1) * HG_CHUNK)
        else:
            rows = pl.ds(pl.multiple_of(ci * HG_CHUNK, HG_CHUNK), HG_CHUNK)
        q = p_ref[rows, _group("b_q")]
        v = p_ref[rows, _group("b_i")]
        cum = cum_ref[rows, :]
        ck = ck_ref[rows, :]
        last = cum[HG_CHUNK - 1:HG_CHUNK, :]
        q_in = q * jnp.exp(cum)
        k_dec = jnp.exp(last - ck)
        if not fast:
            f = f_ref[rows, :]
            kk = 1.0 - f
            qf = q * f

        heads = []
        for h in range(HG_HEADS):
            hs = slice(h * HG_DK, (h + 1) * HG_DK)
            q_h, cum_h, ck_h = q[:, hs], cum[:, hs], ck[:, hs]
            if fast:
                mid = _mid_row(cum_h)
                qt = (q_h * jnp.exp(cum_h - mid)).astype(BF16)
                kt = jnp.exp(mid - ck_h).astype(BF16)
                parts = [(causal, _dot_nt(qt, kt))]
                levels = []
            else:
                k_b = kk[:, hs].astype(BF16)
                parts = [(diag, _dot_nt(q_h.astype(BF16), k_b))]
                levels = list(zip(HG_LEVELS, masks))
            for m, mask in levels:
                if m == 1:
                    qt, kt = qf[:, hs].astype(BF16), k_b
                else:
                    bnd = _block_boundary(cum_h, m)
                    qt = (q_h * jnp.exp(cum_h - bnd)).astype(BF16)
                    kt = jnp.exp(bnd - ck_h).astype(BF16)
                parts.append((mask, _dot_nt(qt, kt)))
            heads.append(dict(parts=parts, q_in=q_in[:, hs].astype(BF16),
                              k_dec=k_dec[:, hs].astype(BF16), v=v[:, hs].astype(BF16)))
        return dict(rows=rows, heads=heads, dec=jnp.exp(last))

    def state_stage(ctx):
        for h, hd in enumerate(ctx["heads"]):
            hs = slice(h * HG_DK, (h + 1) * HG_DK)
            scores = None
            for mask, s in hd["parts"]:
                scores = jnp.where(mask, s, 0.0 if scores is None else scores)
            st = st_ref[h]
            hd["o"] = _dot(hd["q_in"], st.astype(BF16)) + _dot(scores.astype(BF16), hd["v"])
            dec_rows = jnp.broadcast_to(ctx["dec"][:, hs], (HG_DV, HG_DK)).T
            st_ref[h] = st * dec_rows + _dot_tn(hd["k_dec"], hd["v"])

    def output_stage(ctx):
        rows = ctx["rows"]
        gate = p_ref[rows, _group("b_g")]
        gsil = gate * _sigmoid(gate)
        for h, hd in enumerate(ctx["heads"]):
            hs = slice(h * HG_DK, (h + 1) * HG_DK)
            o = hd["o"]
            o = o * lax.rsqrt(jnp.mean(o * o, axis=-1, keepdims=True) + EPS)
            y_ref[rows, A_WIDTH + h * HG_DV:A_WIDTH + (h + 1) * HG_DV] = (
                o * gain[:, hs] * gsil[:, hs]).astype(BF16)

    chunks_a = range(chunks_per_half)
    chunks_b = range(chunks_per_half, 2 * chunks_per_half)

    def is_gate_col(cs):
        return any(_group(n).start <= cs.start < _group(n).stop for n in ("a_b", "a_c", "a_h", "b_f"))

    gate_pieces = [cs for cs in in_pieces if is_gate_col(cs)]
    rest_pieces = [cs for cs in in_pieces if not is_gate_col(cs)]

    norm_b = _spread([functools.partial(_norm_mod_block, x_ref, hb_ref, gs1, sh1, r)
                      for r in range(half, ts, ROW_BLOCK)], len(gate_pieces))
    for cs in gate_pieces:
        project(rows_a, cs)
        norm_b()

    work = _spread([functools.partial(project, rows_b, cs) for cs in gate_pieces],
                   2 * chunks_per_half)
    for ci in chunks_a:
        gates_chunk(ci, work)

    work = _spread([functools.partial(project, rows, cs)
                    for rows in (rows_a, rows_b) for cs in rest_pieces], 2 * chunks_per_half)
    for ci in chunks_b:
        gates_chunk(ci, work)
    fast_ok = jnp.max(worst_ref[...]) < HG_EXP_GUARD

    def recur_tile(fast):
        n = 2 * chunks_per_half
        if not fast:
            def one_chunk(ci, carry):
                c = scores_stage(ci, False)
                state_stage(c)
                output_stage(c)
                return carry

            lax.fori_loop(0, n, one_chunk, 0)
            for cs in out_pieces:
                emit_out(rows_a, cs)
            return
        ready = []
        ctx = {}

        def other_work():
            if ready:
                ready.pop(0)()

        for t in range(n + 2):
            if t < n:
                ctx[t] = scores_stage(t, fast)
            other_work()
            if 0 <= t - 1 < n:
                state_stage(ctx[t - 1])
            other_work()
            if 0 <= t - 2 < n:
                output_stage(ctx.pop(t - 2))
                if t - 2 == chunks_per_half - 1:
                    ready += [functools.partial(emit_out, rows_a, cs) for cs in out_pieces]
            other_work()
        for item in ready:
            item()

    pl.when(fast_ok)(functools.partial(recur_tile, True))
    pl.when(jnp.logical_not(fast_ok))(functools.partial(recur_tile, False))
    for cs in out_pieces:
        emit_out(rows_b, cs)


def _mix0_call(x, mod, ng, win, cw, cb, hglb, gain, wout, *, ts):
    bsz, seq, d = x.shape
    assert ts % (2 * HG_CHUNK) == 0
    assert hglb.shape == (3, B_WIDTH)
    const2 = lambda b, s: (0, 0)
    return pl.pallas_call(
        functools.partial(_mix0_kernel, ts=ts),
        grid=(bsz, seq // ts),
        in_specs=[
            pl.BlockSpec((None, ts, d), lambda b, s: (b, s, 0)),
            pl.BlockSpec((None, 6, d), lambda b, s: (b, 0, 0)),
            pl.BlockSpec((1, d), const2),
            pl.BlockSpec((d, IN0_COLS), const2, pipeline_mode=pl.Buffered(1)),
            pl.BlockSpec((3, A_WIDTH), const2),
            pl.BlockSpec((1, A_WIDTH), const2),
            pl.BlockSpec((3, B_WIDTH), const2),
            pl.BlockSpec((1, B_WIDTH), const2),
            pl.BlockSpec((d, d), const2, pipeline_mode=pl.Buffered(1)),
        ],
        out_specs=pl.BlockSpec((None, ts, d), lambda b, s: (b, s, 0)),
        out_shape=jax.ShapeDtypeStruct((bsz, seq, d), F32),
        scratch_shapes=[
            pltpu.VMEM((ts, d), BF16),
            pltpu.VMEM((ts, IN0_COLS), F32),
            pltpu.VMEM((ts, d), BF16),
            pltpu.VMEM((HG_HEADS, HG_DK, HG_DV), F32),
            pltpu.VMEM((SUBLANES, A_WIDTH), F32),
            pltpu.VMEM((ts, B_WIDTH), F32),
            pltpu.VMEM((ts, B_WIDTH), F32),
            pltpu.VMEM((ts, B_WIDTH), F32),
            pltpu.VMEM((HG_CHUNK, B_WIDTH), F32),
        ],
        compiler_params=pltpu.CompilerParams(
            dimension_semantics=("parallel", "arbitrary"),
            vmem_limit_bytes=VMEM_LIMIT_BYTES),
        name="mixer_conv_hgrn2",
    )(x, mod, ng, win, cw, cb, hglb, gain, wout)


def _mix1_kernel(x_ref, mod_ref, ng_ref, win_ref, bin_ref, lng_ref, lnb_ref, ws_ref,
                 bst_ref, wout_ref, o_ref, hb_ref, z_ref, y_ref, v_ref, *, ts):
    sh1 = mod_ref[0:1, :]
    sc1 = mod_ref[1:2, :]
    g1 = mod_ref[2:3, :]
    gs1 = ng_ref[...] * (1.0 + sc1)
    sub = min(GM_SUB, ts)
    _norm_mod_to(x_ref, hb_ref, gs1, sh1, sub)

    b_in = bin_ref[...]
    ln_g = lng_ref[...]
    ln_b = lnb_ref[...]
    bst = bst_ref[...]
    ti = lax.broadcasted_iota(jnp.int32, (GM_CHUNK, GM_CHUNK), 0)
    si = lax.broadcasted_iota(jnp.int32, (GM_CHUNK, GM_CHUNK), 1)
    w_tri = [jnp.where(si <= ti, ws_ref[g], 0.0).astype(BF16) for g in range(GM_GROUPS)]

    def gelu(t):
        half_t = 0.5 * t
        th = jnp.tanh(t * (_SQRT_2_OVER_PI + (_SQRT_2_OVER_PI * 0.044715) * (t * t)))
        return half_t + half_t * th

    def project(rows, cs):
        z_ref[rows, cs] = _dot(hb_ref[rows, :], win_ref[:, cs])

    def emit_out(rows, cs):
        o_ref[rows, cs] = x_ref[rows, cs] + g1[:, cs] * _dot(y_ref[rows, :], wout_ref[:, cs])

    def gate_chunk(ci, mxu_work):
        rows = slice(ci * GM_CHUNK, (ci + 1) * GM_CHUNK)
        groups = [slice(g * GM_GW, (g + 1) * GM_GW) for g in range(GM_GROUPS)]
        total = None
        for gs in groups:
            vs = slice(GM_WIDTH + gs.start, GM_WIDTH + gs.stop)
            v = gelu(z_ref[rows, vs] + b_in[:, vs])
            v_ref[rows, gs] = v
            part = jnp.sum(v, axis=-1, keepdims=True)
            total = part if total is None else total + part
            mxu_work()
        mu = total * (1.0 / GM_WIDTH)
        total = None
        for gs in groups:
            vc = v_ref[rows, gs] - mu
            v_ref[rows, gs] = vc
            part = jnp.sum(vc * vc, axis=-1, keepdims=True)
            total = part if total is None else total + part
        inv = lax.rsqrt(total * (1.0 / GM_WIDTH) + EPS)
        for g, gs in enumerate(groups):
            vn = ((v_ref[rows, gs] * inv) * ln_g[:, gs] + ln_b[:, gs]).astype(BF16)
            mixed = _dot(w_tri[g], vn) + bst[:, g:g + 1]
            u = gelu(z_ref[rows, gs] + b_in[:, gs])
            y_ref[rows, gs] = (u * mixed).astype(BF16)
            mxu_work()

    n_sub = ts // sub
    chunks_per_sub = sub // GM_CHUNK
    in_pieces = [slice(c, c + PROJ_PIECE) for c in range(0, 2 * GM_WIDTH, PROJ_PIECE)]
    out_pieces = [slice(c, c + PROJ_PIECE) for c in range(0, x_ref.shape[1], PROJ_PIECE)]

    def rows_of(i):
        return slice(i * sub, (i + 1) * sub)

    def side_items(t):
        lists = []
        if t < n_sub:
            lists.append([functools.partial(project, rows_of(t), cs) for cs in in_pieces])
        if t + 1 < n_sub:
            lists.append([functools.partial(_norm_mod_block, x_ref, hb_ref, gs1, sh1, r)
                          for r in range((t + 1) * sub, (t + 2) * sub, ROW_BLOCK)])
        if 0 <= t - 2 < n_sub:
            lists.append([functools.partial(emit_out, rows_of(t - 2), cs) for cs in out_pieces])
        return _interleave(lists)

    for t in range(n_sub + 2):
        items = side_items(t)
        if 0 <= t - 1 < n_sub:
            work = _spread(items, 2 * GM_GROUPS * chunks_per_sub)
            for ci in range((t - 1) * chunks_per_sub, t * chunks_per_sub):
                gate_chunk(ci, work)
        else:
            for item in items:
                item()


def _mix1_call(x, mod, ng, win, b_in, ln_g, ln_b, ws, bst, wout, *, ts):
    bsz, seq, d = x.shape
    const2 = lambda b, s: (0, 0)
    return pl.pallas_call(
        functools.partial(_mix1_kernel, ts=ts),
        grid=(bsz, seq // ts),
        in_specs=[
            pl.BlockSpec((None, ts, d), lambda b, s: (b, s, 0)),
            pl.BlockSpec((None, 6, d), lambda b, s: (b, 0, 0)),
            pl.BlockSpec((1, d), const2),
            pl.BlockSpec((d, 2 * GM_WIDTH), const2, pipeline_mode=pl.Buffered(1)),
            pl.BlockSpec((1, 2 * GM_WIDTH), const2),
            pl.BlockSpec((1, GM_WIDTH), const2),
            pl.BlockSpec((1, GM_WIDTH), const2),
            pl.BlockSpec((GM_GROUPS, GM_CHUNK, GM_CHUNK), lambda b, s: (0, 0, 0)),
            pl.BlockSpec((GM_CHUNK, GM_GROUPS), const2),
            pl.BlockSpec((GM_WIDTH, d), const2, pipeline_mode=pl.Buffered(1)),
        ],
        out_specs=pl.BlockSpec((None, ts, d), lambda b, s: (b, s, 0)),
        out_shape=jax.ShapeDtypeStruct((bsz, seq, d), F32),
        scratch_shapes=[
            pltpu.VMEM((ts, d), BF16),
            pltpu.VMEM((ts, 2 * GM_WIDTH), F32),
            pltpu.VMEM((ts, GM_WIDTH), BF16),
            pltpu.VMEM((ts, GM_WIDTH), F32),
        ],
        compiler_params=pltpu.CompilerParams(
            dimension_semantics=("parallel", "parallel"),
            vmem_limit_bytes=VMEM_LIMIT_BYTES),
        name="mixer_spatial_gating",
    )(x, mod, ng, win, b_in, ln_g, ln_b, ws, bst, wout)


def _ffn_kernel(x_ref, mod_ref, ng_ref, w1_ref, w2_ref, fg_ref, o_ref, hb_ref, *, ts, final):
    sh2 = mod_ref[3:4, :]
    sc2 = mod_ref[4:5, :]
    g2 = mod_ref[5:6, :]
    gs2 = ng_ref[...] * (1.0 + sc2)

    sub = min(FFN_SUB, ts)
    _norm_mod_to(x_ref, hb_ref, gs2, sh2, sub)
    fcols = D_MODEL
    n_f = D_FF // fcols
    for r0 in range(0, ts, sub):
        rows = slice(r0, r0 + sub)
        between = _spread([functools.partial(_norm_mod_block, x_ref, hb_ref, gs2, sh2, r)
                           for r in range(r0 + sub, min(r0 + 2 * sub, ts), ROW_BLOCK)], n_f)
        acc = None
        for j in range(n_f):
            hid = jnp.maximum(_dot(hb_ref[rows, :], w1_ref[:, j * fcols:(j + 1) * fcols]), 0.0)
            between()
            part = _dot((hid * hid).astype(BF16), w2_ref[j * fcols:(j + 1) * fcols, :])
            acc = part if acc is None else acc + part
        out = x_ref[rows, :] + g2 * acc
        if final:
            inv = lax.rsqrt(jnp.mean(out * out, axis=-1, keepdims=True) + EPS)
            out = (out * inv) * fg_ref[...]
        o_ref[rows, :] = out


def _ffn_call(x, mod, ng, w1, w2, fg, *, ts, final):
    bsz, seq, d = x.shape
    const2 = lambda b, s: (0, 0)
    return pl.pallas_call(
        functools.partial(_ffn_kernel, ts=ts, final=final),
        grid=(bsz, seq // ts),
        in_specs=[
            pl.BlockSpec((None, ts, d), lambda b, s: (b, s, 0)),
            pl.BlockSpec((None, 6, d), lambda b, s: (b, 0, 0)),
            pl.BlockSpec((1, d), const2),
            pl.BlockSpec((d, D_FF), const2, pipeline_mode=pl.Buffered(1)),
            pl.BlockSpec((D_FF, d), const2, pipeline_mode=pl.Buffered(1)),
            pl.BlockSpec((1, d), const2),
        ],
        out_specs=pl.BlockSpec((None, ts, d), lambda b, s: (b, s, 0)),
        out_shape=jax.ShapeDtypeStruct((bsz, seq, d), F32),
        scratch_shapes=[pltpu.VMEM((ts, d), BF16)],
        compiler_params=pltpu.CompilerParams(
            dimension_semantics=("parallel", "parallel"),
            vmem_limit_bytes=VMEM_LIMIT_BYTES),
        name="ffn_final" if final else "ffn",
    )(x, mod, ng, w1, w2, fg)


def kernel(x, c, ada_w, ada_b, norm_mix_g, norm_ffn_g, w_in0, conv_w, conv_b, hg_lb, hg_gain,
           w_out0, w_in1, b_in1, gm_ln_g, gm_ln_b, gm_ws, gm_bs, w_out1, w_ff1, w_ff2, final_g):
    bsz, seq, d = x.shape
    ts = min(SEQ_TILE, seq)
    mod = _ada_call(c, ada_w, ada_b).reshape(ada_w.shape[0], bsz, 6, d)
    fg = final_g.reshape(1, d)

    x = _mix0_call(x, mod[0], norm_mix_g[0:1], w_in0[0].astype(BF16), conv_w[0], conv_b[0:1],
                   hg_lb, hg_gain[0:1], w_out0[0].astype(BF16), ts=ts)
    ts_ffn = min(FFN_TILE, seq)
    x = _ffn_call(x, mod[0], norm_ffn_g[0:1], w_ff1[0].astype(BF16), w_ff2[0].astype(BF16), fg,
                  ts=ts_ffn, final=False)
    x = _mix1_call(x, mod[1], norm_mix_g[1:2], w_in1[0].astype(BF16), b_in1[0:1], gm_ln_g[0:1],
                   gm_ln_b[0:1], gm_ws[0], gm_bs[0].T, w_out1[0].astype(BF16),
                   ts=min(GM_TILE, seq))
    x = _ffn_call(x, mod[1], norm_ffn_g[1:2], w_ff1[1].astype(BF16), w_ff2[1].astype(BF16), fg,
                  ts=ts_ffn, final=True)
    return x
```
